```python
import math
import jax, jax.numpy as jnp
from jax import lax
import numpy as np

D_MODEL = 1024
BATCH = 8
SEQ = 4096
DEPTH = 2
DEC_BATCH = 8
DEC_SEQ = 16
PAST_LEN = 1024

CHUNK = 64
N_EVEN = (DEPTH + 1) // 2
N_ODD = DEPTH // 2
N_HEADS = 8
N_KV_HEADS = 2
Q_PER_KV = N_HEADS // N_KV_HEADS
HEAD_DIM = 64
ATTN_WIDTH = N_HEADS * HEAD_DIM
KV_WIDTH = N_KV_HEADS * HEAD_DIM
WINDOW = 128
WINDOW_CHUNKS = WINDOW // CHUNK
N_BUCKETS = 32
MAX_DISTANCE = 128
CONV_WIDTH = D_MODEL // 2
CONV_K = 3
IN_WIDTH = ATTN_WIDTH + 2 * KV_WIDTH + 3 * CONV_WIDTH
MIX_WIDTH = ATTN_WIDTH + CONV_WIDTH
SPLIT_POINTS = (ATTN_WIDTH, ATTN_WIDTH + KV_WIDTH, ATTN_WIDTH + 2 * KV_WIDTH,
                ATTN_WIDTH + 2 * KV_WIDTH + CONV_WIDTH, ATTN_WIDTH + 2 * KV_WIDTH + 2 * CONV_WIDTH)
POOL_SIZES = (2, 4, 8, 16)
POOL_GROUPS = len(POOL_SIZES)
POOL_GROUP = D_MODEL // POOL_GROUPS
POOL_MAX = max(POOL_SIZES)
D_FF = 2816
EPS = 1e-6
NEG = -1e30

kernel_name = "hybrid_streaming_encoder_step"


def rms_norm(x, g):
    xf = x.astype(jnp.float32)
    y = xf * lax.rsqrt(jnp.mean(xf * xf, axis=-1, keepdims=True) + EPS)
    return (y * g.astype(jnp.float32)).astype(x.dtype)


def swiglu(h, w_gate, w_up, w_down):
    return (jax.nn.silu(h @ w_gate) * (h @ w_up)) @ w_down


def rel_bucket(rel):
    half = N_BUCKETS // 2
    ret = jnp.where(rel > 0, half, 0)
    n = jnp.abs(rel)
    max_exact = half // 2
    nf = jnp.maximum(n, 1).astype(jnp.float32)
    large = max_exact + (jnp.log(nf / max_exact) / math.log(MAX_DISTANCE / max_exact)
                         * (half - max_exact)).astype(jnp.int32)
    large = jnp.minimum(large, half - 1)
    return ret + jnp.where(n < max_exact, n, large)


def rel_bias(q_pos, k_pos, table):
    b = rel_bucket(k_pos[None, :] - q_pos[:, None])
    bias = jnp.take(table, b, axis=0).astype(jnp.float32)
    return jnp.transpose(bias, (2, 0, 1)).reshape(N_KV_HEADS, Q_PER_KV, q_pos.shape[0], k_pos.shape[0])


def attend(q, k, v, bias, key_valid, sinks):
    s = jnp.einsum('bnqhgd,bnkhd->bnhgqk', q, k).astype(jnp.float32) * (HEAD_DIM ** -0.5) + bias
    if key_valid is not None:
        s = jnp.where(key_valid[None, :, None, None, None, :], s, NEG)
    sink = sinks.astype(jnp.float32).reshape(N_KV_HEADS, Q_PER_KV)[:, :, None, None]
    m = jnp.maximum(jnp.max(s, axis=-1, keepdims=True), sink)
    p = jnp.exp(s - m)
    w = p / (jnp.sum(p, axis=-1, keepdims=True) + jnp.exp(sink - m))
    return jnp.einsum('bnhgqk,bnkhd->bnqhgd', w.astype(v.dtype), v)


def attn_prompt(q, k, v, sinks, table):
    B, S = q.shape[:2]
    nc = S // CHUNK
    kb_len = WINDOW + CHUNK
    qb = q.reshape(B, nc, CHUNK, N_KV_HEADS, Q_PER_KV, HEAD_DIM)

    def band(t):
        tp = jnp.pad(t, ((0, 0), (WINDOW, 0), (0, 0), (0, 0)))
        tp = tp.reshape(B, nc + WINDOW_CHUNKS, CHUNK, N_KV_HEADS, HEAD_DIM)
        return jnp.concatenate([tp[:, j:j + nc] for j in range(WINDOW_CHUNKS + 1)], axis=2)

    offs = jnp.arange(kb_len) - WINDOW
    bias = rel_bias(jnp.arange(CHUNK), offs, table)
    k_pos = jnp.arange(nc)[:, None] * CHUNK + offs[None, :]
    out = attend(qb, band(k), band(v), bias, k_pos >= 0, sinks)
    return out.reshape(B, S, ATTN_WIDTH)


def attn_sample(q, k, v, ck, cv, t0, sinks, table):
    B, T = q.shape[:2]
    kk = jnp.concatenate([ck, k], axis=1)
    vv = jnp.concatenate([cv, v], axis=1)
    q_pos = t0 + jnp.arange(T)
    k_pos = t0 - ck.shape[1] + jnp.arange(kk.shape[1])
    bias = rel_bias(q_pos, k_pos, table)
    out = attend(q[:, None], kk[:, None], vv[:, None], bias, None, sinks)
    return out.reshape(B, T, ATTN_WIDTH)


def short_conv(u, hist, w):
    T = u.shape[1]
    up = jnp.concatenate([hist, u], axis=1)
    y = sum(w[j] * up[:, j:j + T] for j in range(CONV_K))
    return y, up[:, -(CONV_K - 1):]


def pool_mix(h, hist, t0, w_pool, scale):
    B, T, _ = h.shape
    lh = hist.shape[1]
    hp = jnp.concatenate([hist, h], axis=1)
    cs = jnp.pad(jnp.cumsum(hp.astype(jnp.float32), axis=1), ((0, 0), (1, 0), (0, 0)))
    t = jnp.arange(T)
    end = lh + 1 + t
    hf = h.astype(jnp.float32)
    diffs = []
    for g, w in enumerate(POOL_SIZES):
        sl = slice(g * POOL_GROUP, (g + 1) * POOL_GROUP)
        cnt = jnp.minimum(t0 + t + 1, w)
        csg = cs[..., sl]
        mean = (jnp.take(csg, end, axis=1) - jnp.take(csg, end - cnt, axis=1)) / cnt[:, None].astype(jnp.float32)
        diffs.append(mean - hf[..., sl])
    d = jnp.stack(diffs, axis=2).astype(h.dtype)
    y = jnp.einsum('btgc,gce->btge', d, w_pool).reshape(B, T, D_MODEL) * scale
    return y, hp[:, -(POOL_MAX - 1):]


def trunk(x, cache_k, cache_v, cache_conv, cache_pool, t0, norm_g, ffn_w_gate, ffn_w_up, ffn_w_down,
          mix_w_in, mix_w_out, conv_w, attn_sinks, rel_bias_table, pool_w, pool_scale, final_norm_g):
    first = cache_k is None
    B, T = x.shape[:2]
    new_k, new_v, new_conv, new_pool = [], [], [], []
    for layer in range(DEPTH):
        x = x + 0.5 * swiglu(rms_norm(x, norm_g[layer, 0]), ffn_w_gate[layer, 0], ffn_w_up[layer, 0], ffn_w_down[layer, 0])
        h = rms_norm(x, norm_g[layer, 1])
        if layer % 2 == 0:
            e = layer // 2
            q, k, v, bg, cg, xi = jnp.split(h @ mix_w_in[e], SPLIT_POINTS, axis=-1)
            q = q.reshape(B, T, N_KV_HEADS, Q_PER_KV, HEAD_DIM)
            k = k.reshape(B, T, N_KV_HEADS, HEAD_DIM)
            v = v.reshape(B, T, N_KV_HEADS, HEAD_DIM)
            if first:
                a = attn_prompt(q, k, v, attn_sinks[e], rel_bias_table)
                keep = min(WINDOW, T)
                new_k.append(k[:, -keep:])
                new_v.append(v[:, -keep:])
                hist = jnp.zeros((B, CONV_K - 1, CONV_WIDTH), h.dtype)
            else:
                a = attn_sample(q, k, v, cache_k[e], cache_v[e], t0, attn_sinks[e], rel_bias_table)
                new_k.append(k)
                new_v.append(v)
                hist = cache_conv[e]
            c, conv_state = short_conv(cg * xi, hist, conv_w[e])
            new_conv.append(conv_state)
            x = x + jnp.concatenate([a, bg * c], axis=-1) @ mix_w_out[e]
        else:
            o = layer // 2
            hist = jnp.zeros((B, 0, D_MODEL), h.dtype) if first else cache_pool[o]
            p, pool_state = pool_mix(h, hist, t0, pool_w[o], pool_scale[o])
            new_pool.append(pool_state)
            x = x + p
        x = x + 0.5 * swiglu(rms_norm(x, norm_g[layer, 2]), ffn_w_gate[layer, 1], ffn_w_up[layer, 1], ffn_w_down[layer, 1])
    return (rms_norm(x, final_norm_g), jnp.stack(new_k), jnp.stack(new_v), jnp.stack(new_conv), jnp.stack(new_pool))


def setup_inputs(seed: int = 0) -> dict:
    key = jax.random.key(seed)
    ks = jax.random.split(key, 20)
    f32 = jnp.float32

    def nrm(k, shape, s):
        return s * jax.random.normal(k, shape, f32)

    lc = min(WINDOW, PAST_LEN)
    return {
        "x_prompt": nrm(ks[0], (BATCH, SEQ, D_MODEL), 1.0),
        "x_sample": nrm(ks[1], (DEC_BATCH, DEC_SEQ, D_MODEL), 1.0),
        "cache_attn_k": nrm(ks[2], (N_EVEN, DEC_BATCH, lc, N_KV_HEADS, HEAD_DIM), 1.0),
        "cache_attn_v": nrm(ks[3], (N_EVEN, DEC_BATCH, lc, N_KV_HEADS, HEAD_DIM), 1.0),
        "cache_conv": nrm(ks[4], (N_EVEN, DEC_BATCH, CONV_K - 1, CONV_WIDTH), 1.0),
        "cache_pool": nrm(ks[5], (N_ODD, DEC_BATCH, POOL_MAX - 1, D_MODEL), 1.0),
        "norm_g": 1.0 + nrm(ks[6], (DEPTH, 3, D_MODEL), 0.05),
        "ffn_w_gate": nrm(ks[7], (DEPTH, 2, D_MODEL, D_FF), D_MODEL ** -0.5),
        "ffn_w_up": nrm(ks[8], (DEPTH, 2, D_MODEL, D_FF), D_MODEL ** -0.5),
        "ffn_w_down": nrm(ks[9], (DEPTH, 2, D_FF, D_MODEL), D_FF ** -0.5),
        "mix_w_in": nrm(ks[10], (N_EVEN, D_MODEL, IN_WIDTH), D_MODEL ** -0.5),
        "mix_w_out": nrm(ks[11], (N_EVEN, MIX_WIDTH, D_MODEL), MIX_WIDTH ** -0.5),
        "conv_w": nrm(ks[12], (N_EVEN, CONV_K, CONV_WIDTH), CONV_K ** -0.5),
        "attn_sinks": nrm(ks[13], (N_EVEN, N_HEADS), 0.5),
        "rel_bias_table": nrm(ks[14], (N_BUCKETS, N_HEADS), 0.5),
        "pool_w": nrm(ks[15], (N_ODD, POOL_GROUPS, POOL_GROUP, POOL_GROUP), POOL_GROUP ** -0.5),
        "pool_scale": 1.0 + nrm(ks[16], (N_ODD, D_MODEL), 0.1),
        "final_norm_g": 1.0 + nrm(ks[17], (D_MODEL,), 0.05),
    }


def reference(x_prompt, x_sample, cache_attn_k, cache_attn_v, cache_conv, cache_pool, norm_g, ffn_w_gate, ffn_w_up,
              ffn_w_down, mix_w_in, mix_w_out, conv_w, attn_sinks, rel_bias_table, pool_w, pool_scale, final_norm_g):
    y_prompt, k_p, v_p, conv_p, pool_p = trunk(
        x_prompt, None, None, None, None, 0, norm_g, ffn_w_gate, ffn_w_up, ffn_w_down,
        mix_w_in, mix_w_out, conv_w, attn_sinks, rel_bias_table, pool_w, pool_scale, final_norm_g)
    y_sample, k_s, v_s, conv_s, pool_s = trunk(
        x_sample, cache_attn_k, cache_attn_v, cache_conv, cache_pool, PAST_LEN, norm_g, ffn_w_gate, ffn_w_up, ffn_w_down,
        mix_w_in, mix_w_out, conv_w, attn_sinks, rel_bias_table, pool_w, pool_scale, final_norm_g)
    return (y_prompt, y_sample, k_p, v_p, conv_p, pool_p, k_s, v_s, conv_s, pool_s)
```

```python
import functools
import math

import jax
import jax.numpy as jnp
from jax import lax
from jax.experimental import pallas as pl
from jax.experimental.pallas import tpu as pltpu

D_MODEL = 1024
DEPTH = 2
CHUNK = 64
N_HEADS = 8
N_KV_HEADS = 2
Q_PER_KV = N_HEADS // N_KV_HEADS
HEAD_DIM = 64
ATTN_WIDTH = N_HEADS * HEAD_DIM
KV_WIDTH = N_KV_HEADS * HEAD_DIM
WINDOW = 128
N_BUCKETS = 32
MAX_DISTANCE = 128
CONV_WIDTH = D_MODEL // 2
CONV_K = 3
IN_WIDTH = ATTN_WIDTH + 2 * KV_WIDTH + 3 * CONV_WIDTH
MIX_WIDTH = ATTN_WIDTH + CONV_WIDTH
POOL_SIZES = (2, 4, 8, 16)
POOL_GROUP = D_MODEL // len(POOL_SIZES)
POOL_MAX = max(POOL_SIZES)
D_FF = 2816
EPS = 1e-6
NEG = -1e30

V7X_VMEM_BYTES = 64 * 1024 * 1024
SUBLANES_F32 = 8
CONV_HIST_ROWS = SUBLANES_F32
POOL_HIST_ROWS = 16
ROW_TILE = 512
FF_CHUNKS = ((0, 1024), (1024, 1024), (2048, 768))

F32 = jnp.float32
BF16 = jnp.bfloat16


def _vmem_limit(nbytes):
    return int(min(V7X_VMEM_BYTES - (4 << 20), max(32 << 20, nbytes)))


def _rms(x, g):
    return x * lax.rsqrt(jnp.mean(x * x, axis=-1, keepdims=True) + EPS) * g


def _relbias_body(tab_ref, bkt_ref, o_ref):
    bkt = bkt_ref[...]
    for h in range(N_HEADS):
        acc = jnp.zeros(bkt.shape, F32)
        for b in range(N_BUCKETS):
            acc = jnp.where(bkt == b, tab_ref[b, h], acc)
        o_ref[h] = acc


def _rel_bucket(rel):
    half = N_BUCKETS // 2
    ret = jnp.where(rel > 0, half, 0)
    n = jnp.abs(rel)
    max_exact = half // 2
    nf = jnp.maximum(n, 1).astype(F32)
    large = max_exact + (jnp.log(nf / max_exact) / math.log(MAX_DISTANCE / max_exact)
                         * (half - max_exact)).astype(jnp.int32)
    large = jnp.minimum(large, half - 1)
    return ret + jnp.where(n < max_exact, n, large)


def _relbias(table):
    rel = (jnp.arange(WINDOW + CHUNK, dtype=jnp.int32) - WINDOW)[None, :] \
        - jnp.arange(CHUNK, dtype=jnp.int32)[:, None]
    bucket = _rel_bucket(rel).astype(jnp.int32)
    return pl.pallas_call(
        _relbias_body,
        out_shape=jax.ShapeDtypeStruct((N_HEADS, CHUNK, WINDOW + CHUNK), F32),
        in_specs=[pl.BlockSpec(memory_space=pltpu.SMEM),
                  pl.BlockSpec(memory_space=pltpu.VMEM)],
        out_specs=pl.BlockSpec(memory_space=pltpu.VMEM),
        name="relbias",
    )(table, bucket)


def _ffn_body(x_ref, g_ref, wg_ref, wu_ref, wd_ref, fg_ref, o_ref, a_ref, *, final_norm):
    x = x_ref[...]
    xn = _rms(x, g_ref[...]).astype(BF16)
    for c0, cw in FF_CHUNKS:
        gate = jnp.dot(xn, wg_ref[:, c0:c0 + cw], preferred_element_type=F32)
        up = jnp.dot(xn, wu_ref[:, c0:c0 + cw], preferred_element_type=F32)
        a_ref[:, c0:c0 + cw] = (jax.nn.silu(gate) * up).astype(BF16)
    y = jnp.dot(a_ref[...], wd_ref[...], preferred_element_type=F32)
    out = x + 0.5 * y
    if final_norm:
        out = _rms(out, fg_ref[...])
    o_ref[...] = out


def _ffn(x, g, wg, wu, wd, layer, j, fg, final_norm, tm):
    n = x.shape[0]
    wspec_in = pl.BlockSpec((None, None, D_MODEL, D_FF), lambda i: (layer, j, 0, 0))
    wspec_out = pl.BlockSpec((None, None, D_FF, D_MODEL), lambda i: (layer, j, 0, 0))
    vec = pl.BlockSpec((1, D_MODEL), lambda i: (0, 0))
    est = 2 * 3 * D_MODEL * D_FF * 2 + 4 * tm * D_MODEL * 4 + tm * D_FF * 2 + 6 * tm * 1024 * 4
    return pl.pallas_call(
        functools.partial(_ffn_body, final_norm=final_norm),
        out_shape=jax.ShapeDtypeStruct((n, D_MODEL), F32),
        grid=(n // tm,),
        in_specs=[pl.BlockSpec((tm, D_MODEL), lambda i: (i, 0)), vec,
                  wspec_in, wspec_in, wspec_out, vec],
        out_specs=pl.BlockSpec((tm, D_MODEL), lambda i: (i, 0)),
        scratch_shapes=[pltpu.VMEM((tm, D_FF), BF16)],
        compiler_params=pltpu.CompilerParams(
            dimension_semantics=("arbitrary",), vmem_limit_bytes=_vmem_limit(est)),
        name="ffn",
    )(x, g.reshape(1, D_MODEL), wg, wu, wd, fg.reshape(1, D_MODEL))


def _inproj_body(x_ref, g_ref, w_ref, q_ref, k_ref, v_ref, bg_ref, u_ref):
    h = _rms(x_ref[...], g_ref[...]).astype(BF16)
    p = jnp.dot(h, w_ref[...], preferred_element_type=F32)
    o = 0
    q_ref[...] = p[:, o:o + ATTN_WIDTH].astype(BF16)
    o += ATTN_WIDTH
    k_ref[...] = p[:, o:o + KV_WIDTH]
    o += KV_WIDTH
    v_ref[...] = p[:, o:o + KV_WIDTH]
    o += KV_WIDTH
    bg_ref[...] = p[:, o:o + CONV_WIDTH]
    o += CONV_WIDTH
    u_ref[...] = p[:, o:o + CONV_WIDTH] * p[:, o + CONV_WIDTH:o + 2 * CONV_WIDTH]


def _inproj(x, g, w_in, e, tm):
    n = x.shape[0]
    row = lambda w: pl.BlockSpec((tm, w), lambda i: (i, 0))
    est = 2 * D_MODEL * IN_WIDTH * 2 + 2 * tm * (D_MODEL + IN_WIDTH) * 4 + 2 * tm * IN_WIDTH * 4
    return pl.pallas_call(
        _inproj_body,
        out_shape=(jax.ShapeDtypeStruct((n, ATTN_WIDTH), BF16),
                   jax.ShapeDtypeStruct((n, KV_WIDTH), F32),
                   jax.ShapeDtypeStruct((n, KV_WIDTH), F32),
                   jax.ShapeDtypeStruct((n, CONV_WIDTH), F32),
                   jax.ShapeDtypeStruct((n, CONV_WIDTH), F32)),
        grid=(n // tm,),
        in_specs=[row(D_MODEL), pl.BlockSpec((1, D_MODEL), lambda i: (0, 0)),
                  pl.BlockSpec((None, D_MODEL, IN_WIDTH), lambda i: (e, 0, 0))],
        out_specs=(row(ATTN_WIDTH), row(KV_WIDTH), row(KV_WIDTH), row(CONV_WIDTH), row(CONV_WIDTH)),
        compiler_params=pltpu.CompilerParams(
            dimension_semantics=("arbitrary",), vmem_limit_bytes=_vmem_limit(est)),
        name="inproj",
    )(x, g.reshape(1, D_MODEL), w_in)


def _mixer_ab_body(x_ref, q_ref, k_ref, v_ref, kh_ref, vh_ref, bg_ref, u_ref, uh_ref,
                   bias_ref, sink_ref, cw_ref, wo_ref, o_ref,
                   kbuf, vbuf, ubuf, mix, *, tq, cl, zero_history):
    t = pl.program_id(1)
    kw = WINDOW + cl
    for hh in range(N_KV_HEADS):
        sl = slice(hh * HEAD_DIM, (hh + 1) * HEAD_DIM)
        kbuf[hh, 0:WINDOW, :] = kh_ref[:, sl].astype(BF16)
        vbuf[hh, 0:WINDOW, :] = vh_ref[:, sl].astype(BF16)
        kbuf[hh, WINDOW:WINDOW + tq, :] = k_ref[:, sl].astype(BF16)
        vbuf[hh, WINDOW:WINDOW + tq, :] = v_ref[:, sl].astype(BF16)

    col = lax.broadcasted_iota(jnp.int32, (Q_PER_KV * cl, kw), 1)

    def chunk(c, carry):
        r0 = pl.multiple_of(c * cl, cl)
        qc = q_ref[pl.ds(r0, cl), :]
        for hh in range(N_KV_HEADS):
            qg = jnp.concatenate(
                [qc[:, (hh * Q_PER_KV + g) * HEAD_DIM:(hh * Q_PER_KV + g + 1) * HEAD_DIM]
                 for g in range(Q_PER_KV)], axis=0)
            kk = kbuf[hh, pl.ds(r0, kw), :]
            vv = vbuf[hh, pl.ds(r0, kw), :]
            s = lax.dot_general(qg, kk, (((1,), (1,)), ((), ())), preferred_element_type=F32)
            s = s * (HEAD_DIM ** -0.5) + bias_ref[hh]
            if zero_history:
                s = jnp.where(col + (t * tq + r0 - WINDOW) >= 0, s, NEG)
            sk = sink_ref[hh]
            m = jnp.maximum(jnp.max(s, axis=-1, keepdims=True), sk)
            p = jnp.exp(s - m)
            den = jnp.sum(p, axis=-1, keepdims=True) + jnp.exp(sk - m)
            o = jnp.dot(p.astype(BF16), vv, preferred_element_type=F32) / den
            for g in range(Q_PER_KV):
                c0 = (hh * Q_PER_KV + g) * HEAD_DIM
                mix[pl.ds(r0, cl), c0:c0 + HEAD_DIM] = o[g * cl:(g + 1) * cl].astype(BF16)
        return carry

    lax.fori_loop(0, tq // cl, chunk, 0)

    uh = uh_ref[...]
    if zero_history:
        uh = jnp.where(t == 0, 0.0, uh)
    ubuf[0:CONV_HIST_ROWS, :] = uh
    ubuf[CONV_HIST_ROWS:CONV_HIST_ROWS + tq, :] = u_ref[...]
    conv = None
    for j in range(CONV_K):
        off = CONV_HIST_ROWS - (CONV_K - 1) + j
        term = cw_ref[j:j + 1, :] * ubuf[off:off + tq, :]
        conv = term if conv is None else conv + term
    mix[:, ATTN_WIDTH:MIX_WIDTH] = (bg_ref[...] * conv).astype(BF16)

    o_ref[...] = x_ref[...] + jnp.dot(mix[...], wo_ref[...], preferred_element_type=F32)


def _mixer_ab(x, q, k, v, k_hist, v_hist, bg, u, u_hist, bias, sinks, conv_w, w_out, e,
              *, nb, t_len, tq, cl, zero_history):
    n = x.shape[0]
    nt = t_len // tq
    row = lambda w: pl.BlockSpec((tq, w), lambda b, t: (b * nt + t, 0))
    if zero_history:
        kh_spec = pl.BlockSpec((WINDOW, KV_WIDTH), lambda b, t: (
            jnp.maximum((b * t_len + t * tq) // WINDOW - 1, 0), 0))
        uh_spec = pl.BlockSpec((CONV_HIST_ROWS, CONV_WIDTH), lambda b, t: (
            jnp.maximum((b * t_len + t * tq) // CONV_HIST_ROWS - 1, 0), 0))
    else:
        kh_spec = pl.BlockSpec((WINDOW, KV_WIDTH), lambda b, t: (b, 0))
        uh_spec = pl.BlockSpec((CONV_HIST_ROWS, CONV_WIDTH), lambda b, t: (b, 0))
    kw = WINDOW + cl
    const3 = lambda s: pl.BlockSpec(s, lambda b, t: (0, 0, 0))
    est = 2 * tq * (2 * D_MODEL * 4 + ATTN_WIDTH * 2 + 2 * KV_WIDTH * 4 + 2 * CONV_WIDTH * 4) \
        + 2 * MIX_WIDTH * D_MODEL * 2 + 8 * tq * D_MODEL * 4
    return pl.pallas_call(
        functools.partial(_mixer_ab_body, tq=tq, cl=cl, zero_history=zero_history),
        out_shape=jax.ShapeDtypeStruct((n, D_MODEL), F32),
        grid=(nb, nt),
        in_specs=[row(D_MODEL), row(ATTN_WIDTH), row(KV_WIDTH), row(KV_WIDTH),
                  kh_spec, kh_spec, row(CONV_WIDTH), row(CONV_WIDTH), uh_spec,
                  const3((N_KV_HEADS, Q_PER_KV * cl, kw)),
                  const3((N_KV_HEADS, Q_PER_KV * cl, 1)),
                  pl.BlockSpec((None, CONV_K, CONV_WIDTH), lambda b, t: (e, 0, 0)),
                  pl.BlockSpec((None, MIX_WIDTH, D_MODEL), lambda b, t: (e, 0, 0))],
        out_specs=row(D_MODEL),
        scratch_shapes=[pltpu.VMEM((N_KV_HEADS, WINDOW + tq, HEAD_DIM), BF16),
                        pltpu.VMEM((N_KV_HEADS, WINDOW + tq, HEAD_DIM), BF16),
                        pltpu.VMEM((CONV_HIST_ROWS + tq, CONV_WIDTH), F32),
                        pltpu.VMEM((tq, MIX_WIDTH), BF16)],
        compiler_params=pltpu.CompilerParams(
            dimension_semantics=("arbitrary", "arbitrary"), vmem_limit_bytes=_vmem_limit(est)),
        name="mixer_ab",
    )(x, q, k, v, k_hist, v_hist, bg, u, u_hist, bias, sinks, conv_w, w_out)


def _mixer_c_body(x_ref, hist_ref, g_ref, pw_ref, sc_ref, o_ref, st_ref, hbuf,
                  *, tp, first_pos, hist_is_input):
    t = pl.program_id(1)
    x = x_ref[...]
    g = g_ref[...]
    h = _rms(x, g)
    if hist_is_input:
        hist = jnp.where(t == 0, 0.0, _rms(hist_ref[...], g))
    else:
        hist = hist_ref[...]
    hbuf[0:POOL_HIST_ROWS, :] = hist
    hbuf[POOL_HIST_ROWS:POOL_HIST_ROWS + tp, :] = h

    pos = first_pos + t * tp + lax.broadcasted_iota(jnp.int32, (tp, 1), 0)
    for gi, w in enumerate(POOL_SIZES):
        sl = slice(gi * POOL_GROUP, (gi + 1) * POOL_GROUP)
        acc = h[:, sl]
        for j in range(1, w):
            acc = acc + hbuf[POOL_HIST_ROWS - j:POOL_HIST_ROWS - j + tp, sl]
        cnt = jnp.minimum(pos + 1, w).astype(F32)
        d = (acc / cnt - h[:, sl]).astype(BF16)
        y = jnp.dot(d, pw_ref[gi], preferred_element_type=F32)
        o_ref[:, sl] = x[:, sl] + y * sc_ref[:, sl]

    @pl.when(t == pl.num_programs(1) - 1)
    def _():
        st_ref[...] = h[tp - POOL_HIST_ROWS:tp, :]


def _mixer_c(x, hist, g, pool_w, scale, o, *, nb, t_len, tp, first_pos, hist_is_input):
    n = x.shape[0]
    nt = t_len // tp
    row = pl.BlockSpec((tp, D_MODEL), lambda b, t: (b * nt + t, 0))
    if hist_is_input:
        hist_spec = pl.BlockSpec((POOL_HIST_ROWS, D_MODEL), lambda b, t: (
            jnp.maximum((b * t_len + t * tp) // POOL_HIST_ROWS - 1, 0), 0))
    else:
        hist_spec = pl.BlockSpec((POOL_HIST_ROWS, D_MODEL), lambda b, t: (b, 0))
    vec = pl.BlockSpec((1, D_MODEL), lambda b, t: (0, 0))
    est = 6 * tp * D_MODEL * 4 + 8 * tp * D_MODEL * 4
    return pl.pallas_call(
        functools.partial(_mixer_c_body, tp=tp, first_pos=first_pos, hist_is_input=hist_is_input),
        out_shape=(jax.ShapeDtypeStruct((n, D_MODEL), F32),
                   jax.ShapeDtypeStruct((nb * POOL_HIST_ROWS, D_MODEL), F32)),
        grid=(nb, nt),
        in_specs=[row, hist_spec, vec,
                  pl.BlockSpec((None, len(POOL_SIZES), POOL_GROUP, POOL_GROUP),
                               lambda b, t: (o, 0, 0, 0)),
                  pl.BlockSpec((None, 1, D_MODEL), lambda b, t: (o, 0, 0))],
        out_specs=(row, pl.BlockSpec((POOL_HIST_ROWS, D_MODEL), lambda b, t: (b, 0))),
        scratch_shapes=[pltpu.VMEM((POOL_HIST_ROWS + tp, D_MODEL), F32)],
        compiler_params=pltpu.CompilerParams(
            dimension_semantics=("arbitrary", "arbitrary"), vmem_limit_bytes=_vmem_limit(est)),
        name="mixer_c",
    )(x, hist, g.reshape(1, D_MODEL), pool_w, scale.reshape(scale.shape[0], 1, D_MODEL))


def _trunk(x3, caches, first_pos, w, bias):
    nb, t_len, _ = x3.shape
    prompt = caches is None
    x = x3.reshape(nb * t_len, D_MODEL)
    n = x.shape[0]
    tm = min(ROW_TILE, n)
    tq = min(ROW_TILE, t_len)
    cl = min(CHUNK, t_len)
    kw = WINDOW + cl
    bias_rows = bias[:, :cl, :kw].reshape(N_KV_HEADS, Q_PER_KV * cl, kw)
    new_k, new_v, new_conv, new_pool = [], [], [], []
    for layer in range(DEPTH):
        last = layer == DEPTH - 1
        x = _ffn(x, w["norm_g"][layer, 0], w["wg"], w["wu"], w["wd"], layer, 0,
                 w["final_norm_g"], False, tm)
        if layer % 2 == 0:
            e = layer // 2
            q, k, v, bg, u = _inproj(x, w["norm_g"][layer, 1], w["w_in"], e, tm)
            if prompt:
                k_hist, v_hist, u_hist = k, v, u
            else:
                ck, cv, cc, _ = caches
                k_hist = ck[e].reshape(nb * WINDOW, KV_WIDTH)
                v_hist = cv[e].reshape(nb * WINDOW, KV_WIDTH)
                u_hist = jnp.pad(cc[e], ((0, 0), (CONV_HIST_ROWS - (CONV_K - 1), 0), (0, 0))
                                 ).reshape(nb * CONV_HIST_ROWS, CONV_WIDTH)
            sink_rows = jnp.repeat(w["attn_sinks"][e].reshape(N_KV_HEADS, Q_PER_KV), cl, axis=1
                                   ).reshape(N_KV_HEADS, Q_PER_KV * cl, 1)
            x = _mixer_ab(x, q, k, v, k_hist, v_hist, bg, u, u_hist, bias_rows, sink_rows,
                          w["conv_w"], w["w_out"], e,
                          nb=nb, t_len=t_len, tq=tq, cl=cl, zero_history=prompt)
            keep = min(WINDOW, t_len)
            new_k.append(k.reshape(nb, t_len, N_KV_HEADS, HEAD_DIM)[:, t_len - keep:])
            new_v.append(v.reshape(nb, t_len, N_KV_HEADS, HEAD_DIM)[:, t_len - keep:])
            new_conv.append(u.reshape(nb, t_len, CONV_WIDTH)[:, t_len - (CONV_K - 1):])
        else:
            o = layer // 2
            if prompt:
                hist = x
            else:
                hist = jnp.pad(caches[3][o], ((0, 0), (POOL_HIST_ROWS - (POOL_MAX - 1), 0), (0, 0))
                               ).reshape(nb * POOL_HIST_ROWS, D_MODEL)
            x, st = _mixer_c(x, hist, w["norm_g"][layer, 1], w["pool_w"], w["pool_scale"], o,
                             nb=nb, t_len=t_len, tp=tq, first_pos=first_pos, hist_is_input=prompt)
            new_pool.append(st.reshape(nb, POOL_HIST_ROWS, D_MODEL)[:, POOL_HIST_ROWS - (POOL_MAX - 1):])
        x = _ffn(x, w["norm_g"][layer, 2], w["wg"], w["wu"], w["wd"], layer, 1,
                 w["final_norm_g"], last, tm)
    return (x.reshape(nb, t_len, D_MODEL), jnp.stack(new_k), jnp.stack(new_v),
            jnp.stack(new_conv), jnp.stack(new_pool))


def kernel(x_prompt, x_sample, cache_attn_k, cache_attn_v, cache_conv, cache_pool, norm_g, ffn_w_gate, ffn_w_up, ffn_w_down, mix_w_in, mix_w_out, conv_w, attn_sinks, rel_bias_table, pool_w, pool_scale, final_norm_g):
    past_len = 1024
    w = dict(norm_g=norm_g, final_norm_g=final_norm_g,
             wg=ffn_w_gate.astype(BF16), wu=ffn_w_up.astype(BF16), wd=ffn_w_down.astype(BF16),
             w_in=mix_w_in.astype(BF16), w_out=mix_w_out.astype(BF16), conv_w=conv_w,
             attn_sinks=attn_sinks, pool_w=pool_w.astype(BF16), pool_scale=pool_scale)
    bias = _relbias(rel_bias_table)
    y_p, k_p, v_p, conv_p, pool_p = _trunk(x_prompt, None, 0, w, bias)
    y_s, k_s, v_s, conv_s, pool_s = _trunk(
        x_sample, (cache_attn_k, cache_attn_v, cache_conv, cache_pool), past_len, w, bias)
    return (y_p, y_s, k_p, v_p, conv_p, pool_p, k_s, v_s, conv_s, pool_s)
```

```python
import functools
import math

import jax
import jax.numpy as jnp
from jax import lax
from jax.experimental import pallas as pl
from jax.experimental.pallas import tpu as pltpu

D_MODEL = 1024
DEPTH = 2
CHUNK = 64
N_HEADS = 8
N_KV_HEADS = 2
Q_PER_KV = N_HEADS // N_KV_HEADS
HEAD_DIM = 64
ATTN_WIDTH = N_HEADS * HEAD_DIM
KV_WIDTH = N_KV_HEADS * HEAD_DIM
WINDOW = 128
N_BUCKETS = 32
MAX_DISTANCE = 128
CONV_WIDTH = D_MODEL // 2
CONV_K = 3
IN_WIDTH = ATTN_WIDTH + 2 * KV_WIDTH + 3 * CONV_WIDTH
MIX_WIDTH = ATTN_WIDTH + CONV_WIDTH
POOL_SIZES = (2, 4, 8, 16)
POOL_GROUP = D_MODEL // len(POOL_SIZES)
POOL_MAX = max(POOL_SIZES)
D_FF = 2816
EPS = 1e-6
NEG = -1e30

V7X_VMEM_BYTES = 64 * 1024 * 1024
SUBLANES_F32 = 8
CONV_HIST_ROWS = SUBLANES_F32
POOL_HIST_ROWS = 16
ROW_TILE = 512
FF_CHUNKS = ((0, 1024), (1024, 1024), (2048, 768))

F32 = jnp.float32
BF16 = jnp.bfloat16


def _vmem_limit(nbytes):
    return int(min(V7X_VMEM_BYTES - (4 << 20), max(32 << 20, nbytes)))


def _rms(x, g):
    return x * lax.rsqrt(jnp.mean(x * x, axis=-1, keepdims=True) + EPS) * g


def _relbias_body(tab_ref, bkt_ref, o_ref):
    bkt = bkt_ref[...]
    for h in range(N_HEADS):
        acc = jnp.zeros(bkt.shape, F32)
        for b in range(N_BUCKETS):
            acc = jnp.where(bkt == b, tab_ref[b, h], acc)
        o_ref[h] = acc


def _rel_bucket(rel):
    half = N_BUCKETS // 2
    ret = jnp.where(rel > 0, half, 0)
    n = jnp.abs(rel)
    max_exact = half // 2
    nf = jnp.maximum(n, 1).astype(F32)
    large = max_exact + (jnp.log(nf / max_exact) / math.log(MAX_DISTANCE / max_exact)
                         * (half - max_exact)).astype(jnp.int32)
    large = jnp.minimum(large, half - 1)
    return ret + jnp.where(n < max_exact, n, large)


def _relbias(table):
    rel = (jnp.arange(WINDOW + CHUNK, dtype=jnp.int32) - WINDOW)[None, :] \
        - jnp.arange(CHUNK, dtype=jnp.int32)[:, None]
    bucket = _rel_bucket(rel).astype(jnp.int32)
    return pl.pallas_call(
        _relbias_body,
        out_shape=jax.ShapeDtypeStruct((N_HEADS, CHUNK, WINDOW + CHUNK), F32),
        in_specs=[pl.BlockSpec(memory_space=pltpu.SMEM),
                  pl.BlockSpec(memory_space=pltpu.VMEM)],
        out_specs=pl.BlockSpec(memory_space=pltpu.VMEM),
        name="relbias",
    )(table, bucket)


def _ffn_body(x_ref, g_ref, wg_ref, wu_ref, wd_ref, fg_ref, o_ref, a_ref, *, final_norm):
    x = x_ref[...]
    xn = _rms(x, g_ref[...]).astype(BF16)
    for c0, cw in FF_CHUNKS:
        gate = jnp.dot(xn, wg_ref[:, c0:c0 + cw], preferred_element_type=F32)
        up = jnp.dot(xn, wu_ref[:, c0:c0 + cw], preferred_element_type=F32)
        a_ref[:, c0:c0 + cw] = (jax.nn.silu(gate) * up).astype(BF16)
    y = jnp.dot(a_ref[...], wd_ref[...], preferred_element_type=F32)
    out = x + 0.5 * y
    if final_norm:
        out = _rms(out, fg_ref[...])
    o_ref[...] = out


def _ffn(x, g, wg, wu, wd, layer, j, fg, final_norm, tm):
    n = x.shape[0]
    wspec_in = pl.BlockSpec((None, None, D_MODEL, D_FF), lambda i: (layer, j, 0, 0))
    wspec_out = pl.BlockSpec((None, None, D_FF, D_MODEL), lambda i: (layer, j, 0, 0))
    vec = pl.BlockSpec((1, D_MODEL), lambda i: (0, 0))
    est = 2 * 3 * D_MODEL * D_FF * 2 + 4 * tm * D_MODEL * 4 + tm * D_FF * 2 + 6 * tm * 1024 * 4
    return pl.pallas_call(
        functools.partial(_ffn_body, final_norm=final_norm),
        out_shape=jax.ShapeDtypeStruct((n, D_MODEL), F32),
        grid=(n // tm,),
        in_specs=[pl.BlockSpec((tm, D_MODEL), lambda i: (i, 0)), vec,
                  wspec_in, wspec_in, wspec_out, vec],
        out_specs=pl.BlockSpec((tm, D_MODEL), lambda i: (i, 0)),
        scratch_shapes=[pltpu.VMEM((tm, D_FF), BF16)],
        compiler_params=pltpu.CompilerParams(
            dimension_semantics=("arbitrary",), vmem_limit_bytes=_vmem_limit(est)),
        name="ffn",
    )(x, g.reshape(1, D_MODEL), wg, wu, wd, fg.reshape(1, D_MODEL))


def _inproj_body(x_ref, g_ref, w_ref, q_ref, k_ref, v_ref, bg_ref, u_ref):
    h = _rms(x_ref[...], g_ref[...]).astype(BF16)
    p = jnp.dot(h, w_ref[...], preferred_element_type=F32)
    o = 0
    q_ref[...] = (p[:, o:o + ATTN_WIDTH] * (HEAD_DIM ** -0.5)).astype(BF16)
    o += ATTN_WIDTH
    k_ref[...] = p[:, o:o + KV_WIDTH]
    o += KV_WIDTH
    v_ref[...] = p[:, o:o + KV_WIDTH]
    o += KV_WIDTH
    bg_ref[...] = p[:, o:o + CONV_WIDTH]
    o += CONV_WIDTH
    u_ref[...] = p[:, o:o + CONV_WIDTH] * p[:, o + CONV_WIDTH:o + 2 * CONV_WIDTH]


def _inproj(x, g, w_in, e, tm):
    n = x.shape[0]
    row = lambda w: pl.BlockSpec((tm, w), lambda i: (i, 0))
    est = 2 * D_MODEL * IN_WIDTH * 2 + 2 * tm * (D_MODEL + IN_WIDTH) * 4 + 2 * tm * IN_WIDTH * 4
    return pl.pallas_call(
        _inproj_body,
        out_shape=(jax.ShapeDtypeStruct((n, ATTN_WIDTH), BF16),
                   jax.ShapeDtypeStruct((n, KV_WIDTH), F32),
                   jax.ShapeDtypeStruct((n, KV_WIDTH), F32),
                   jax.ShapeDtypeStruct((n, CONV_WIDTH), F32),
                   jax.ShapeDtypeStruct((n, CONV_WIDTH), F32)),
        grid=(n // tm,),
        in_specs=[row(D_MODEL), pl.BlockSpec((1, D_MODEL), lambda i: (0, 0)),
                  pl.BlockSpec((None, D_MODEL, IN_WIDTH), lambda i: (e, 0, 0))],
        out_specs=(row(ATTN_WIDTH), row(KV_WIDTH), row(KV_WIDTH), row(CONV_WIDTH), row(CONV_WIDTH)),
        compiler_params=pltpu.CompilerParams(
            dimension_semantics=("arbitrary",), vmem_limit_bytes=_vmem_limit(est)),
        name="inproj",
    )(x, g.reshape(1, D_MODEL), w_in)


PAIR_WIDTH = 2 * HEAD_DIM
PAIRS_PER_KV = Q_PER_KV // 2
ATTN_SCALE = HEAD_DIM ** -0.5


def _mixer_ab_body(x_ref, q_ref, k_ref, v_ref, kh_ref, vh_ref, bg_ref, u_ref, uh_ref,
                   bias_ref, sink_ref, cw_ref, wo_ref, o_ref,
                   kbuf, vbuf, ubuf, mix, sbuf, *, tq, cl, zero_history):
    t = pl.program_id(1)
    kw = WINDOW + cl
    low =lax.broadcasted_iota(jnp.int32, (1, KV_WIDTH), 1) < HEAD_DIM

    def stage(buf, src_ref, r0):
        a = src_ref[...]
        b = pltpu.roll(a, HEAD_DIM, axis=1)
        rows = a.shape[0]
        buf[0, 0, r0:r0 + rows, :] = jnp.where(low, a, 0.0).astype(BF16)
        buf[0, 1, r0:r0 + rows, :] = jnp.where(low, 0.0, b).astype(BF16)
        buf[1, 0, r0:r0 + rows, :] = jnp.where(low, b, 0.0).astype(BF16)
        buf[1, 1, r0:r0 + rows, :] = jnp.where(low, 0.0, a).astype(BF16)

    stage(kbuf, kh_ref, 0)
    stage(kbuf, k_ref, WINDOW)
    stage(vbuf, vh_ref, 0)
    stage(vbuf, v_ref, WINDOW)

    col = lax.broadcasted_iota(jnp.int32, (PAIRS_PER_KV * cl, kw), 1)

    n_chunks = tq // cl

    for c in range(n_chunks):
        r0 = c * cl
        for hh in range(N_KV_HEADS):
            ql = jnp.concatenate(
                [q_ref[r0:r0 + cl, (hh * PAIRS_PER_KV + pr) * PAIR_WIDTH:
                       (hh * PAIRS_PER_KV + pr + 1) * PAIR_WIDTH] for pr in range(PAIRS_PER_KV)],
                axis=0)
            for lh in range(2):
                kk = kbuf[hh, lh, r0:r0 + kw, :]
                sbuf[c, hh, lh] = lax.dot_general(ql, kk, (((1,), (1,)), ((), ())),
                                                  preferred_element_type=F32)

    @pl.when(t >= 0)
    def _():
        for c in range(n_chunks):
            r0 = c * cl
            for hh in range(N_KV_HEADS):
                acc = None
                for lh in range(2):
                    s = sbuf[c, hh, lh] + bias_ref[hh, lh]
                    if zero_history and r0 < WINDOW:
                        s = jnp.where(col + (t * tq + r0 - WINDOW) >= 0, s, NEG)
                    sk = sink_ref[hh, lh]
                    m = jnp.maximum(jnp.max(s, axis=-1, keepdims=True), sk)
                    p = jnp.exp(s - m)
                    den = jnp.sum(p, axis=-1, keepdims=True) + jnp.exp(sk - m)
                    vv = vbuf[hh, lh, r0:r0 + kw, :]
                    o = jnp.dot(p.astype(BF16), vv, preferred_element_type=F32) * (1.0 / den)
                    acc = o if acc is None else acc + o
                for pr in range(PAIRS_PER_KV):
                    c0 = (hh * PAIRS_PER_KV + pr) * PAIR_WIDTH
                    mix[r0:r0 + cl, c0:c0 + PAIR_WIDTH] = acc[pr * cl:(pr + 1) * cl].astype(BF16)

        uh = uh_ref[...]
        if zero_history:
            uh = jnp.where(t == 0, 0.0, uh)
        ubuf[0:CONV_HIST_ROWS, :] = uh
        ubuf[CONV_HIST_ROWS:CONV_HIST_ROWS + tq, :] = u_ref[...]
        conv = None
        for j in range(CONV_K):
            off = CONV_HIST_ROWS - (CONV_K - 1) + j
            term = cw_ref[j:j + 1, :] * ubuf[off:off + tq, :]
            conv = term if conv is None else conv + term
        mix[:, ATTN_WIDTH:MIX_WIDTH] = (bg_ref[...] * conv).astype(BF16)

        o_ref[...] = x_ref[...] + jnp.dot(mix[...], wo_ref[...], preferred_element_type=F32)


def _mixer_ab(x, q, k, v, k_hist, v_hist, bg, u, u_hist, bias, sinks, conv_w, w_out, e,
              *, nb, t_len, tq, cl, zero_history):
    n = x.shape[0]
    nt = t_len // tq
    row = lambda w: pl.BlockSpec((tq, w), lambda b, t: (b * nt + t, 0))
    if zero_history:
        kh_spec = pl.BlockSpec((WINDOW, KV_WIDTH), lambda b, t: (
            jnp.maximum((b * t_len + t * tq) // WINDOW - 1, 0), 0))
        uh_spec = pl.BlockSpec((CONV_HIST_ROWS, CONV_WIDTH), lambda b, t: (
            jnp.maximum((b * t_len + t * tq) // CONV_HIST_ROWS - 1, 0), 0))
    else:
        kh_spec = pl.BlockSpec((WINDOW, KV_WIDTH), lambda b, t: (b, 0))
        uh_spec = pl.BlockSpec((CONV_HIST_ROWS, CONV_WIDTH), lambda b, t: (b, 0))
    kw = WINDOW + cl
    const4 = lambda s: pl.BlockSpec(s, lambda b, t: (0, 0, 0, 0))
    est = 2 * tq * (2 * D_MODEL * 4 + ATTN_WIDTH * 2 + 2 * KV_WIDTH * 4 + 2 * CONV_WIDTH * 4) \
        + 2 * MIX_WIDTH * D_MODEL * 2 + 12 * tq * D_MODEL * 4
    return pl.pallas_call(
        functools.partial(_mixer_ab_body, tq=tq, cl=cl, zero_history=zero_history),
        out_shape=jax.ShapeDtypeStruct((n, D_MODEL), F32),
        grid=(nb, nt),
        in_specs=[row(D_MODEL), row(ATTN_WIDTH), row(KV_WIDTH), row(KV_WIDTH),
                  kh_spec, kh_spec, row(CONV_WIDTH), row(CONV_WIDTH), uh_spec,
                  const4((N_KV_HEADS, 2, PAIRS_PER_KV * cl, kw)),
                  const4((N_KV_HEADS, 2, PAIRS_PER_KV * cl, 1)),
                  pl.BlockSpec((None, CONV_K, CONV_WIDTH), lambda b, t: (e, 0, 0)),
                  pl.BlockSpec((None, MIX_WIDTH, D_MODEL), lambda b, t: (e, 0, 0))],
        out_specs=row(D_MODEL),
        scratch_shapes=[pltpu.VMEM((N_KV_HEADS, 2, WINDOW + tq, KV_WIDTH), BF16),
                        pltpu.VMEM((N_KV_HEADS, 2, WINDOW + tq, KV_WIDTH), BF16),
                        pltpu.VMEM((CONV_HIST_ROWS + tq, CONV_WIDTH), F32),
                        pltpu.VMEM((tq, MIX_WIDTH), BF16),
                        pltpu.VMEM((tq // cl, N_KV_HEADS, 2, PAIRS_PER_KV * cl, kw), F32)],
        compiler_params=pltpu.CompilerParams(
            dimension_semantics=("arbitrary", "arbitrary"), vmem_limit_bytes=_vmem_limit(est)),
        name="mixer_ab",
    )(x, q, k, v, k_hist, v_hist, bg, u, u_hist, bias, sinks, conv_w, w_out)


def _mixer_c_body(x_ref, hist_ref, g_ref, pw_ref, sc_ref, o_ref, st_ref, hbuf,
                  *, tp, first_pos, hist_is_input):
    t = pl.program_id(1)
    x = x_ref[...]
    g = g_ref[...]
    h = _rms(x, g)
    if hist_is_input:
        hist = jnp.where(t == 0, 0.0, _rms(hist_ref[...], g))
    else:
        hist = hist_ref[...]
    hbuf[0:POOL_HIST_ROWS, :] = hist
    hbuf[POOL_HIST_ROWS:POOL_HIST_ROWS + tp, :] = h

    sums = []
    cur = hbuf[...]
    for gi, w in enumerate(POOL_SIZES):
        assert w == 2 ** (gi + 1)
        cur = cur + pltpu.roll(cur, w // 2, axis=0)
        sums.append(cur[POOL_HIST_ROWS:POOL_HIST_ROWS + tp, 0:POOL_GROUP])
        if gi + 1 < len(POOL_SIZES):
            cur = cur[:, POOL_GROUP:]

    pos = first_pos + t * tp + lax.broadcasted_iota(jnp.int32, (tp, 1), 0)
    for gi, w in enumerate(POOL_SIZES):
        sl = slice(gi * POOL_GROUP, (gi + 1) * POOL_GROUP)
        acc = sums[gi]
        cnt = jnp.minimum(pos + 1, w).astype(F32)
        d = (acc / cnt - h[:, sl]).astype(BF16)
        y = jnp.dot(d, pw_ref[gi], preferred_element_type=F32)
        o_ref[:, sl] = x[:, sl] + y * sc_ref[:, sl]

    @pl.when(t == pl.num_programs(1) - 1)
    def _():
        st_ref[...] = h[tp - POOL_HIST_ROWS:tp, :]


def _mixer_c(x, hist, g, pool_w, scale, o, *, nb, t_len, tp, first_pos, hist_is_input):
    n = x.shape[0]
    nt = t_len // tp
    row = pl.BlockSpec((tp, D_MODEL), lambda b, t: (b * nt + t, 0))
    if hist_is_input:
        hist_spec = pl.BlockSpec((POOL_HIST_ROWS, D_MODEL), lambda b, t: (
            jnp.maximum((b * t_len + t * tp) // POOL_HIST_ROWS - 1, 0), 0))
    else:
        hist_spec = pl.BlockSpec((POOL_HIST_ROWS, D_MODEL), lambda b, t: (b, 0))
    vec = pl.BlockSpec((1, D_MODEL), lambda b, t: (0, 0))
    est = 6 * tp * D_MODEL * 4 + 8 * tp * D_MODEL * 4
    return pl.pallas_call(
        functools.partial(_mixer_c_body, tp=tp, first_pos=first_pos, hist_is_input=hist_is_input),
        out_shape=(jax.ShapeDtypeStruct((n, D_MODEL), F32),
                   jax.ShapeDtypeStruct((nb * POOL_HIST_ROWS, D_MODEL), F32)),
        grid=(nb, nt),
        in_specs=[row, hist_spec, vec,
                  pl.BlockSpec((None, len(POOL_SIZES), POOL_GROUP, POOL_GROUP),
                               lambda b, t: (o, 0, 0, 0)),
                  pl.BlockSpec((None, 1, D_MODEL), lambda b, t: (o, 0, 0))],
        out_specs=(row, pl.BlockSpec((POOL_HIST_ROWS, D_MODEL), lambda b, t: (b, 0))),
        scratch_shapes=[pltpu.VMEM((POOL_HIST_ROWS + tp, D_MODEL), F32)],
        compiler_params=pltpu.CompilerParams(
            dimension_semantics=("arbitrary", "arbitrary"), vmem_limit_bytes=_vmem_limit(est)),
        name="mixer_c",
    )(x, hist, g.reshape(1, D_MODEL), pool_w, scale.reshape(scale.shape[0], 1, D_MODEL))


def _trunk(x3, caches, first_pos, w, bias):
    nb, t_len, _ = x3.shape
    prompt = caches is None
    x = x3.reshape(nb * t_len, D_MODEL)
    n = x.shape[0]
    tm = min(ROW_TILE, n)
    tq = min(ROW_TILE, t_len)
    cl = min(CHUNK, t_len)
    kw = WINDOW + cl

    def head_rows(a):
        a = a.reshape(N_KV_HEADS, Q_PER_KV // 2, 2, cl, a.shape[-1])
        return jnp.transpose(a, (0, 2, 1, 3, 4)).reshape(N_KV_HEADS, 2, (Q_PER_KV // 2) * cl, a.shape[-1])

    bias_rows = head_rows(bias[:, :cl, :kw])
    new_k, new_v, new_conv, new_pool = [], [], [], []
    for layer in range(DEPTH):
        last = layer == DEPTH - 1
        x = _ffn(x, w["norm_g"][layer, 0], w["wg"], w["wu"], w["wd"], layer, 0,
                 w["final_norm_g"], False, tm)
        if layer % 2 == 0:
            e = layer // 2
            q, k, v, bg, u = _inproj(x, w["norm_g"][layer, 1], w["w_in"], e, tm)
            if prompt:
                k_hist, v_hist, u_hist = k, v, u
            else:
                ck, cv, cc, _ = caches
                k_hist = ck[e].reshape(nb * WINDOW, KV_WIDTH)
                v_hist = cv[e].reshape(nb * WINDOW, KV_WIDTH)
                u_hist = jnp.pad(cc[e], ((0, 0), (CONV_HIST_ROWS - (CONV_K - 1), 0), (0, 0))
                                 ).reshape(nb * CONV_HIST_ROWS, CONV_WIDTH)
            sink_rows = head_rows(jnp.broadcast_to(w["attn_sinks"][e][:, None, None], (N_HEADS, cl, 1)))
            x = _mixer_ab(x, q, k, v, k_hist, v_hist, bg, u, u_hist, bias_rows, sink_rows,
                          w["conv_w"], w["w_out"], e,
                          nb=nb, t_len=t_len, tq=tq, cl=cl, zero_history=prompt)
            keep = min(WINDOW, t_len)
            new_k.append(k.reshape(nb, t_len, N_KV_HEADS, HEAD_DIM)[:, t_len - keep:])
            new_v.append(v.reshape(nb, t_len, N_KV_HEADS, HEAD_DIM)[:, t_len - keep:])
            new_conv.append(u.reshape(nb, t_len, CONV_WIDTH)[:, t_len - (CONV_K - 1):])
        else:
            o = layer // 2
            if prompt:
                hist = x
            else:
                hist = jnp.pad(caches[3][o], ((0, 0), (POOL_HIST_ROWS - (POOL_MAX - 1), 0), (0, 0))
                               ).reshape(nb * POOL_HIST_ROWS, D_MODEL)
            x, st = _mixer_c(x, hist, w["norm_g"][layer, 1], w["pool_w"], w["pool_scale"], o,
                             nb=nb, t_len=t_len, tp=tq, first_pos=first_pos, hist_is_input=prompt)
            new_pool.append(st.reshape(nb, POOL_HIST_ROWS, D_MODEL)[:, POOL_HIST_ROWS - (POOL_MAX - 1):])
        x = _ffn(x, w["norm_g"][layer, 2], w["wg"], w["wu"], w["wd"], layer, 1,
                 w["final_norm_g"], last, tm)
    return (x.reshape(nb, t_len, D_MODEL), jnp.stack(new_k), jnp.stack(new_v),
            jnp.stack(new_conv), jnp.stack(new_pool))


def kernel(x_prompt, x_sample, cache_attn_k, cache_attn_v, cache_conv, cache_pool, norm_g, ffn_w_gate, ffn_w_up, ffn_w_down, mix_w_in, mix_w_out, conv_w, attn_sinks, rel_bias_table, pool_w, pool_scale, final_norm_g):
    past_len = 1024
    w = dict(norm_g=norm_g, final_norm_g=final_norm_g,
             wg=ffn_w_gate.astype(BF16), wu=ffn_w_up.astype(BF16), wd=ffn_w_down.astype(BF16),
             w_in=mix_w_in.astype(BF16), w_out=mix_w_out.astype(BF16), conv_w=conv_w,
             attn_sinks=attn_sinks, pool_w=pool_w.astype(BF16), pool_scale=pool_scale)
    bias = _relbias(rel_bias_table)
    y_p, k_p, v_p, conv_p, pool_p = _trunk(x_prompt, None, 0, w, bias)
    y_s, k_s, v_s, conv_s, pool_s = _trunk(
        x_sample, (cache_attn_k, cache_attn_v, cache_conv, cache_pool), past_len, w, bias)
    return (y_p, y_s, k_p, v_p, conv_p, pool_p, k_s, v_s, conv_s, pool_s)
```

```python
import functools
import math

import jax
import jax.numpy as jnp
from jax import lax
from jax.experimental import pallas as pl
from jax.experimental.pallas import tpu as pltpu

D_MODEL = 1024
DEPTH = 2
CHUNK = 64
N_HEADS = 8
N_KV_HEADS = 2
Q_PER_KV = N_HEADS // N_KV_HEADS
HEAD_DIM = 64
ATTN_WIDTH = N_HEADS * HEAD_DIM
KV_WIDTH = N_KV_HEADS * HEAD_DIM
WINDOW = 128
N_BUCKETS = 32
MAX_DISTANCE = 128
CONV_WIDTH = D_MODEL // 2
CONV_K = 3
IN_WIDTH = ATTN_WIDTH + 2 * KV_WIDTH + 3 * CONV_WIDTH
MIX_WIDTH = ATTN_WIDTH + CONV_WIDTH
POOL_SIZES = (2, 4, 8, 16)
POOL_GROUP = D_MODEL // len(POOL_SIZES)
POOL_MAX = max(POOL_SIZES)
D_FF = 2816
EPS = 1e-6
NEG = -1e30

V7X_VMEM_BYTES = 64 * 1024 * 1024
SUBLANES_F32 = 8
CONV_HIST_ROWS = SUBLANES_F32
POOL_HIST_ROWS = 16
ROW_TILE = 512
FFN_ROW_TILE = 1024
FF_CHUNK = 512
FF_CHUNKS = tuple((c0, min(FF_CHUNK, D_FF - c0)) for c0 in range(0, D_FF, FF_CHUNK))

F32 = jnp.float32
BF16 = jnp.bfloat16


def _vmem_limit(nbytes):
    return int(min(V7X_VMEM_BYTES - (4 << 20), max(32 << 20, nbytes)))


def _rms(x, g):
    return x * lax.rsqrt(jnp.mean(x * x, axis=-1, keepdims=True) + EPS) * g


def _relbias_body(tab_ref, bkt_ref, o_ref):
    bkt = bkt_ref[...]
    for h in range(N_HEADS):
        acc = jnp.zeros(bkt.shape, F32)
        for b in range(N_BUCKETS):
            acc = jnp.where(bkt == b, tab_ref[b, h], acc)
        o_ref[h] = acc


def _rel_bucket(rel):
    half = N_BUCKETS // 2
    ret = jnp.where(rel > 0, half, 0)
    n = jnp.abs(rel)
    max_exact = half // 2
    nf = jnp.maximum(n, 1).astype(F32)
    large = max_exact + (jnp.log(nf / max_exact) / math.log(MAX_DISTANCE / max_exact)
                         * (half - max_exact)).astype(jnp.int32)
    large = jnp.minimum(large, half - 1)
    return ret + jnp.where(n < max_exact, n, large)


def _relbias(table):
    rel = (jnp.arange(WINDOW + CHUNK, dtype=jnp.int32) - WINDOW)[None, :] \
        - jnp.arange(CHUNK, dtype=jnp.int32)[:, None]
    bucket = _rel_bucket(rel).astype(jnp.int32)
    return pl.pallas_call(
        _relbias_body,
        out_shape=jax.ShapeDtypeStruct((N_HEADS, CHUNK, WINDOW + CHUNK), F32),
        in_specs=[pl.BlockSpec(memory_space=pltpu.SMEM),
                  pl.BlockSpec(memory_space=pltpu.VMEM)],
        out_specs=pl.BlockSpec(memory_space=pltpu.VMEM),
        name="relbias",
    )(table, bucket)


def _ffn_body(x_ref, g_ref, wg_ref, wu_ref, wd_ref, fg_ref, o_ref, a_ref, *, final_norm):
    x = x_ref[...]
    xn = _rms(x, g_ref[...]).astype(BF16)
    for c0, cw in FF_CHUNKS:
        gate = jnp.dot(xn, wg_ref[:, c0:c0 + cw], preferred_element_type=F32)
        up = jnp.dot(xn, wu_ref[:, c0:c0 + cw], preferred_element_type=F32)
        a_ref[:, c0:c0 + cw] = (jax.nn.silu(gate) * up).astype(BF16)
    y = jnp.dot(a_ref[...], wd_ref[...], preferred_element_type=F32)
    out = x + 0.5 * y
    if final_norm:
        out = _rms(out, fg_ref[...])
    o_ref[...] = out


def _ffn(x, g, wg, wu, wd, layer, j, fg, final_norm, tm):
    n = x.shape[0]
    wspec_in = pl.BlockSpec((None, None, D_MODEL, D_FF), lambda i: (layer, j, 0, 0),
                            pipeline_mode=pl.Buffered(1))
    wspec_out = pl.BlockSpec((None, None, D_FF, D_MODEL), lambda i: (layer, j, 0, 0),
                             pipeline_mode=pl.Buffered(1))
    vec = pl.BlockSpec((1, D_MODEL), lambda i: (0, 0))
    est = 3 * D_MODEL * D_FF * 2 + 4 * tm * D_MODEL * 4 + tm * D_FF * 2 + 5 * tm * 1024 * 4
    return pl.pallas_call(
        functools.partial(_ffn_body, final_norm=final_norm),
        out_shape=jax.ShapeDtypeStruct((n, D_MODEL), F32),
        grid=(n // tm,),
        in_specs=[pl.BlockSpec((tm, D_MODEL), lambda i: (i, 0)), vec,
                  wspec_in, wspec_in, wspec_out, vec],
        out_specs=pl.BlockSpec((tm, D_MODEL), lambda i: (i, 0)),
        scratch_shapes=[pltpu.VMEM((tm, D_FF), BF16)],
        compiler_params=pltpu.CompilerParams(
            dimension_semantics=("arbitrary",), vmem_limit_bytes=_vmem_limit(est)),
        name="ffn",
    )(x, g.reshape(1, D_MODEL), wg, wu, wd, fg.reshape(1, D_MODEL))


def _inproj_body(x_ref, g_ref, w_ref, q_ref, k_ref, v_ref, bg_ref, u_ref):
    h = _rms(x_ref[...], g_ref[...]).astype(BF16)
    p = jnp.dot(h, w_ref[...], preferred_element_type=F32)
    o = 0
    q_ref[...] = (p[:, o:o + ATTN_WIDTH] * (HEAD_DIM ** -0.5)).astype(BF16)
    o += ATTN_WIDTH
    k_ref[...] = p[:, o:o + KV_WIDTH]
    o += KV_WIDTH
    v_ref[...] = p[:, o:o + KV_WIDTH]
    o += KV_WIDTH
    bg_ref[...] = p[:, o:o + CONV_WIDTH]
    o += CONV_WIDTH
    u_ref[...] = p[:, o:o + CONV_WIDTH] * p[:, o + CONV_WIDTH:o + 2 * CONV_WIDTH]


def _inproj(x, g, w_in, e, tm):
    n = x.shape[0]
    row = lambda w: pl.BlockSpec((tm, w), lambda i: (i, 0))
    est = 2 * D_MODEL * IN_WIDTH * 2 + 2 * tm * (D_MODEL + IN_WIDTH) * 4 + 2 * tm * IN_WIDTH * 4
    return pl.pallas_call(
        _inproj_body,
        out_shape=(jax.ShapeDtypeStruct((n, ATTN_WIDTH), BF16),
                   jax.ShapeDtypeStruct((n, KV_WIDTH), F32),
                   jax.ShapeDtypeStruct((n, KV_WIDTH), F32),
                   jax.ShapeDtypeStruct((n, CONV_WIDTH), F32),
                   jax.ShapeDtypeStruct((n, CONV_WIDTH), F32)),
        grid=(n // tm,),
        in_specs=[row(D_MODEL), pl.BlockSpec((1, D_MODEL), lambda i: (0, 0)),
                  pl.BlockSpec((None, D_MODEL, IN_WIDTH), lambda i: (e, 0, 0))],
        out_specs=(row(ATTN_WIDTH), row(KV_WIDTH), row(KV_WIDTH), row(CONV_WIDTH), row(CONV_WIDTH)),
        compiler_params=pltpu.CompilerParams(
            dimension_semantics=("arbitrary",), vmem_limit_bytes=_vmem_limit(est)),
        name="inproj",
    )(x, g.reshape(1, D_MODEL), w_in)


PAIR_WIDTH = 2 * HEAD_DIM
PAIRS_PER_KV = Q_PER_KV // 2
ATTN_SCALE = HEAD_DIM ** -0.5


def _mixer_ab_body(x_ref, q_ref, k_ref, v_ref, kh_ref, vh_ref, bg_ref, u_ref, uh_ref,
                   bias_ref, sink_ref, cw_ref, wo_ref, o_ref,
                   kbuf, vbuf, ubuf, mix, sbuf, *, tq, cl, zero_history):
    t = pl.program_id(1)
    kw = WINDOW + cl
    low =lax.broadcasted_iota(jnp.int32, (1, KV_WIDTH), 1) < HEAD_DIM

    def stage(buf, src_ref, r0):
        a = src_ref[...]
        b = pltpu.roll(a, HEAD_DIM, axis=1)
        rows = a.shape[0]
        buf[0, 0, r0:r0 + rows, :] = jnp.where(low, a, 0.0).astype(BF16)
        buf[0, 1, r0:r0 + rows, :] = jnp.where(low, 0.0, b).astype(BF16)
        buf[1, 0, r0:r0 + rows, :] = jnp.where(low, b, 0.0).astype(BF16)
        buf[1, 1, r0:r0 + rows, :] = jnp.where(low, 0.0, a).astype(BF16)

    stage(kbuf, kh_ref, 0)
    stage(kbuf, k_ref, WINDOW)
    stage(vbuf, vh_ref, 0)
    stage(vbuf, v_ref, WINDOW)

    col = lax.broadcasted_iota(jnp.int32, (PAIRS_PER_KV * cl, kw), 1)

    n_chunks = tq // cl

    for c in range(n_chunks):
        r0 = c * cl
        for hh in range(N_KV_HEADS):
            ql = jnp.concatenate(
                [q_ref[r0:r0 + cl, (hh * PAIRS_PER_KV + pr) * PAIR_WIDTH:
                       (hh * PAIRS_PER_KV + pr + 1) * PAIR_WIDTH] for pr in range(PAIRS_PER_KV)],
                axis=0)
            for lh in range(2):
                kk = kbuf[hh, lh, r0:r0 + kw, :]
                sbuf[c, hh, lh] = lax.dot_general(ql, kk, (((1,), (1,)), ((), ())),
                                                  preferred_element_type=F32)

    @pl.when(t >= 0)
    def _():
        for c in range(n_chunks):
            r0 = c * cl
            for hh in range(N_KV_HEADS):
                acc = None
                for lh in range(2):
                    s = sbuf[c, hh, lh] + bias_ref[hh, lh]
                    if zero_history and r0 < WINDOW:
                        s = jnp.where(col + (t * tq + r0 - WINDOW) >= 0, s, NEG)
                    sk = sink_ref[hh, lh]
                    m = jnp.maximum(jnp.max(s, axis=-1, keepdims=True), sk)
                    p = jnp.exp(s - m)
                    den = jnp.sum(p, axis=-1, keepdims=True) + jnp.exp(sk - m)
                    vv = vbuf[hh, lh, r0:r0 + kw, :]
                    o = jnp.dot(p.astype(BF16), vv, preferred_element_type=F32) * (1.0 / den)
                    acc = o if acc is None else acc + o
                for pr in range(PAIRS_PER_KV):
                    c0 = (hh * PAIRS_PER_KV + pr) * PAIR_WIDTH
                    mix[r0:r0 + cl, c0:c0 + PAIR_WIDTH] = acc[pr * cl:(pr + 1) * cl].astype(BF16)

        _conv_and_project(x_ref, bg_ref, u_ref, uh_ref, cw_ref, wo_ref, o_ref, ubuf, mix,
                          t=t, tq=tq, zero_history=zero_history)


def _conv_and_project(x_ref, bg_ref, u_ref, uh_ref, cw_ref, wo_ref, o_ref, ubuf, mix,
                      *, t, tq, zero_history):
    uh = uh_ref[...]
    if zero_history:
        uh = jnp.where(t == 0, 0.0, uh)
    ubuf[0:CONV_HIST_ROWS, :] = uh
    ubuf[CONV_HIST_ROWS:CONV_HIST_ROWS + tq, :] = u_ref[...]
    conv = None
    for j in range(CONV_K):
        off = CONV_HIST_ROWS - (CONV_K - 1) + j
        term = cw_ref[j:j + 1, :] * ubuf[off:off + tq, :]
        conv = term if conv is None else conv + term
    mix[:, ATTN_WIDTH:MIX_WIDTH] = (bg_ref[...] * conv).astype(BF16)

    o_ref[...] = x_ref[...] + jnp.dot(mix[...], wo_ref[...], preferred_element_type=F32)


def _mixer_ab_t_body(x_ref, q_ref, k_ref, v_ref, kh_ref, vh_ref, bg_ref, u_ref, uh_ref,
                     biast_ref, sinkt_ref, cw_ref, wo_ref, o_ref,
                     kdup, vta, vtb, ubuf, mix, sbuf, pbuf, rbuf, *, tq, zero_history):
    t = pl.program_id(1)
    cl = CHUNK
    kw = WINDOW + cl
    w = WINDOW + tq
    low = lax.broadcasted_iota(jnp.int32, (1, KV_WIDTH), 1) < HEAD_DIM

    kf = jnp.concatenate([kh_ref[...], k_ref[...]], axis=0)
    kr = pltpu.roll(kf, HEAD_DIM, axis=1)
    kdup[0] = jnp.where(low, kf, kr).astype(BF16)
    kdup[1] = jnp.where(low, kr, kf).astype(BF16)

    vf = jnp.concatenate([vh_ref[...], v_ref[...], jnp.zeros((cl, KV_WIDTH), F32)], axis=0)
    vr = pltpu.roll(vf, HEAD_DIM, axis=1)
    for hh, vd in enumerate((jnp.where(low, vf, vr), jnp.where(low, vr, vf))):
        for blk in range(w // KV_WIDTH):
            b0 = blk * KV_WIDTH
            vta[hh, :, b0:b0 + KV_WIDTH] = vd[b0:b0 + KV_WIDTH].T.astype(BF16)
            vtb[hh, :, b0:b0 + KV_WIDTH] = vd[cl + b0:cl + b0 + KV_WIDTH].T.astype(BF16)

    n_cols = PAIRS_PER_KV * PAIR_WIDTH
    key = lax.broadcasted_iota(jnp.int32, (kw, n_cols), 0)
    rr = lax.broadcasted_iota(jnp.int32, (PAIR_WIDTH, PAIR_WIDTH), 0) < HEAD_DIM
    cc = lax.broadcasted_iota(jnp.int32, (PAIR_WIDTH, PAIR_WIDTH), 1) < HEAD_DIM
    diag = rr == cc
    for c in range(tq // cl):
        r0 = c * cl
        for hh in range(N_KV_HEADS):
            rows = []
            for pr in range(PAIRS_PER_KV):
                c0 = (hh * PAIRS_PER_KV + pr) * PAIR_WIDTH
                qp = q_ref[r0:r0 + cl, c0:c0 + PAIR_WIDTH]
                rows += [jnp.where(low, qp, 0), jnp.where(low, 0, qp)]
            qm = jnp.concatenate(rows, axis=0)
            s = lax.dot_general(kdup[hh, r0:r0 + kw, :], qm, (((1,), (1,)), ((), ())),
                                preferred_element_type=F32) + biast_ref[hh]
            if zero_history and r0 < WINDOW:
                s = jnp.where(key + (t * tq + r0 - WINDOW) >= 0, s, NEG)
            j0 = (c * N_KV_HEADS + hh) * n_cols
            sbuf[:, j0:j0 + n_cols] = s

    s = sbuf[...]
    sk = sinkt_ref[...]
    m = jnp.maximum(jnp.max(s, axis=0, keepdims=True), sk)
    p = jnp.exp(s - m)
    den = jnp.sum(p, axis=0, keepdims=True) + jnp.exp(sk - m)
    pbuf[...] = p.astype(BF16)
    rbuf[...] = 1.0 / den

    for c in range(tq // cl):
        r0 = c * cl
        k0 = r0 if c % 2 == 0 else r0 - cl
        vt_ref = vta if c % 2 == 0 else vtb
        for hh in range(N_KV_HEADS):
            j0 = (c * N_KV_HEADS + hh) * n_cols
            ot = jnp.dot(vt_ref[hh, :, k0:k0 + kw], pbuf[:, j0:j0 + n_cols],
                         preferred_element_type=F32) * rbuf[:, j0:j0 + n_cols]
            for pr in range(PAIRS_PER_KV):
                blk = jnp.where(diag, ot[:, pr * PAIR_WIDTH:(pr + 1) * PAIR_WIDTH], 0.0).T
                c0 = (hh * PAIRS_PER_KV + pr) * PAIR_WIDTH
                mix[r0:r0 + cl, c0:c0 + PAIR_WIDTH] = (blk[0:cl] + blk[cl:2 * cl]).astype(BF16)

    _conv_and_project(x_ref, bg_ref, u_ref, uh_ref, cw_ref, wo_ref, o_ref, ubuf, mix,
                      t=t, tq=tq, zero_history=zero_history)


def _mixer_ab(x, q, k, v, k_hist, v_hist, bg, u, u_hist, bias, sinks, conv_w, w_out, e,
              *, nb, t_len, tq, cl, zero_history):
    n = x.shape[0]
    nt = t_len // tq
    row = lambda w: pl.BlockSpec((tq, w), lambda b, t: (b * nt + t, 0))
    transposed = cl == CHUNK and tq % KV_WIDTH == 0
    if zero_history:
        kh_spec = pl.BlockSpec((WINDOW, KV_WIDTH), lambda b, t: (
            jnp.maximum((b * t_len + t * tq) // WINDOW - 1, 0), 0))
        uh_spec = pl.BlockSpec((CONV_HIST_ROWS, CONV_WIDTH), lambda b, t: (
            jnp.maximum((b * t_len + t * tq) // CONV_HIST_ROWS - 1, 0), 0))
    else:
        kh_spec = pl.BlockSpec((WINDOW, KV_WIDTH), lambda b, t: (b, 0))
        uh_spec = pl.BlockSpec((CONV_HIST_ROWS, CONV_WIDTH), lambda b, t: (b, 0))
    kw = WINDOW + cl
    b5 = bias[:, :cl, :kw].reshape(N_KV_HEADS, PAIRS_PER_KV, 2, cl, kw)
    s5 = jnp.broadcast_to(sinks.reshape(N_KV_HEADS, PAIRS_PER_KV, 2, 1, 1),
                          (N_KV_HEADS, PAIRS_PER_KV, 2, cl, 1))
    ubuf = pltpu.VMEM((CONV_HIST_ROWS + tq, CONV_WIDTH), F32)
    mix = pltpu.VMEM((tq, MIX_WIDTH), BF16)
    if transposed:
        n_cols = PAIRS_PER_KV * PAIR_WIDTH
        bias_l = jnp.transpose(b5, (0, 4, 1, 2, 3)).reshape(N_KV_HEADS, kw, n_cols)
        all_cols = (tq // cl) * N_KV_HEADS * n_cols
        sink_l = jnp.tile(s5.reshape(1, N_KV_HEADS * n_cols), (1, tq // cl))
        body = functools.partial(_mixer_ab_t_body, tq=tq, zero_history=zero_history)
        scratch = [pltpu.VMEM((N_KV_HEADS, WINDOW + tq, KV_WIDTH), BF16),
                   pltpu.VMEM((N_KV_HEADS, PAIR_WIDTH, WINDOW + tq), BF16),
                   pltpu.VMEM((N_KV_HEADS, PAIR_WIDTH, WINDOW + tq), BF16),
                   ubuf, mix,
                   pltpu.VMEM((kw, all_cols), F32), pltpu.VMEM((kw, all_cols), BF16),
                   pltpu.VMEM((1, all_cols), F32)]
    else:
        bias_l = jnp.transpose(b5, (0, 2, 1, 3, 4)).reshape(N_KV_HEADS, 2, PAIRS_PER_KV * cl, kw)
        sink_l = jnp.transpose(s5, (0, 2, 1, 3, 4)).reshape(N_KV_HEADS, 2, PAIRS_PER_KV * cl, 1)
        body = functools.partial(_mixer_ab_body, tq=tq, cl=cl, zero_history=zero_history)
        scratch = [pltpu.VMEM((N_KV_HEADS, 2, WINDOW + tq, KV_WIDTH), BF16),
                   pltpu.VMEM((N_KV_HEADS, 2, WINDOW + tq, KV_WIDTH), BF16),
                   ubuf, mix,
                   pltpu.VMEM((tq // cl, N_KV_HEADS, 2, PAIRS_PER_KV * cl, kw), F32)]
    const = lambda a: pl.BlockSpec(a.shape, lambda b, t: (0,) * a.ndim)
    est = 2 * tq * (2 * D_MODEL * 4 + ATTN_WIDTH * 2 + 2 * KV_WIDTH * 4 + 2 * CONV_WIDTH * 4) \
        + 2 * MIX_WIDTH * D_MODEL * 2 + 12 * tq * D_MODEL * 4
    return pl.pallas_call(
        body,
        out_shape=jax.ShapeDtypeStruct((n, D_MODEL), F32),
        grid=(nb, nt),
        in_specs=[row(D_MODEL), row(ATTN_WIDTH), row(KV_WIDTH), row(KV_WIDTH),
                  kh_spec, kh_spec, row(CONV_WIDTH), row(CONV_WIDTH), uh_spec,
                  const(bias_l), const(sink_l),
                  pl.BlockSpec((None, CONV_K, CONV_WIDTH), lambda b, t: (e, 0, 0)),
                  pl.BlockSpec((None, MIX_WIDTH, D_MODEL), lambda b, t: (e, 0, 0))],
        out_specs=row(D_MODEL),
        scratch_shapes=scratch,
        compiler_params=pltpu.CompilerParams(
            dimension_semantics=("arbitrary", "arbitrary"), vmem_limit_bytes=_vmem_limit(est)),
        name="mixer_ab",
    )(x, q, k, v, k_hist, v_hist, bg, u, u_hist, bias_l, sink_l, conv_w, w_out)


def _mixer_c_body(x_ref, hist_ref, g_ref, pw_ref, sc_ref, o_ref, st_ref, hbuf,
                  *, tp, first_pos, hist_is_input):
    t = pl.program_id(1)
    x = x_ref[...]
    g = g_ref[...]
    h = _rms(x, g)
    if hist_is_input:
        hist = jnp.where(t == 0, 0.0, _rms(hist_ref[...], g))
    else:
        hist = hist_ref[...]
    hbuf[0:POOL_HIST_ROWS, :] = hist
    hbuf[POOL_HIST_ROWS:POOL_HIST_ROWS + tp, :] = h

    sums = []
    cur = hbuf[...]
    for gi, w in enumerate(POOL_SIZES):
        assert w == 2 ** (gi + 1)
        cur = cur + pltpu.roll(cur, w // 2, axis=0)
        sums.append(cur[POOL_HIST_ROWS:POOL_HIST_ROWS + tp, 0:POOL_GROUP])
        if gi + 1 < len(POOL_SIZES):
            cur = cur[:, POOL_GROUP:]

    pos = first_pos + t * tp + lax.broadcasted_iota(jnp.int32, (tp, 1), 0)
    for gi, w in enumerate(POOL_SIZES):
        sl = slice(gi * POOL_GROUP, (gi + 1) * POOL_GROUP)
        acc = sums[gi]
        cnt = jnp.minimum(pos + 1, w).astype(F32)
        d = (acc / cnt - h[:, sl]).astype(BF16)
        y = jnp.dot(d, pw_ref[gi], preferred_element_type=F32)
        o_ref[:, sl] = x[:, sl] + y * sc_ref[:, sl]

    @pl.when(t == pl.num_programs(1) - 1)
    def _():
        st_ref[...] = h[tp - POOL_HIST_ROWS:tp, :]


def _mixer_c(x, hist, g, pool_w, scale, o, *, nb, t_len, tp, first_pos, hist_is_input):
    n = x.shape[0]
    nt = t_len // tp
    row = pl.BlockSpec((tp, D_MODEL), lambda b, t: (b * nt + t, 0))
    if hist_is_input:
        hist_spec = pl.BlockSpec((POOL_HIST_ROWS, D_MODEL), lambda b, t: (
            jnp.maximum((b * t_len + t * tp) // POOL_HIST_ROWS - 1, 0), 0))
    else:
        hist_spec = pl.BlockSpec((POOL_HIST_ROWS, D_MODEL), lambda b, t: (b, 0))
    vec = pl.BlockSpec((1, D_MODEL), lambda b, t: (0, 0))
    est = 6 * tp * D_MODEL * 4 + 8 * tp * D_MODEL * 4
    return pl.pallas_call(
        functools.partial(_mixer_c_body, tp=tp, first_pos=first_pos, hist_is_input=hist_is_input),
        out_shape=(jax.ShapeDtypeStruct((n, D_MODEL), F32),
                   jax.ShapeDtypeStruct((nb * POOL_HIST_ROWS, D_MODEL), F32)),
        grid=(nb, nt),
        in_specs=[row, hist_spec, vec,
                  pl.BlockSpec((None, len(POOL_SIZES), POOL_GROUP, POOL_GROUP),
                               lambda b, t: (o, 0, 0, 0)),
                  pl.BlockSpec((None, 1, D_MODEL), lambda b, t: (o, 0, 0))],
        out_specs=(row, pl.BlockSpec((POOL_HIST_ROWS, D_MODEL), lambda b, t: (b, 0))),
        scratch_shapes=[pltpu.VMEM((POOL_HIST_ROWS + tp, D_MODEL), F32)],
        compiler_params=pltpu.CompilerParams(
            dimension_semantics=("arbitrary", "arbitrary"), vmem_limit_bytes=_vmem_limit(est)),
        name="mixer_c",
    )(x, hist, g.reshape(1, D_MODEL), pool_w, scale.reshape(scale.shape[0], 1, D_MODEL))


def _trunk(x3, caches, first_pos, w, bias):
    nb, t_len, _ = x3.shape
    prompt = caches is None
    x = x3.reshape(nb * t_len, D_MODEL)
    n = x.shape[0]
    tm = min(ROW_TILE, n)
    tf = min(FFN_ROW_TILE, n)
    tq = min(ROW_TILE, t_len)
    cl = min(CHUNK, t_len)
    new_k, new_v, new_conv, new_pool = [], [], [], []
    for layer in range(DEPTH):
        last = layer == DEPTH - 1
        x = _ffn(x, w["norm_g"][layer, 0], w["wg"], w["wu"], w["wd"], layer, 0,
                 w["final_norm_g"], False, tf)
        if layer % 2 == 0:
            e = layer // 2
            q, k, v, bg, u = _inproj(x, w["norm_g"][layer, 1], w["w_in"], e, tm)
            if prompt:
                k_hist, v_hist, u_hist = k, v, u
            else:
                ck, cv, cc, _ = caches
                k_hist = ck[e].reshape(nb * WINDOW, KV_WIDTH)
                v_hist = cv[e].reshape(nb * WINDOW, KV_WIDTH)
                u_hist = jnp.pad(cc[e], ((0, 0), (CONV_HIST_ROWS - (CONV_K - 1), 0), (0, 0))
                                 ).reshape(nb * CONV_HIST_ROWS, CONV_WIDTH)
            x = _mixer_ab(x, q, k, v, k_hist, v_hist, bg, u, u_hist, bias, w["attn_sinks"][e],
                          w["conv_w"], w["w_out"], e,
                          nb=nb, t_len=t_len, tq=tq, cl=cl, zero_history=prompt)
            keep = min(WINDOW, t_len)
            new_k.append(k.reshape(nb, t_len, N_KV_HEADS, HEAD_DIM)[:, t_len - keep:])
            new_v.append(v.reshape(nb, t_len, N_KV_HEADS, HEAD_DIM)[:, t_len - keep:])
            new_conv.append(u.reshape(nb, t_len, CONV_WIDTH)[:, t_len - (CONV_K - 1):])
        else:
            o = layer // 2
            if prompt:
                hist = x
            else:
                hist = jnp.pad(caches[3][o], ((0, 0), (POOL_HIST_ROWS - (POOL_MAX - 1), 0), (0, 0))
                               ).reshape(nb * POOL_HIST_ROWS, D_MODEL)
            x, st = _mixer_c(x, hist, w["norm_g"][layer, 1], w["pool_w"], w["pool_scale"], o,
                             nb=nb, t_len=t_len, tp=tq, first_pos=first_pos, hist_is_input=prompt)
            new_pool.append(st.reshape(nb, POOL_HIST_ROWS, D_MODEL)[:, POOL_HIST_ROWS - (POOL_MAX - 1):])
        x = _ffn(x, w["norm_g"][layer, 2], w["wg"], w["wu"], w["wd"], layer, 1,
                 w["final_norm_g"], last, tf)
    return (x.reshape(nb, t_len, D_MODEL), jnp.stack(new_k), jnp.stack(new_v),
            jnp.stack(new_conv), jnp.stack(new_pool))


def kernel(x_prompt, x_sample, cache_attn_k, cache_attn_v, cache_conv, cache_pool, norm_g, ffn_w_gate, ffn_w_up, ffn_w_down, mix_w_in, mix_w_out, conv_w, attn_sinks, rel_bias_table, pool_w, pool_scale, final_norm_g):
    past_len = 1024
    w = dict(norm_g=norm_g, final_norm_g=final_norm_g,
             wg=ffn_w_gate.astype(BF16), wu=ffn_w_up.astype(BF16), wd=ffn_w_down.astype(BF16),
             w_in=mix_w_in.astype(BF16), w_out=mix_w_out.astype(BF16), conv_w=conv_w,
             attn_sinks=attn_sinks, pool_w=pool_w.astype(BF16), pool_scale=pool_scale)
    bias = _relbias(rel_bias_table)
    y_p, k_p, v_p, conv_p, pool_p = _trunk(x_prompt, None, 0, w, bias)
    y_s, k_s, v_s, conv_s, pool_s = _trunk(
        x_sample, (cache_attn_k, cache_attn_v, cache_conv, cache_pool), past_len, w, bias)
    return (y_p, y_s, k_p, v_p, conv_p, pool_p, k_s, v_s, conv_s, pool_s)
```

```python
import functools
import math

import jax
import jax.numpy as jnp
from jax import lax
from jax.experimental import pallas as pl
from jax.experimental.pallas import tpu as pltpu

D_MODEL = 1024
DEPTH = 2
CHUNK = 64
N_HEADS = 8
N_KV_HEADS = 2
Q_PER_KV = N_HEADS // N_KV_HEADS
HEAD_DIM = 64
ATTN_WIDTH = N_HEADS * HEAD_DIM
KV_WIDTH = N_KV_HEADS * HEAD_DIM
WINDOW = 128
N_BUCKETS = 32
MAX_DISTANCE = 128
CONV_WIDTH = D_MODEL // 2
CONV_K = 3
IN_WIDTH = ATTN_WIDTH + 2 * KV_WIDTH + 3 * CONV_WIDTH
MIX_WIDTH = ATTN_WIDTH + CONV_WIDTH
POOL_SIZES = (2, 4, 8, 16)
POOL_GROUP = D_MODEL // len(POOL_SIZES)
POOL_MAX = max(POOL_SIZES)
D_FF = 2816
EPS = 1e-6
NEG = -1e30

V7X_VMEM_BYTES = 64 * 1024 * 1024
SUBLANES_F32 = 8
CONV_HIST_ROWS = SUBLANES_F32
POOL_HIST_ROWS = 16
ROW_TILE = 512
FFN_ROW_TILE = 1024
FF_CHUNK = 512
FF_CHUNKS = tuple((c0, min(FF_CHUNK, D_FF - c0)) for c0 in range(0, D_FF, FF_CHUNK))

F32 = jnp.float32
BF16 = jnp.bfloat16


def _vmem_limit(nbytes):
    return int(min(V7X_VMEM_BYTES - (4 << 20), max(32 << 20, nbytes)))


def _rms(x, g):
    return x * lax.rsqrt(jnp.mean(x * x, axis=-1, keepdims=True) + EPS) * g


def _relbias_body(tab_ref, bkt_ref, o_ref):
    bkt = bkt_ref[...]
    for h in range(N_HEADS):
        acc = jnp.zeros(bkt.shape, F32)
        for b in range(N_BUCKETS):
            acc = jnp.where(bkt == b, tab_ref[b, h], acc)
        o_ref[h] = acc


def _rel_bucket(rel):
    half = N_BUCKETS // 2
    ret = jnp.where(rel > 0, half, 0)
    n = jnp.abs(rel)
    max_exact = half // 2
    nf = jnp.maximum(n, 1).astype(F32)
    large = max_exact + (jnp.log(nf / max_exact) / math.log(MAX_DISTANCE / max_exact)
                         * (half - max_exact)).astype(jnp.int32)
    large = jnp.minimum(large, half - 1)
    return ret + jnp.where(n < max_exact, n, large)


def _relbias(table):
    rel = (jnp.arange(WINDOW + CHUNK, dtype=jnp.int32) - WINDOW)[None, :] \
        - jnp.arange(CHUNK, dtype=jnp.int32)[:, None]
    bucket = _rel_bucket(rel).astype(jnp.int32)
    return pl.pallas_call(
        _relbias_body,
        out_shape=jax.ShapeDtypeStruct((N_HEADS, CHUNK, WINDOW + CHUNK), F32),
        in_specs=[pl.BlockSpec(memory_space=pltpu.SMEM),
                  pl.BlockSpec(memory_space=pltpu.VMEM)],
        out_specs=pl.BlockSpec(memory_space=pltpu.VMEM),
        name="relbias",
    )(table, bucket)


def _swiglu_rows(x, g, wg_ref, wu_ref, wd_ref, a_ref, r0):
    rows = x.shape[0]
    xn = _rms(x, g).astype(BF16)
    for c0, cw in FF_CHUNKS:
        gate = jnp.dot(xn, wg_ref[:, c0:c0 + cw], preferred_element_type=F32)
        up = jnp.dot(xn, wu_ref[:, c0:c0 + cw], preferred_element_type=F32)
        a_ref[r0:r0 + rows, c0:c0 + cw] = (jax.nn.silu(gate) * up).astype(BF16)
    y = jnp.dot(a_ref[r0:r0 + rows, :], wd_ref[...], preferred_element_type=F32)
    return x + 0.5 * y


def _ffn_body(x_ref, g_ref, wg_ref, wu_ref, wd_ref, fg_ref, o_ref, a_ref, *, final_norm):
    out = _swiglu_rows(x_ref[...], g_ref[...], wg_ref, wu_ref, wd_ref, a_ref, 0)
    if final_norm:
        out = _rms(out, fg_ref[...])
    o_ref[...] = out


def _ffn(x, g, wg, wu, wd, layer, j, fg, final_norm, tm):
    n = x.shape[0]
    wspec_in = pl.BlockSpec((None, None, D_MODEL, D_FF), lambda i: (layer, j, 0, 0),
                            pipeline_mode=pl.Buffered(1))
    wspec_out = pl.BlockSpec((None, None, D_FF, D_MODEL), lambda i: (layer, j, 0, 0),
                             pipeline_mode=pl.Buffered(1))
    vec = pl.BlockSpec((1, D_MODEL), lambda i: (0, 0))
    est = 3 * D_MODEL * D_FF * 2 + 4 * tm * D_MODEL * 4 + tm * D_FF * 2 + 5 * tm * 1024 * 4
    return pl.pallas_call(
        functools.partial(_ffn_body, final_norm=final_norm),
        out_shape=jax.ShapeDtypeStruct((n, D_MODEL), F32),
        grid=(n // tm,),
        in_specs=[pl.BlockSpec((tm, D_MODEL), lambda i: (i, 0)), vec,
                  wspec_in, wspec_in, wspec_out, vec],
        out_specs=pl.BlockSpec((tm, D_MODEL), lambda i: (i, 0)),
        scratch_shapes=[pltpu.VMEM((tm, D_FF), BF16)],
        compiler_params=pltpu.CompilerParams(
            dimension_semantics=("arbitrary",), vmem_limit_bytes=_vmem_limit(est)),
        name="ffn",
    )(x, g.reshape(1, D_MODEL), wg, wu, wd, fg.reshape(1, D_MODEL))


def _inproj_body(x_ref, g_ref, w_ref, q_ref, k_ref, v_ref, bg_ref, u_ref):
    h = _rms(x_ref[...], g_ref[...]).astype(BF16)
    p = jnp.dot(h, w_ref[...], preferred_element_type=F32)
    o = 0
    q_ref[...] = (p[:, o:o + ATTN_WIDTH] * (HEAD_DIM ** -0.5)).astype(BF16)
    o += ATTN_WIDTH
    k_ref[...] = p[:, o:o + KV_WIDTH]
    o += KV_WIDTH
    v_ref[...] = p[:, o:o + KV_WIDTH]
    o += KV_WIDTH
    bg_ref[...] = p[:, o:o + CONV_WIDTH]
    o += CONV_WIDTH
    u_ref[...] = p[:, o:o + CONV_WIDTH] * p[:, o + CONV_WIDTH:o + 2 * CONV_WIDTH]


def _inproj(x, g, w_in, e, tm):
    n = x.shape[0]
    row = lambda w: pl.BlockSpec((tm, w), lambda i: (i, 0))
    est = 2 * D_MODEL * IN_WIDTH * 2 + 2 * tm * (D_MODEL + IN_WIDTH) * 4 + 2 * tm * IN_WIDTH * 4
    return pl.pallas_call(
        _inproj_body,
        out_shape=(jax.ShapeDtypeStruct((n, ATTN_WIDTH), BF16),
                   jax.ShapeDtypeStruct((n, KV_WIDTH), F32),
                   jax.ShapeDtypeStruct((n, KV_WIDTH), F32),
                   jax.ShapeDtypeStruct((n, CONV_WIDTH), F32),
                   jax.ShapeDtypeStruct((n, CONV_WIDTH), F32)),
        grid=(n // tm,),
        in_specs=[row(D_MODEL), pl.BlockSpec((1, D_MODEL), lambda i: (0, 0)),
                  pl.BlockSpec((None, D_MODEL, IN_WIDTH), lambda i: (e, 0, 0))],
        out_specs=(row(ATTN_WIDTH), row(KV_WIDTH), row(KV_WIDTH), row(CONV_WIDTH), row(CONV_WIDTH)),
        compiler_params=pltpu.CompilerParams(
            dimension_semantics=("arbitrary",), vmem_limit_bytes=_vmem_limit(est)),
        name="inproj",
    )(x, g.reshape(1, D_MODEL), w_in)


PAIR_WIDTH = 2 * HEAD_DIM
PAIRS_PER_KV = Q_PER_KV // 2
ATTN_SCALE = HEAD_DIM ** -0.5


def _mixer_ab_body(x_ref, q_ref, k_ref, v_ref, kh_ref, vh_ref, bg_ref, u_ref, uh_ref,
                   bias_ref, sink_ref, cw_ref, wo_ref, o_ref,
                   kbuf, vbuf, ubuf, mix, sbuf, *, tq, cl, zero_history):
    t = pl.program_id(1)
    kw = WINDOW + cl
    low =lax.broadcasted_iota(jnp.int32, (1, KV_WIDTH), 1) < HEAD_DIM

    def stage(buf, src_ref, r0):
        a = src_ref[...]
        b = pltpu.roll(a, HEAD_DIM, axis=1)
        rows = a.shape[0]
        buf[0, 0, r0:r0 + rows, :] = jnp.where(low, a, 0.0).astype(BF16)
        buf[0, 1, r0:r0 + rows, :] = jnp.where(low, 0.0, b).astype(BF16)
        buf[1, 0, r0:r0 + rows, :] = jnp.where(low, b, 0.0).astype(BF16)
        buf[1, 1, r0:r0 + rows, :] = jnp.where(low, 0.0, a).astype(BF16)

    stage(kbuf, kh_ref, 0)
    stage(kbuf, k_ref, WINDOW)
    stage(vbuf, vh_ref, 0)
    stage(vbuf, v_ref, WINDOW)

    col = lax.broadcasted_iota(jnp.int32, (PAIRS_PER_KV * cl, kw), 1)

    n_chunks = tq // cl

    for c in range(n_chunks):
        r0 = c * cl
        for hh in range(N_KV_HEADS):
            ql = jnp.concatenate(
                [q_ref[r0:r0 + cl, (hh * PAIRS_PER_KV + pr) * PAIR_WIDTH:
                       (hh * PAIRS_PER_KV + pr + 1) * PAIR_WIDTH] for pr in range(PAIRS_PER_KV)],
                axis=0)
            for lh in range(2):
                kk = kbuf[hh, lh, r0:r0 + kw, :]
                sbuf[c, hh, lh] = lax.dot_general(ql, kk, (((1,), (1,)), ((), ())),
                                                  preferred_element_type=F32)

    @pl.when(t >= 0)
    def _():
        for c in range(n_chunks):
            r0 = c * cl
            for hh in range(N_KV_HEADS):
                acc = None
                for lh in range(2):
                    s = sbuf[c, hh, lh] + bias_ref[hh, lh]
                    if zero_history and r0 < WINDOW:
                        s = jnp.where(col + (t * tq + r0 - WINDOW) >= 0, s, NEG)
                    sk = sink_ref[hh, lh]
                    m = jnp.maximum(jnp.max(s, axis=-1, keepdims=True), sk)
                    p = jnp.exp(s - m)
                    den = jnp.sum(p, axis=-1, keepdims=True) + jnp.exp(sk - m)
                    vv = vbuf[hh, lh, r0:r0 + kw, :]
                    o = jnp.dot(p.astype(BF16), vv, preferred_element_type=F32) * (1.0 / den)
                    acc = o if acc is None else acc + o
                for pr in range(PAIRS_PER_KV):
                    c0 = (hh * PAIRS_PER_KV + pr) * PAIR_WIDTH
                    mix[r0:r0 + cl, c0:c0 + PAIR_WIDTH] = acc[pr * cl:(pr + 1) * cl].astype(BF16)

        _conv_and_project(x_ref, bg_ref, u_ref, uh_ref, cw_ref, wo_ref, o_ref, ubuf, mix,
                          t=t, tq=tq, zero_history=zero_history)


def _conv_and_project(x_ref, bg_ref, u_ref, uh_ref, cw_ref, wo_ref, o_ref, ubuf, mix,
                      *, t, tq, zero_history):
    uh = uh_ref[...]
    if zero_history:
        uh = jnp.where(t == 0, 0.0, uh)
    ubuf[0:CONV_HIST_ROWS, :] = uh
    ubuf[CONV_HIST_ROWS:CONV_HIST_ROWS + tq, :] = u_ref[...]
    conv = None
    for j in range(CONV_K):
        off = CONV_HIST_ROWS - (CONV_K - 1) + j
        term = cw_ref[j:j + 1, :] * ubuf[off:off + tq, :]
        conv = term if conv is None else conv + term
    mix[:, ATTN_WIDTH:MIX_WIDTH] = (bg_ref[...] * conv).astype(BF16)

    o_ref[...] = x_ref[...] + jnp.dot(mix[...], wo_ref[...], preferred_element_type=F32)


def _mixer_ab_t_body(x_ref, q_ref, k_ref, v_ref, kh_ref, vh_ref, bg_ref, u_ref, uh_ref,
                     biast_ref, sinkt_ref, cw_ref, wo_ref, o_ref,
                     kdup, vta, vtb, ubuf, mix, sbuf, pbuf, rbuf, *, tq, zero_history):
    t = pl.program_id(1)
    cl = CHUNK
    kw = WINDOW + cl
    w = WINDOW + tq
    low = lax.broadcasted_iota(jnp.int32, (1, KV_WIDTH), 1) < HEAD_DIM

    kf = jnp.concatenate([kh_ref[...], k_ref[...]], axis=0)
    kr = pltpu.roll(kf, HEAD_DIM, axis=1)
    kdup[0] = jnp.where(low, kf, kr).astype(BF16)
    kdup[1] = jnp.where(low, kr, kf).astype(BF16)

    vf = jnp.concatenate([vh_ref[...], v_ref[...], jnp.zeros((cl, KV_WIDTH), F32)], axis=0)
    vr = pltpu.roll(vf, HEAD_DIM, axis=1)
    for hh, vd in enumerate((jnp.where(low, vf, vr), jnp.where(low, vr, vf))):
        for blk in range(w // KV_WIDTH):
            b0 = blk * KV_WIDTH
            vta[hh, :, b0:b0 + KV_WIDTH] = vd[b0:b0 + KV_WIDTH].T.astype(BF16)
            vtb[hh, :, b0:b0 + KV_WIDTH] = vd[cl + b0:cl + b0 + KV_WIDTH].T.astype(BF16)

    n_cols = PAIRS_PER_KV * PAIR_WIDTH
    key = lax.broadcasted_iota(jnp.int32, (kw, n_cols), 0)
    rr = lax.broadcasted_iota(jnp.int32, (PAIR_WIDTH, PAIR_WIDTH), 0) < HEAD_DIM
    cc = lax.broadcasted_iota(jnp.int32, (PAIR_WIDTH, PAIR_WIDTH), 1) < HEAD_DIM
    diag = rr == cc
    for c in range(tq // cl):
        r0 = c * cl
        for hh in range(N_KV_HEADS):
            rows = []
            for pr in range(PAIRS_PER_KV):
                c0 = (hh * PAIRS_PER_KV + pr) * PAIR_WIDTH
                qp = q_ref[r0:r0 + cl, c0:c0 + PAIR_WIDTH]
                rows += [jnp.where(low, qp, 0), jnp.where(low, 0, qp)]
            qm = jnp.concatenate(rows, axis=0)
            s = lax.dot_general(kdup[hh, r0:r0 + kw, :], qm, (((1,), (1,)), ((), ())),
                                preferred_element_type=F32) + biast_ref[hh]
            if zero_history and r0 < WINDOW:
                s = jnp.where(key + (t * tq + r0 - WINDOW) >= 0, s, NEG)
            j0 = (c * N_KV_HEADS + hh) * n_cols
            sbuf[:, j0:j0 + n_cols] = s

    s = sbuf[...]
    sk = sinkt_ref[...]
    m = jnp.maximum(jnp.max(s, axis=0, keepdims=True), sk)
    p = jnp.exp(s - m)
    den = jnp.sum(p, axis=0, keepdims=True) + jnp.exp(sk - m)
    pbuf[...] = p.astype(BF16)
    rbuf[...] = 1.0 / den

    for c in range(tq // cl):
        r0 = c * cl
        k0 = r0 if c % 2 == 0 else r0 - cl
        vt_ref = vta if c % 2 == 0 else vtb
        for hh in range(N_KV_HEADS):
            j0 = (c * N_KV_HEADS + hh) * n_cols
            ot = jnp.dot(vt_ref[hh, :, k0:k0 + kw], pbuf[:, j0:j0 + n_cols],
                         preferred_element_type=F32) * rbuf[:, j0:j0 + n_cols]
            for pr in range(PAIRS_PER_KV):
                blk = jnp.where(diag, ot[:, pr * PAIR_WIDTH:(pr + 1) * PAIR_WIDTH], 0.0).T
                c0 = (hh * PAIRS_PER_KV + pr) * PAIR_WIDTH
                mix[r0:r0 + cl, c0:c0 + PAIR_WIDTH] = (blk[0:cl] + blk[cl:2 * cl]).astype(BF16)

    _conv_and_project(x_ref, bg_ref, u_ref, uh_ref, cw_ref, wo_ref, o_ref, ubuf, mix,
                      t=t, tq=tq, zero_history=zero_history)


def _mixer_ab(x, q, k, v, k_hist, v_hist, bg, u, u_hist, bias, sinks, conv_w, w_out, e,
              *, nb, t_len, tq, cl, zero_history):
    n = x.shape[0]
    nt = t_len // tq
    row = lambda w: pl.BlockSpec((tq, w), lambda b, t: (b * nt + t, 0))
    transposed = cl == CHUNK and tq % KV_WIDTH == 0
    if zero_history:
        kh_spec = pl.BlockSpec((WINDOW, KV_WIDTH), lambda b, t: (
            jnp.maximum((b * t_len + t * tq) // WINDOW - 1, 0), 0))
        uh_spec = pl.BlockSpec((CONV_HIST_ROWS, CONV_WIDTH), lambda b, t: (
            jnp.maximum((b * t_len + t * tq) // CONV_HIST_ROWS - 1, 0), 0))
    else:
        kh_spec = pl.BlockSpec((WINDOW, KV_WIDTH), lambda b, t: (b, 0))
        uh_spec = pl.BlockSpec((CONV_HIST_ROWS, CONV_WIDTH), lambda b, t: (b, 0))
    kw = WINDOW + cl
    b5 = bias[:, :cl, :kw].reshape(N_KV_HEADS, PAIRS_PER_KV, 2, cl, kw)
    s5 = jnp.broadcast_to(sinks.reshape(N_KV_HEADS, PAIRS_PER_KV, 2, 1, 1),
                          (N_KV_HEADS, PAIRS_PER_KV, 2, cl, 1))
    ubuf = pltpu.VMEM((CONV_HIST_ROWS + tq, CONV_WIDTH), F32)
    mix = pltpu.VMEM((tq, MIX_WIDTH), BF16)
    if transposed:
        n_cols = PAIRS_PER_KV * PAIR_WIDTH
        bias_l = jnp.transpose(b5, (0, 4, 1, 2, 3)).reshape(N_KV_HEADS, kw, n_cols)
        all_cols = (tq // cl) * N_KV_HEADS * n_cols
        sink_l = jnp.tile(s5.reshape(1, N_KV_HEADS * n_cols), (1, tq // cl))
        body = functools.partial(_mixer_ab_t_body, tq=tq, zero_history=zero_history)
        scratch = [pltpu.VMEM((N_KV_HEADS, WINDOW + tq, KV_WIDTH), BF16),
                   pltpu.VMEM((N_KV_HEADS, PAIR_WIDTH, WINDOW + tq), BF16),
                   pltpu.VMEM((N_KV_HEADS, PAIR_WIDTH, WINDOW + tq), BF16),
                   ubuf, mix,
                   pltpu.VMEM((kw, all_cols), F32), pltpu.VMEM((kw, all_cols), BF16),
                   pltpu.VMEM((1, all_cols), F32)]
    else:
        bias_l = jnp.transpose(b5, (0, 2, 1, 3, 4)).reshape(N_KV_HEADS, 2, PAIRS_PER_KV * cl, kw)
        sink_l = jnp.transpose(s5, (0, 2, 1, 3, 4)).reshape(N_KV_HEADS, 2, PAIRS_PER_KV * cl, 1)
        body = functools.partial(_mixer_ab_body, tq=tq, cl=cl, zero_history=zero_history)
        scratch = [pltpu.VMEM((N_KV_HEADS, 2, WINDOW + tq, KV_WIDTH), BF16),
                   pltpu.VMEM((N_KV_HEADS, 2, WINDOW + tq, KV_WIDTH), BF16),
                   ubuf, mix,
                   pltpu.VMEM((tq // cl, N_KV_HEADS, 2, PAIRS_PER_KV * cl, kw), F32)]
    const = lambda a: pl.BlockSpec(a.shape, lambda b, t: (0,) * a.ndim)
    est = 2 * tq * (2 * D_MODEL * 4 + ATTN_WIDTH * 2 + 2 * KV_WIDTH * 4 + 2 * CONV_WIDTH * 4) \
        + 2 * MIX_WIDTH * D_MODEL * 2 + 12 * tq * D_MODEL * 4
    return pl.pallas_call(
        body,
        out_shape=jax.ShapeDtypeStruct((n, D_MODEL), F32),
        grid=(nb, nt),
        in_specs=[row(D_MODEL), row(ATTN_WIDTH), row(KV_WIDTH), row(KV_WIDTH),
                  kh_spec, kh_spec, row(CONV_WIDTH), row(CONV_WIDTH), uh_spec,
                  const(bias_l), const(sink_l),
                  pl.BlockSpec((None, CONV_K, CONV_WIDTH), lambda b, t: (e, 0, 0)),
                  pl.BlockSpec((None, MIX_WIDTH, D_MODEL), lambda b, t: (e, 0, 0))],
        out_specs=row(D_MODEL),
        scratch_shapes=scratch,
        compiler_params=pltpu.CompilerParams(
            dimension_semantics=("arbitrary", "arbitrary"), vmem_limit_bytes=_vmem_limit(est)),
        name="mixer_ab",
    )(x, q, k, v, k_hist, v_hist, bg, u, u_hist, bias_l, sink_l, conv_w, w_out)


def _pool_rows(x, hist, g, pw_ref, sc_ref, o_ref, r0, pos0):
    rows = x.shape[0]
    h = _rms(x, g)
    sums = []
    cur = jnp.concatenate([hist, h], axis=0)
    for gi, w in enumerate(POOL_SIZES):
        assert w == 2 ** (gi + 1)
        cur = cur + pltpu.roll(cur, w // 2, axis=0)
        sums.append(cur[POOL_HIST_ROWS:POOL_HIST_ROWS + rows, 0:POOL_GROUP])
        if gi + 1 < len(POOL_SIZES):
            cur = cur[:, POOL_GROUP:]

    pos = pos0 + lax.broadcasted_iota(jnp.int32, (rows, 1), 0)
    for gi, w in enumerate(POOL_SIZES):
        sl = slice(gi * POOL_GROUP, (gi + 1) * POOL_GROUP)
        cnt = jnp.minimum(pos + 1, w).astype(F32)
        d = (sums[gi] / cnt - h[:, sl]).astype(BF16)
        y = jnp.dot(d, pw_ref[gi], preferred_element_type=F32)
        o_ref[r0:r0 + rows, sl] = x[:, sl] + y * sc_ref[:, sl]
    return h[rows - POOL_HIST_ROWS:rows, :]


def _mixer_c_body(x_ref, hist_ref, g_ref, pw_ref, sc_ref, o_ref, st_ref, *, first_pos):
    st_ref[...] = _pool_rows(x_ref[...], hist_ref[...], g_ref[...], pw_ref, sc_ref, o_ref, 0, first_pos)


def _pool_specs(o):
    return [pl.BlockSpec((None, len(POOL_SIZES), POOL_GROUP, POOL_GROUP), lambda *a: (o, 0, 0, 0)),
            pl.BlockSpec((None, 1, D_MODEL), lambda *a: (o, 0, 0))]


def _mixer_c(x, hist, g, pool_w, scale, o, *, nb, t_len, first_pos):
    n = x.shape[0]
    row = pl.BlockSpec((t_len, D_MODEL), lambda b: (b, 0))
    hist_spec = pl.BlockSpec((POOL_HIST_ROWS, D_MODEL), lambda b: (b, 0))
    return pl.pallas_call(
        functools.partial(_mixer_c_body, first_pos=first_pos),
        out_shape=(jax.ShapeDtypeStruct((n, D_MODEL), F32),
                   jax.ShapeDtypeStruct((nb * POOL_HIST_ROWS, D_MODEL), F32)),
        grid=(nb,),
        in_specs=[row, hist_spec, pl.BlockSpec((1, D_MODEL), lambda b: (0, 0))] + _pool_specs(o),
        out_specs=(row, hist_spec),
        compiler_params=pltpu.CompilerParams(dimension_semantics=("arbitrary",)),
        name="mixer_c",
    )(x, hist, g.reshape(1, D_MODEL), pool_w, scale.reshape(scale.shape[0], 1, D_MODEL))


def _ffn_pool_body(x_ref, g1_ref, wg_ref, wu_ref, wd_ref, g2_ref, pw_ref, sc_ref,
                   o_ref, st_ref, a_ref, carry_ref, *, tm, sub, tiles_per_seq):
    i = pl.program_id(0)
    t = i % tiles_per_seq

    @pl.when(i == 0)
    def _():
        carry_ref[...] = jnp.zeros_like(carry_ref)

    hist = jnp.where(t == 0, 0.0, carry_ref[...])
    g1 = g1_ref[...]
    g2 = g2_ref[...]
    blocks = [_swiglu_rows(x_ref[r0:r0 + sub, :], g1, wg_ref, wu_ref, wd_ref, a_ref, r0)
              for r0 in range(0, tm, sub)]
    for s, xb in enumerate(blocks):
        hist = _pool_rows(xb, hist, g2, pw_ref, sc_ref, o_ref, s * sub, t * tm + s * sub)
    carry_ref[...] = hist
    st_ref[...] = hist


def _ffn_pool(x, g1, wg, wu, wd, layer, j, g2, pool_w, scale, o, *, nb, t_len, tm, sub):
    n = x.shape[0]
    tiles_per_seq = t_len // tm
    wspec_in = pl.BlockSpec((None, None, D_MODEL, D_FF), lambda i: (layer, j, 0, 0),
                            pipeline_mode=pl.Buffered(1))
    wspec_out = pl.BlockSpec((None, None, D_FF, D_MODEL), lambda i: (layer, j, 0, 0),
                             pipeline_mode=pl.Buffered(1))
    vec = pl.BlockSpec((1, D_MODEL), lambda i: (0, 0))
    row = pl.BlockSpec((tm, D_MODEL), lambda i: (i, 0))
    est = 3 * D_MODEL * D_FF * 2 + 4 * tm * D_MODEL * 4 + tm * D_FF * 2 + 6 * tm * 1024 * 4
    return pl.pallas_call(
        functools.partial(_ffn_pool_body, tm=tm, sub=sub, tiles_per_seq=tiles_per_seq),
        out_shape=(jax.ShapeDtypeStruct((n, D_MODEL), F32),
                   jax.ShapeDtypeStruct((nb * POOL_HIST_ROWS, D_MODEL), F32)),
        grid=(n // tm,),
        in_specs=[row, vec, wspec_in, wspec_in, wspec_out, vec] + _pool_specs(o),
        out_specs=(row, pl.BlockSpec((POOL_HIST_ROWS, D_MODEL), lambda i: (i // tiles_per_seq, 0))),
        scratch_shapes=[pltpu.VMEM((tm, D_FF), BF16), pltpu.VMEM((POOL_HIST_ROWS, D_MODEL), F32)],
        compiler_params=pltpu.CompilerParams(
            dimension_semantics=("arbitrary",), vmem_limit_bytes=_vmem_limit(est)),
        name="ffn_pool",
    )(x, g1.reshape(1, D_MODEL), wg, wu, wd, g2.reshape(1, D_MODEL), pool_w,
      scale.reshape(scale.shape[0], 1, D_MODEL))


def _trunk(x3, caches, first_pos, w, bias):
    nb, t_len, _ = x3.shape
    prompt = caches is None
    x = x3.reshape(nb * t_len, D_MODEL)
    n = x.shape[0]
    tm = min(ROW_TILE, n)
    tf = min(FFN_ROW_TILE, n)
    tq = min(ROW_TILE, t_len)
    cl = min(CHUNK, t_len)
    new_k, new_v, new_conv, new_pool = [], [], [], []
    for layer in range(DEPTH):
        last = layer == DEPTH - 1
        fuse_pool = prompt and layer % 2 == 1
        assert not fuse_pool or t_len % tf == 0
        if not fuse_pool:
            x = _ffn(x, w["norm_g"][layer, 0], w["wg"], w["wu"], w["wd"], layer, 0,
                     w["final_norm_g"], False, tf)
        if layer % 2 == 0:
            e = layer // 2
            q, k, v, bg, u = _inproj(x, w["norm_g"][layer, 1], w["w_in"], e, tm)
            if prompt:
                k_hist, v_hist, u_hist = k, v, u
            else:
                ck, cv, cc, _ = caches
                k_hist = ck[e].reshape(nb * WINDOW, KV_WIDTH)
                v_hist = cv[e].reshape(nb * WINDOW, KV_WIDTH)
                u_hist = jnp.pad(cc[e], ((0, 0), (CONV_HIST_ROWS - (CONV_K - 1), 0), (0, 0))
                                 ).reshape(nb * CONV_HIST_ROWS, CONV_WIDTH)
            x = _mixer_ab(x, q, k, v, k_hist, v_hist, bg, u, u_hist, bias, w["attn_sinks"][e],
                          w["conv_w"], w["w_out"], e,
                          nb=nb, t_len=t_len, tq=tq, cl=cl, zero_history=prompt)
            keep = min(WINDOW, t_len)
            new_k.append(k.reshape(nb, t_len, N_KV_HEADS, HEAD_DIM)[:, t_len - keep:])
            new_v.append(v.reshape(nb, t_len, N_KV_HEADS, HEAD_DIM)[:, t_len - keep:])
            new_conv.append(u.reshape(nb, t_len, CONV_WIDTH)[:, t_len - (CONV_K - 1):])
        else:
            o = layer // 2
            if fuse_pool:
                x, st = _ffn_pool(x, w["norm_g"][layer, 0], w["wg"], w["wu"], w["wd"], layer, 0,
                                  w["norm_g"][layer, 1], w["pool_w"], w["pool_scale"], o,
                                  nb=nb, t_len=t_len, tm=tf, sub=tf // 2)
            else:
                hist = jnp.pad(caches[3][o], ((0, 0), (POOL_HIST_ROWS - (POOL_MAX - 1), 0), (0, 0))
                               ).reshape(nb * POOL_HIST_ROWS, D_MODEL)
                x, st = _mixer_c(x, hist, w["norm_g"][layer, 1], w["pool_w"], w["pool_scale"], o,
                                 nb=nb, t_len=t_len, first_pos=first_pos)
            new_pool.append(st.reshape(nb, POOL_HIST_ROWS, D_MODEL)[:, POOL_HIST_ROWS - (POOL_MAX - 1):])
        x = _ffn(x, w["norm_g"][layer, 2], w["wg"], w["wu"], w["wd"], layer, 1,
                 w["final_norm_g"], last, tf)
    return (x.reshape(nb, t_len, D_MODEL), jnp.stack(new_k), jnp.stack(new_v),
            jnp.stack(new_conv), jnp.stack(new_pool))


def kernel(x_prompt, x_sample, cache_attn_k, cache_attn_v, cache_conv, cache_pool, norm_g, ffn_w_gate, ffn_w_up, ffn_w_down, mix_w_in, mix_w_out, conv_w, attn_sinks, rel_bias_table, pool_w, pool_scale, final_norm_g):
    past_len = 1024
    w = dict(norm_g=norm_g, final_norm_g=final_norm_g,
             wg=ffn_w_gate.astype(BF16), wu=ffn_w_up.astype(BF16), wd=ffn_w_down.astype(BF16),
             w_in=mix_w_in.astype(BF16), w_out=mix_w_out.astype(BF16), conv_w=conv_w,
             attn_sinks=attn_sinks, pool_w=pool_w.astype(BF16), pool_scale=pool_scale)
    bias = _relbias(rel_bias_table)
    y_p, k_p, v_p, conv_p, pool_p = _trunk(x_prompt, None, 0, w, bias)
    y_s, k_s, v_s, conv_s, pool_s = _trunk(
        x_sample, (cache_attn_k, cache_attn_v, cache_conv, cache_pool), past_len, w, bias)
    return (y_p, y_s, k_p, v_p, conv_p, pool_p, k_s, v_s, conv_s, pool_s)
```

```python
import functools
import math

import jax
import jax.numpy as jnp
from jax import lax
from jax.experimental import pallas as pl
from jax.experimental.pallas import tpu as pltpu

D_MODEL = 1024
DEPTH = 2
CHUNK = 64
N_HEADS = 8
N_KV_HEADS = 2
Q_PER_KV = N_HEADS // N_KV_HEADS
HEAD_DIM = 64
ATTN_WIDTH = N_HEADS * HEAD_DIM
KV_WIDTH = N_KV_HEADS * HEAD_DIM
WINDOW = 128
N_BUCKETS = 32
MAX_DISTANCE = 128
CONV_WIDTH = D_MODEL // 2
CONV_K = 3
IN_WIDTH = ATTN_WIDTH + 2 * KV_WIDTH + 3 * CONV_WIDTH
MIX_WIDTH = ATTN_WIDTH + CONV_WIDTH
POOL_SIZES = (2, 4, 8, 16)
POOL_GROUP = D_MODEL // len(POOL_SIZES)
POOL_MAX = max(POOL_SIZES)
D_FF = 2816
EPS = 1e-6
NEG = -1e30

V7X_VMEM_BYTES = 64 * 1024 * 1024
SUBLANES_F32 = 8
CONV_HIST_ROWS = SUBLANES_F32
POOL_HIST_ROWS = 16
ROW_TILE = 512
MIXER_ROW_TILE = 1024
FFN_ROW_TILE = 1024
FF_CHUNK = 512
FF_CHUNKS = tuple((c0, min(FF_CHUNK, D_FF - c0)) for c0 in range(0, D_FF, FF_CHUNK))

F32 = jnp.float32
BF16 = jnp.bfloat16


def _vmem_limit(nbytes):
    return int(min(V7X_VMEM_BYTES - (4 << 20), max(32 << 20, nbytes)))


def _rms(x, g):
    return x * lax.rsqrt(jnp.mean(x * x, axis=-1, keepdims=True) + EPS) * g


def _relbias_body(tab_ref, bkt_ref, o_ref):
    bkt = bkt_ref[...]
    for h in range(N_HEADS):
        acc = jnp.zeros(bkt.shape, F32)
        for b in range(N_BUCKETS):
            acc = jnp.where(bkt == b, tab_ref[b, h], acc)
        o_ref[h] = acc


def _rel_bucket(rel):
    half = N_BUCKETS // 2
    ret = jnp.where(rel > 0, half, 0)
    n = jnp.abs(rel)
    max_exact = half // 2
    nf = jnp.maximum(n, 1).astype(F32)
    large = max_exact + (jnp.log(nf / max_exact) / math.log(MAX_DISTANCE / max_exact)
                         * (half - max_exact)).astype(jnp.int32)
    large = jnp.minimum(large, half - 1)
    return ret + jnp.where(n < max_exact, n, large)


def _relbias(table):
    rel = (jnp.arange(WINDOW + CHUNK, dtype=jnp.int32) - WINDOW)[None, :] \
        - jnp.arange(CHUNK, dtype=jnp.int32)[:, None]
    bucket = _rel_bucket(rel).astype(jnp.int32)
    return pl.pallas_call(
        _relbias_body,
        out_shape=jax.ShapeDtypeStruct((N_HEADS, CHUNK, WINDOW + CHUNK), F32),
        in_specs=[pl.BlockSpec(memory_space=pltpu.SMEM),
                  pl.BlockSpec(memory_space=pltpu.VMEM)],
        out_specs=pl.BlockSpec(memory_space=pltpu.VMEM),
        name="relbias",
    )(table, bucket)


def _swiglu_rows(x, g, wg_ref, wu_ref, wd_ref, a_ref, r0):
    rows = x.shape[0]
    xn = _rms(x, g).astype(BF16)
    for c0, cw in FF_CHUNKS:
        gate = jnp.dot(xn, wg_ref[:, c0:c0 + cw], preferred_element_type=F32)
        up = jnp.dot(xn, wu_ref[:, c0:c0 + cw], preferred_element_type=F32)
        a_ref[r0:r0 + rows, c0:c0 + cw] = (jax.nn.silu(gate) * up).astype(BF16)
    y = jnp.dot(a_ref[r0:r0 + rows, :], wd_ref[...], preferred_element_type=F32)
    return x + 0.5 * y


def _ffn_body(x_ref, g_ref, wg_ref, wu_ref, wd_ref, fg_ref, o_ref, a_ref, *, final_norm):
    out = _swiglu_rows(x_ref[...], g_ref[...], wg_ref, wu_ref, wd_ref, a_ref, 0)
    if final_norm:
        out = _rms(out, fg_ref[...])
    o_ref[...] = out


def _ffn(x, g, wg, wu, wd, layer, j, fg, final_norm, tm):
    n = x.shape[0]
    wspec_in = pl.BlockSpec((None, None, D_MODEL, D_FF), lambda i: (layer, j, 0, 0),
                            pipeline_mode=pl.Buffered(1))
    wspec_out = pl.BlockSpec((None, None, D_FF, D_MODEL), lambda i: (layer, j, 0, 0),
                             pipeline_mode=pl.Buffered(1))
    vec = pl.BlockSpec((1, D_MODEL), lambda i: (0, 0))
    est = 3 * D_MODEL * D_FF * 2 + 4 * tm * D_MODEL * 4 + tm * D_FF * 2 + 5 * tm * 1024 * 4
    return pl.pallas_call(
        functools.partial(_ffn_body, final_norm=final_norm),
        out_shape=jax.ShapeDtypeStruct((n, D_MODEL), F32),
        grid=(n // tm,),
        in_specs=[pl.BlockSpec((tm, D_MODEL), lambda i: (i, 0)), vec,
                  wspec_in, wspec_in, wspec_out, vec],
        out_specs=pl.BlockSpec((tm, D_MODEL), lambda i: (i, 0)),
        scratch_shapes=[pltpu.VMEM((tm, D_FF), BF16)],
        compiler_params=pltpu.CompilerParams(
            dimension_semantics=("arbitrary",), vmem_limit_bytes=_vmem_limit(est)),
        name="ffn",
    )(x, g.reshape(1, D_MODEL), wg, wu, wd, fg.reshape(1, D_MODEL))


def _inproj_body(x_ref, g_ref, w_ref, q_ref, k_ref, v_ref, bg_ref, u_ref):
    h = _rms(x_ref[...], g_ref[...]).astype(BF16)
    p = jnp.dot(h, w_ref[...], preferred_element_type=F32)
    o = 0
    q_ref[...] = (p[:, o:o + ATTN_WIDTH] * (HEAD_DIM ** -0.5)).astype(BF16)
    o += ATTN_WIDTH
    k_ref[...] = p[:, o:o + KV_WIDTH]
    o += KV_WIDTH
    v_ref[...] = p[:, o:o + KV_WIDTH]
    o += KV_WIDTH
    bg_ref[...] = p[:, o:o + CONV_WIDTH]
    o += CONV_WIDTH
    u_ref[...] = p[:, o:o + CONV_WIDTH] * p[:, o + CONV_WIDTH:o + 2 * CONV_WIDTH]


def _inproj(x, g, w_in, e, tm):
    n = x.shape[0]
    row = lambda w: pl.BlockSpec((tm, w), lambda i: (i, 0))
    est = 2 * D_MODEL * IN_WIDTH * 2 + 2 * tm * (D_MODEL + IN_WIDTH) * 4 + 2 * tm * IN_WIDTH * 4
    return pl.pallas_call(
        _inproj_body,
        out_shape=(jax.ShapeDtypeStruct((n, ATTN_WIDTH), BF16),
                   jax.ShapeDtypeStruct((n, KV_WIDTH), F32),
                   jax.ShapeDtypeStruct((n, KV_WIDTH), F32),
                   jax.ShapeDtypeStruct((n, CONV_WIDTH), F32),
                   jax.ShapeDtypeStruct((n, CONV_WIDTH), F32)),
        grid=(n // tm,),
        in_specs=[row(D_MODEL), pl.BlockSpec((1, D_MODEL), lambda i: (0, 0)),
                  pl.BlockSpec((None, D_MODEL, IN_WIDTH), lambda i: (e, 0, 0))],
        out_specs=(row(ATTN_WIDTH), row(KV_WIDTH), row(KV_WIDTH), row(CONV_WIDTH), row(CONV_WIDTH)),
        compiler_params=pltpu.CompilerParams(
            dimension_semantics=("arbitrary",), vmem_limit_bytes=_vmem_limit(est)),
        name="inproj",
    )(x, g.reshape(1, D_MODEL), w_in)


PAIR_WIDTH = 2 * HEAD_DIM
PAIRS_PER_KV = Q_PER_KV // 2
ATTN_SCALE = HEAD_DIM ** -0.5


def _mixer_ab_body(x_ref, q_ref, k_ref, v_ref, kh_ref, vh_ref, bg_ref, u_ref, uh_ref,
                   bias_ref, sink_ref, cw_ref, wo_ref, o_ref,
                   kbuf, vbuf, ubuf, mix, *, tq, cl):
    kw = WINDOW + cl
    low = lax.broadcasted_iota(jnp.int32, (1, KV_WIDTH), 1) < HEAD_DIM

    def stage(buf, src_ref, r0):
        a = src_ref[...]
        b = pltpu.roll(a, HEAD_DIM, axis=1)
        rows = a.shape[0]
        buf[0, 0, r0:r0 + rows, :] = jnp.where(low, a, 0.0).astype(BF16)
        buf[0, 1, r0:r0 + rows, :] = jnp.where(low, 0.0, b).astype(BF16)
        buf[1, 0, r0:r0 + rows, :] = jnp.where(low, b, 0.0).astype(BF16)
        buf[1, 1, r0:r0 + rows, :] = jnp.where(low, 0.0, a).astype(BF16)

    stage(kbuf, kh_ref, 0)
    stage(kbuf, k_ref, WINDOW)
    stage(vbuf, vh_ref, 0)
    stage(vbuf, v_ref, WINDOW)

    for c in range(tq // cl):
        r0 = c * cl
        for hh in range(N_KV_HEADS):
            ql = jnp.concatenate(
                [q_ref[r0:r0 + cl, (hh * PAIRS_PER_KV + pr) * PAIR_WIDTH:
                       (hh * PAIRS_PER_KV + pr + 1) * PAIR_WIDTH] for pr in range(PAIRS_PER_KV)],
                axis=0)
            acc = None
            for lh in range(2):
                kk = kbuf[hh, lh, r0:r0 + kw, :]
                s = lax.dot_general(ql, kk, (((1,), (1,)), ((), ())),
                                    preferred_element_type=F32) + bias_ref[hh, lh]
                sk = sink_ref[hh, lh]
                m = jnp.maximum(jnp.max(s, axis=-1, keepdims=True), sk)
                p = jnp.exp(s - m)
                den = jnp.sum(p, axis=-1, keepdims=True) + jnp.exp(sk - m)
                vv = vbuf[hh, lh, r0:r0 + kw, :]
                o = jnp.dot(p.astype(BF16), vv, preferred_element_type=F32) * (1.0 / den)
                acc = o if acc is None else acc + o
            for pr in range(PAIRS_PER_KV):
                c0 = (hh * PAIRS_PER_KV + pr) * PAIR_WIDTH
                mix[r0:r0 + cl, c0:c0 + PAIR_WIDTH] = acc[pr * cl:(pr + 1) * cl].astype(BF16)

    o_ref[...] = _conv_and_project(x_ref[...], bg_ref[...], u_ref[...], uh_ref[...],
                                   cw_ref, wo_ref, ubuf, mix)


def _conv_and_project(x, bg, u, uh, cw_ref, wo_ref, ubuf, mix):
    rows = u.shape[0]
    ubuf[0:CONV_HIST_ROWS, :] = uh
    ubuf[CONV_HIST_ROWS:CONV_HIST_ROWS + rows, :] = u
    conv = None
    for j in range(CONV_K):
        off = CONV_HIST_ROWS - (CONV_K - 1) + j
        term = cw_ref[j:j + 1, :] * ubuf[off:off + rows, :]
        conv = term if conv is None else conv + term
    mix[:, ATTN_WIDTH:MIX_WIDTH] = (bg * conv).astype(BF16)
    return x + jnp.dot(mix[...], wo_ref[...], preferred_element_type=F32)


def _mixer_full_body(x_ref, g_ref, win_ref, biast_ref, sinkt_ref, cw_ref, wo_ref,
                     o_ref, klast_ref, vlast_ref, ulast_ref,
                     qbuf, kdup, vta, vtb, ubuf, mix, sbuf, pbuf, rbuf, kcar, vcar, ucar, *, tq):
    b = pl.program_id(0)
    t = pl.program_id(1)
    cl = CHUNK
    kw = WINDOW + cl
    w = WINDOW + tq
    low = lax.broadcasted_iota(jnp.int32, (1, KV_WIDTH), 1) < HEAD_DIM

    @pl.when((b == 0) & (t == 0))
    def _():
        kcar[...] = jnp.zeros_like(kcar)
        vcar[...] = jnp.zeros_like(vcar)
        ucar[...] = jnp.zeros_like(ucar)

    x = x_ref[...]
    h = _rms(x, g_ref[...]).astype(BF16)

    def proj(c0, width):
        return jnp.dot(h, win_ref[:, c0:c0 + width], preferred_element_type=F32)

    qbuf[...] = (proj(0, ATTN_WIDTH) * ATTN_SCALE).astype(BF16)
    kv = proj(ATTN_WIDTH, 2 * KV_WIDTH)
    k = kv[:, 0:KV_WIDTH]
    v = kv[:, KV_WIDTH:2 * KV_WIDTH]
    c0 = ATTN_WIDTH + 2 * KV_WIDTH
    bg = proj(c0, CONV_WIDTH)
    u = proj(c0 + CONV_WIDTH, CONV_WIDTH) * proj(c0 + 2 * CONV_WIDTH, CONV_WIDTH)

    kh = kcar[...]
    vh = vcar[...]
    uh = jnp.where(t == 0, 0.0, ucar[...])
    k_tail = k[tq - WINDOW:tq, :]
    v_tail = v[tq - WINDOW:tq, :]
    u_tail = u[tq - CONV_HIST_ROWS:tq, :]
    kcar[...] = k_tail
    vcar[...] = v_tail
    ucar[...] = u_tail
    klast_ref[...] = k_tail
    vlast_ref[...] = v_tail
    ulast_ref[...] = u_tail

    kf = jnp.concatenate([kh, k], axis=0)
    kr = pltpu.roll(kf, HEAD_DIM, axis=1)
    kdup[0] = jnp.where(low, kf, kr).astype(BF16)
    kdup[1] = jnp.where(low, kr, kf).astype(BF16)

    vf = jnp.concatenate([vh, v, jnp.zeros((cl, KV_WIDTH), F32)], axis=0)
    vr = pltpu.roll(vf, HEAD_DIM, axis=1)
    for hh, vd in enumerate((jnp.where(low, vf, vr), jnp.where(low, vr, vf))):
        for blk in range(w // KV_WIDTH):
            b0 = blk * KV_WIDTH
            vta[hh, :, b0:b0 + KV_WIDTH] = vd[b0:b0 + KV_WIDTH].T.astype(BF16)
            vtb[hh, :, b0:b0 + KV_WIDTH] = vd[cl + b0:cl + b0 + KV_WIDTH].T.astype(BF16)

    n_cols = PAIRS_PER_KV * PAIR_WIDTH
    key = lax.broadcasted_iota(jnp.int32, (kw, n_cols), 0)
    rr = lax.broadcasted_iota(jnp.int32, (PAIR_WIDTH, PAIR_WIDTH), 0) < HEAD_DIM
    cc = lax.broadcasted_iota(jnp.int32, (PAIR_WIDTH, PAIR_WIDTH), 1) < HEAD_DIM
    diag = rr == cc
    for c in range(tq // cl):
        r0 = c * cl
        for hh in range(N_KV_HEADS):
            rows = []
            for pr in range(PAIRS_PER_KV):
                c0 = (hh * PAIRS_PER_KV + pr) * PAIR_WIDTH
                qp = qbuf[r0:r0 + cl, c0:c0 + PAIR_WIDTH]
                rows += [jnp.where(low, qp, 0), jnp.where(low, 0, qp)]
            qm = jnp.concatenate(rows, axis=0)
            s = lax.dot_general(kdup[hh, r0:r0 + kw, :], qm, (((1,), (1,)), ((), ())),
                                preferred_element_type=F32) + biast_ref[hh]
            if r0 < WINDOW:
                s = jnp.where(key + (t * tq + r0 - WINDOW) >= 0, s, NEG)
            j0 = (c * N_KV_HEADS + hh) * n_cols
            sbuf[:, j0:j0 + n_cols] = s

    s = sbuf[...]
    sk = sinkt_ref[...]
    m = jnp.maximum(jnp.max(s, axis=0, keepdims=True), sk)
    p = jnp.exp(s - m)
    den = jnp.sum(p, axis=0, keepdims=True) + jnp.exp(sk - m)
    pbuf[...] = p.astype(BF16)
    rbuf[...] = 1.0 / den

    for c in range(tq // cl):
        r0 = c * cl
        k0 = r0 if c % 2 == 0 else r0 - cl
        vt_ref = vta if c % 2 == 0 else vtb
        for hh in range(N_KV_HEADS):
            j0 = (c * N_KV_HEADS + hh) * n_cols
            ot = jnp.dot(vt_ref[hh, :, k0:k0 + kw], pbuf[:, j0:j0 + n_cols],
                         preferred_element_type=F32) * rbuf[:, j0:j0 + n_cols]
            for pr in range(PAIRS_PER_KV):
                blk = jnp.where(diag, ot[:, pr * PAIR_WIDTH:(pr + 1) * PAIR_WIDTH], 0.0).T
                c0 = (hh * PAIRS_PER_KV + pr) * PAIR_WIDTH
                mix[r0:r0 + cl, c0:c0 + PAIR_WIDTH] = (blk[0:cl] + blk[cl:2 * cl]).astype(BF16)

    o_ref[...] = _conv_and_project(x, bg, u, uh, cw_ref, wo_ref, ubuf, mix)


def _head_layout(bias, sinks, cl):
    kw = WINDOW + cl
    b5 = bias[:, :cl, :kw].reshape(N_KV_HEADS, PAIRS_PER_KV, 2, cl, kw)
    s5 = jnp.broadcast_to(sinks.reshape(N_KV_HEADS, PAIRS_PER_KV, 2, 1, 1),
                          (N_KV_HEADS, PAIRS_PER_KV, 2, cl, 1))
    return b5, s5


def _mixer_full(x, g, w_in, bias, sinks, conv_w, w_out, e, *, nb, t_len, tq):
    n = x.shape[0]
    nt = t_len // tq
    cl = CHUNK
    kw = WINDOW + cl
    assert tq % KV_WIDTH == 0 and tq >= WINDOW
    b5, s5 = _head_layout(bias, sinks, cl)
    n_cols = PAIRS_PER_KV * PAIR_WIDTH
    all_cols = (tq // cl) * N_KV_HEADS * n_cols
    bias_l = jnp.transpose(b5, (0, 4, 1, 2, 3)).reshape(N_KV_HEADS, kw, n_cols)
    sink_l = jnp.tile(s5.reshape(1, N_KV_HEADS * n_cols), (1, tq // cl))
    row = pl.BlockSpec((tq, D_MODEL), lambda b, t: (b * nt + t, 0))
    const = lambda a: pl.BlockSpec(a.shape, lambda b, t: (0,) * a.ndim)
    per_seq = lambda rows, width: pl.BlockSpec((rows, width), lambda b, t: (b, 0))
    est = 4 * tq * D_MODEL * 4 + (D_MODEL * IN_WIDTH + MIX_WIDTH * D_MODEL) * 2 \
        + kw * all_cols * 6 + 16 * tq * D_MODEL * 4
    return pl.pallas_call(
        functools.partial(_mixer_full_body, tq=tq),
        out_shape=(jax.ShapeDtypeStruct((n, D_MODEL), F32),
                   jax.ShapeDtypeStruct((nb * WINDOW, KV_WIDTH), F32),
                   jax.ShapeDtypeStruct((nb * WINDOW, KV_WIDTH), F32),
                   jax.ShapeDtypeStruct((nb * CONV_HIST_ROWS, CONV_WIDTH), F32)),
        grid=(nb, nt),
        in_specs=[row, pl.BlockSpec((1, D_MODEL), lambda b, t: (0, 0)),
                  pl.BlockSpec((None, D_MODEL, IN_WIDTH), lambda b, t: (e, 0, 0),
                               pipeline_mode=pl.Buffered(1)),
                  const(bias_l), const(sink_l),
                  pl.BlockSpec((None, CONV_K, CONV_WIDTH), lambda b, t: (e, 0, 0)),
                  pl.BlockSpec((None, MIX_WIDTH, D_MODEL), lambda b, t: (e, 0, 0),
                               pipeline_mode=pl.Buffered(1))],
        out_specs=(row, per_seq(WINDOW, KV_WIDTH), per_seq(WINDOW, KV_WIDTH),
                   per_seq(CONV_HIST_ROWS, CONV_WIDTH)),
        scratch_shapes=[pltpu.VMEM((tq, ATTN_WIDTH), BF16),
                        pltpu.VMEM((N_KV_HEADS, WINDOW + tq, KV_WIDTH), BF16),
                        pltpu.VMEM((N_KV_HEADS, PAIR_WIDTH, WINDOW + tq), BF16),
                        pltpu.VMEM((N_KV_HEADS, PAIR_WIDTH, WINDOW + tq), BF16),
                        pltpu.VMEM((CONV_HIST_ROWS + tq, CONV_WIDTH), F32),
                        pltpu.VMEM((tq, MIX_WIDTH), BF16),
                        pltpu.VMEM((kw, all_cols), F32), pltpu.VMEM((kw, all_cols), BF16),
                        pltpu.VMEM((1, all_cols), F32),
                        pltpu.VMEM((WINDOW, KV_WIDTH), F32), pltpu.VMEM((WINDOW, KV_WIDTH), F32),
                        pltpu.VMEM((CONV_HIST_ROWS, CONV_WIDTH), F32)],
        compiler_params=pltpu.CompilerParams(
            dimension_semantics=("arbitrary", "arbitrary"), vmem_limit_bytes=_vmem_limit(est)),
        name="mixer_full",
    )(x, g.reshape(1, D_MODEL), w_in, bias_l, sink_l, conv_w, w_out)


def _mixer_ab(x, q, k, v, k_hist, v_hist, bg, u, u_hist, bias, sinks, conv_w, w_out, e,
              *, nb, t_len):
    n = x.shape[0]
    tq = cl = t_len
    assert t_len <= CHUNK
    row = lambda w: pl.BlockSpec((tq, w), lambda b: (b, 0))
    kh_spec = pl.BlockSpec((WINDOW, KV_WIDTH), lambda b: (b, 0))
    uh_spec = pl.BlockSpec((CONV_HIST_ROWS, CONV_WIDTH), lambda b: (b, 0))
    kw = WINDOW + cl
    b5, s5 = _head_layout(bias, sinks, cl)
    bias_l = jnp.transpose(b5, (0, 2, 1, 3, 4)).reshape(N_KV_HEADS, 2, PAIRS_PER_KV * cl, kw)
    sink_l = jnp.transpose(s5, (0, 2, 1, 3, 4)).reshape(N_KV_HEADS, 2, PAIRS_PER_KV * cl, 1)
    const = lambda a: pl.BlockSpec(a.shape, lambda b: (0,) * a.ndim)
    return pl.pallas_call(
        functools.partial(_mixer_ab_body, tq=tq, cl=cl),
        out_shape=jax.ShapeDtypeStruct((n, D_MODEL), F32),
        grid=(nb,),
        in_specs=[row(D_MODEL), row(ATTN_WIDTH), row(KV_WIDTH), row(KV_WIDTH),
                  kh_spec, kh_spec, row(CONV_WIDTH), row(CONV_WIDTH), uh_spec,
                  const(bias_l), const(sink_l),
                  pl.BlockSpec((None, CONV_K, CONV_WIDTH), lambda b: (e, 0, 0)),
                  pl.BlockSpec((None, MIX_WIDTH, D_MODEL), lambda b: (e, 0, 0))],
        out_specs=row(D_MODEL),
        scratch_shapes=[pltpu.VMEM((N_KV_HEADS, 2, WINDOW + tq, KV_WIDTH), BF16),
                        pltpu.VMEM((N_KV_HEADS, 2, WINDOW + tq, KV_WIDTH), BF16),
                        pltpu.VMEM((CONV_HIST_ROWS + tq, CONV_WIDTH), F32),
                        pltpu.VMEM((tq, MIX_WIDTH), BF16)],
        compiler_params=pltpu.CompilerParams(dimension_semantics=("arbitrary",)),
        name="mixer_ab",
    )(x, q, k, v, k_hist, v_hist, bg, u, u_hist, bias_l, sink_l, conv_w, w_out)


def _pool_rows(x, hist, g, pw_ref, sc_ref, o_ref, r0, pos0):
    rows = x.shape[0]
    h = _rms(x, g)
    sums = []
    cur = jnp.concatenate([hist, h], axis=0)
    for gi, w in enumerate(POOL_SIZES):
        assert w == 2 ** (gi + 1)
        cur = cur + pltpu.roll(cur, w // 2, axis=0)
        sums.append(cur[POOL_HIST_ROWS:POOL_HIST_ROWS + rows, 0:POOL_GROUP])
        if gi + 1 < len(POOL_SIZES):
            cur = cur[:, POOL_GROUP:]

    pos = pos0 + lax.broadcasted_iota(jnp.int32, (rows, 1), 0)
    for gi, w in enumerate(POOL_SIZES):
        sl = slice(gi * POOL_GROUP, (gi + 1) * POOL_GROUP)
        cnt = jnp.minimum(pos + 1, w).astype(F32)
        d = (sums[gi] / cnt - h[:, sl]).astype(BF16)
        y = jnp.dot(d, pw_ref[gi], preferred_element_type=F32)
        o_ref[r0:r0 + rows, sl] = x[:, sl] + y * sc_ref[:, sl]
    return h[rows - POOL_HIST_ROWS:rows, :]


def _mixer_c_body(x_ref, hist_ref, g_ref, pw_ref, sc_ref, o_ref, st_ref, *, first_pos):
    st_ref[...] = _pool_rows(x_ref[...], hist_ref[...], g_ref[...], pw_ref, sc_ref, o_ref, 0, first_pos)


def _pool_specs(o):
    return [pl.BlockSpec((None, len(POOL_SIZES), POOL_GROUP, POOL_GROUP), lambda *a: (o, 0, 0, 0)),
            pl.BlockSpec((None, 1, D_MODEL), lambda *a: (o, 0, 0))]


def _mixer_c(x, hist, g, pool_w, scale, o, *, nb, t_len, first_pos):
    n = x.shape[0]
    row = pl.BlockSpec((t_len, D_MODEL), lambda b: (b, 0))
    hist_spec = pl.BlockSpec((POOL_HIST_ROWS, D_MODEL), lambda b: (b, 0))
    return pl.pallas_call(
        functools.partial(_mixer_c_body, first_pos=first_pos),
        out_shape=(jax.ShapeDtypeStruct((n, D_MODEL), F32),
                   jax.ShapeDtypeStruct((nb * POOL_HIST_ROWS, D_MODEL), F32)),
        grid=(nb,),
        in_specs=[row, hist_spec, pl.BlockSpec((1, D_MODEL), lambda b: (0, 0))] + _pool_specs(o),
        out_specs=(row, hist_spec),
        compiler_params=pltpu.CompilerParams(dimension_semantics=("arbitrary",)),
        name="mixer_c",
    )(x, hist, g.reshape(1, D_MODEL), pool_w, scale.reshape(scale.shape[0], 1, D_MODEL))


def _ffn_pool_body(x_ref, g1_ref, wg_ref, wu_ref, wd_ref, g2_ref, pw_ref, sc_ref,
                   o_ref, st_ref, a_ref, carry_ref, *, tm, sub, tiles_per_seq):
    i = pl.program_id(0)
    t = i % tiles_per_seq

    @pl.when(i == 0)
    def _():
        carry_ref[...] = jnp.zeros_like(carry_ref)

    hist = jnp.where(t == 0, 0.0, carry_ref[...])
    g1 = g1_ref[...]
    g2 = g2_ref[...]
    blocks = [_swiglu_rows(x_ref[r0:r0 + sub, :], g1, wg_ref, wu_ref, wd_ref, a_ref, r0)
              for r0 in range(0, tm, sub)]
    for s, xb in enumerate(blocks):
        hist = _pool_rows(xb, hist, g2, pw_ref, sc_ref, o_ref, s * sub, t * tm + s * sub)
    carry_ref[...] = hist
    st_ref[...] = hist


def _ffn_pool(x, g1, wg, wu, wd, layer, j, g2, pool_w, scale, o, *, nb, t_len, tm, sub):
    n = x.shape[0]
    tiles_per_seq = t_len // tm
    wspec_in = pl.BlockSpec((None, None, D_MODEL, D_FF), lambda i: (layer, j, 0, 0),
                            pipeline_mode=pl.Buffered(1))
    wspec_out = pl.BlockSpec((None, None, D_FF, D_MODEL), lambda i: (layer, j, 0, 0),
                             pipeline_mode=pl.Buffered(1))
    vec = pl.BlockSpec((1, D_MODEL), lambda i: (0, 0))
    row = pl.BlockSpec((tm, D_MODEL), lambda i: (i, 0))
    est = 3 * D_MODEL * D_FF * 2 + 4 * tm * D_MODEL * 4 + tm * D_FF * 2 + 6 * tm * 1024 * 4
    return pl.pallas_call(
        functools.partial(_ffn_pool_body, tm=tm, sub=sub, tiles_per_seq=tiles_per_seq),
        out_shape=(jax.ShapeDtypeStruct((n, D_MODEL), F32),
                   jax.ShapeDtypeStruct((nb * POOL_HIST_ROWS, D_MODEL), F32)),
        grid=(n // tm,),
        in_specs=[row, vec, wspec_in, wspec_in, wspec_out, vec] + _pool_specs(o),
        out_specs=(row, pl.BlockSpec((POOL_HIST_ROWS, D_MODEL), lambda i: (i // tiles_per_seq, 0))),
        scratch_shapes=[pltpu.VMEM((tm, D_FF), BF16), pltpu.VMEM((POOL_HIST_ROWS, D_MODEL), F32)],
        compiler_params=pltpu.CompilerParams(
            dimension_semantics=("arbitrary",), vmem_limit_bytes=_vmem_limit(est)),
        name="ffn_pool",
    )(x, g1.reshape(1, D_MODEL), wg, wu, wd, g2.reshape(1, D_MODEL), pool_w,
      scale.reshape(scale.shape[0], 1, D_MODEL))


def _trunk(x3, caches, first_pos, w, bias):
    nb, t_len, _ = x3.shape
    prompt = caches is None
    x = x3.reshape(nb * t_len, D_MODEL)
    n = x.shape[0]
    tm = min(ROW_TILE, n)
    tf = min(FFN_ROW_TILE, n)
    tq = min(MIXER_ROW_TILE, t_len)
    new_k, new_v, new_conv, new_pool = [], [], [], []
    for layer in range(DEPTH):
        last = layer == DEPTH - 1
        fuse_pool = prompt and layer % 2 == 1
        assert not fuse_pool or t_len % tf == 0
        if not fuse_pool:
            x = _ffn(x, w["norm_g"][layer, 0], w["wg"], w["wu"], w["wd"], layer, 0,
                     w["final_norm_g"], False, tf)
        if layer % 2 == 0:
            e = layer // 2
            if prompt:
                x, k, v, u = _mixer_full(x, w["norm_g"][layer, 1], w["w_in"], bias,
                                         w["attn_sinks"][e], w["conv_w"], w["w_out"], e,
                                         nb=nb, t_len=t_len, tq=tq)
                rows_k, rows_u = WINDOW, CONV_HIST_ROWS
            else:
                q, k, v, bg, u = _inproj(x, w["norm_g"][layer, 1], w["w_in"], e, tm)
                ck, cv, cc, _ = caches
                k_hist = ck[e].reshape(nb * WINDOW, KV_WIDTH)
                v_hist = cv[e].reshape(nb * WINDOW, KV_WIDTH)
                u_hist = jnp.pad(cc[e], ((0, 0), (CONV_HIST_ROWS - (CONV_K - 1), 0), (0, 0))
                                 ).reshape(nb * CONV_HIST_ROWS, CONV_WIDTH)
                x = _mixer_ab(x, q, k, v, k_hist, v_hist, bg, u, u_hist, bias, w["attn_sinks"][e],
                              w["conv_w"], w["w_out"], e, nb=nb, t_len=t_len)
                rows_k = rows_u = t_len
            keep = min(WINDOW, t_len)
            new_k.append(k.reshape(nb, rows_k, N_KV_HEADS, HEAD_DIM)[:, rows_k - keep:])
            new_v.append(v.reshape(nb, rows_k, N_KV_HEADS, HEAD_DIM)[:, rows_k - keep:])
            new_conv.append(u.reshape(nb, rows_u, CONV_WIDTH)[:, rows_u - (CONV_K - 1):])
        else:
            o = layer // 2
            if fuse_pool:
                x, st = _ffn_pool(x, w["norm_g"][layer, 0], w["wg"], w["wu"], w["wd"], layer, 0,
                                  w["norm_g"][layer, 1], w["pool_w"], w["pool_scale"], o,
                                  nb=nb, t_len=t_len, tm=tf, sub=tf // 2)
            else:
                hist = jnp.pad(caches[3][o], ((0, 0), (POOL_HIST_ROWS - (POOL_MAX - 1), 0), (0, 0))
                               ).reshape(nb * POOL_HIST_ROWS, D_MODEL)
                x, st = _mixer_c(x, hist, w["norm_g"][layer, 1], w["pool_w"], w["pool_scale"], o,
                                 nb=nb, t_len=t_len, first_pos=first_pos)
            new_pool.append(st.reshape(nb, POOL_HIST_ROWS, D_MODEL)[:, POOL_HIST_ROWS - (POOL_MAX - 1):])
        x = _ffn(x, w["norm_g"][layer, 2], w["wg"], w["wu"], w["wd"], layer, 1,
                 w["final_norm_g"], last, tf)
    return (x.reshape(nb, t_len, D_MODEL), jnp.stack(new_k), jnp.stack(new_v),
            jnp.stack(new_conv), jnp.stack(new_pool))


def kernel(x_prompt, x_sample, cache_attn_k, cache_attn_v, cache_conv, cache_pool, norm_g, ffn_w_gate, ffn_w_up, ffn_w_down, mix_w_in, mix_w_out, conv_w, attn_sinks, rel_bias_table, pool_w, pool_scale, final_norm_g):
    past_len = 1024
    w = dict(norm_g=norm_g, final_norm_g=final_norm_g,
             wg=ffn_w_gate.astype(BF16), wu=ffn_w_up.astype(BF16), wd=ffn_w_down.astype(BF16),
             w_in=mix_w_in.astype(BF16), w_out=mix_w_out.astype(BF16), conv_w=conv_w,
             attn_sinks=attn_sinks, pool_w=pool_w.astype(BF16), pool_scale=pool_scale)
    bias = _relbias(rel_bias_table)
    y_p, k_p, v_p, conv_p, pool_p = _trunk(x_prompt, None, 0, w, bias)
    y_s, k_s, v_s, conv_s, pool_s = _trunk(
        x_sample, (cache_attn_k, cache_attn_v, cache_conv, cache_pool), past_len, w, bias)
    return (y_p, y_s, k_p, v_p, conv_p, pool_p, k_s, v_s, conv_s, pool_s)
```

```python
import functools
import math

import jax
import jax.numpy as jnp
from jax import lax
from jax.experimental import pallas as pl
from jax.experimental.pallas import tpu as pltpu

D_MODEL = 1024
DEPTH = 2
CHUNK = 64
N_HEADS = 8
N_KV_HEADS = 2
Q_PER_KV = N_HEADS // N_KV_HEADS
HEAD_DIM = 64
ATTN_WIDTH = N_HEADS * HEAD_DIM
KV_WIDTH = N_KV_HEADS * HEAD_DIM
WINDOW = 128
N_BUCKETS = 32
MAX_DISTANCE = 128
CONV_WIDTH = D_MODEL // 2
CONV_K = 3
IN_WIDTH = ATTN_WIDTH + 2 * KV_WIDTH + 3 * CONV_WIDTH
MIX_WIDTH = ATTN_WIDTH + CONV_WIDTH
POOL_SIZES = (2, 4, 8, 16)
POOL_GROUP = D_MODEL // len(POOL_SIZES)
POOL_MAX = max(POOL_SIZES)
D_FF = 2816
EPS = 1e-6
NEG = -1e30

V7X_VMEM_BYTES = 64 * 1024 * 1024
SUBLANES_F32 = 8
CONV_HIST_ROWS = SUBLANES_F32
POOL_HIST_ROWS = 16
ROW_TILE = 512
MIXER_ROW_TILE = 1024
FFN_ROW_TILE = 1024
FF_CHUNK = 512
FF_CHUNKS = tuple((c0, min(FF_CHUNK, D_FF - c0)) for c0 in range(0, D_FF, FF_CHUNK))

F32 = jnp.float32
BF16 = jnp.bfloat16


def _vmem_limit(nbytes):
    return int(min(V7X_VMEM_BYTES - (4 << 20), max(32 << 20, nbytes)))


def _rms(x, g):
    return x * lax.rsqrt(jnp.mean(x * x, axis=-1, keepdims=True) + EPS) * g


def _relbias_body(tab_ref, bkt_ref, o_ref):
    bkt = bkt_ref[...]
    for h in range(N_HEADS):
        acc = jnp.zeros(bkt.shape, F32)
        for b in range(N_BUCKETS):
            acc = jnp.where(bkt == b, tab_ref[b, h], acc)
        o_ref[h] = acc


def _rel_bucket(rel):
    half = N_BUCKETS // 2
    ret = jnp.where(rel > 0, half, 0)
    n = jnp.abs(rel)
    max_exact = half // 2
    nf = jnp.maximum(n, 1).astype(F32)
    large = max_exact + (jnp.log(nf / max_exact) / math.log(MAX_DISTANCE / max_exact)
                         * (half - max_exact)).astype(jnp.int32)
    large = jnp.minimum(large, half - 1)
    return ret + jnp.where(n < max_exact, n, large)


def _relbias(table):
    rel = (jnp.arange(WINDOW + CHUNK, dtype=jnp.int32) - WINDOW)[None, :] \
        - jnp.arange(CHUNK, dtype=jnp.int32)[:, None]
    bucket = _rel_bucket(rel).astype(jnp.int32)
    return pl.pallas_call(
        _relbias_body,
        out_shape=jax.ShapeDtypeStruct((N_HEADS, CHUNK, WINDOW + CHUNK), F32),
        in_specs=[pl.BlockSpec(memory_space=pltpu.SMEM),
                  pl.BlockSpec(memory_space=pltpu.VMEM)],
        out_specs=pl.BlockSpec(memory_space=pltpu.VMEM),
        name="relbias",
    )(table, bucket)


def _swiglu_rows(x, g, wg_ref, wu_ref, wd_ref, a_ref, r0):
    rows = x.shape[0]
    xn = _rms(x, g).astype(BF16)
    for c0, cw in FF_CHUNKS:
        gate = jnp.dot(xn, wg_ref[:, c0:c0 + cw], preferred_element_type=F32)
        up = jnp.dot(xn, wu_ref[:, c0:c0 + cw], preferred_element_type=F32)
        a_ref[r0:r0 + rows, c0:c0 + cw] = (jax.nn.silu(gate) * up).astype(BF16)
    y = jnp.dot(a_ref[r0:r0 + rows, :], wd_ref[...], preferred_element_type=F32)
    return x + 0.5 * y


BF16_SUBLANES = 16


class _CastJob:
    def __init__(self, a, lead, steps):
        self.a, self.lead = a, tuple(lead)
        rows, cols = a.shape[-2:]
        self.chunk = next(c for c in range(BF16_SUBLANES, rows + 1, BF16_SUBLANES)
                          if rows % c == 0 and steps % (rows // c) == 0)
        self.repeat = steps // (rows // self.chunk)
        self.out_shape = jax.ShapeDtypeStruct((rows, cols), BF16)

    def specs(self, step_of):
        none = (None,) * len(self.lead)
        cols = self.a.shape[-1]
        return (pl.BlockSpec(none + (self.chunk, cols),
                             lambda *g: self.lead + (step_of(*g) // self.repeat, 0)),
                pl.BlockSpec((self.chunk, cols), lambda *g: (step_of(*g) // self.repeat, 0)))


def _host_casts(body, n_in, n_out, jobs):
    k = len(jobs)

    def hosted(*refs):
        ins, cast_in = refs[:n_in], refs[n_in:n_in + k]
        outs, cast_out = refs[n_in + k:n_in + k + n_out], refs[n_in + k + n_out:n_in + 2 * k + n_out]
        for src, dst in zip(cast_in, cast_out):
            dst[...] = src[...].astype(BF16)
        body(*ins, *outs, *refs[n_in + 2 * k + n_out:])
    return hosted


def _ffn_body(x_ref, g_ref, wg_ref, wu_ref, wd_ref, fg_ref, o_ref, a_ref, *, final_norm):
    out = _swiglu_rows(x_ref[...], g_ref[...], wg_ref, wu_ref, wd_ref, a_ref, 0)
    if final_norm:
        out = _rms(out, fg_ref[...])
    o_ref[...] = out


def _ffn(x, g, w16, fg, final_norm, tm, cast=()):
    n = x.shape[0]
    steps = n // tm
    jobs = [_CastJob(a, lead, steps) for a, lead in cast]
    job_specs = [job.specs(lambda i: i) for job in jobs]
    whole = lambda a: pl.BlockSpec(a.shape, lambda i: (0, 0), pipeline_mode=pl.Buffered(1))
    vec = pl.BlockSpec((1, D_MODEL), lambda i: (0, 0))
    row = pl.BlockSpec((tm, D_MODEL), lambda i: (i, 0))
    est = 3 * D_MODEL * D_FF * 2 + 4 * tm * D_MODEL * 4 + tm * D_FF * 2 + 5 * tm * 1024 * 4 \
        + sum(6 * job.chunk * job.a.shape[-1] * 2 for job in jobs)
    return pl.pallas_call(
        _host_casts(functools.partial(_ffn_body, final_norm=final_norm), 6, 1, jobs),
        out_shape=(jax.ShapeDtypeStruct((n, D_MODEL), F32),) + tuple(job.out_shape for job in jobs),
        grid=(steps,),
        in_specs=[row, vec, whole(w16[0]), whole(w16[1]), whole(w16[2]), vec] + [s[0] for s in job_specs],
        out_specs=(row,) + tuple(s[1] for s in job_specs),
        scratch_shapes=[pltpu.VMEM((tm, D_FF), BF16)],
        compiler_params=pltpu.CompilerParams(
            dimension_semantics=("arbitrary",), vmem_limit_bytes=_vmem_limit(est)),
        name="ffn",
    )(x, g.reshape(1, D_MODEL), *w16, fg.reshape(1, D_MODEL), *[job.a for job in jobs])


def _inproj_body(x_ref, g_ref, w_ref, q_ref, k_ref, v_ref, bg_ref, u_ref):
    h = _rms(x_ref[...], g_ref[...]).astype(BF16)
    p = jnp.dot(h, w_ref[...], preferred_element_type=F32)
    o = 0
    q_ref[...] = (p[:, o:o + ATTN_WIDTH] * (HEAD_DIM ** -0.5)).astype(BF16)
    o += ATTN_WIDTH
    k_ref[...] = p[:, o:o + KV_WIDTH]
    o += KV_WIDTH
    v_ref[...] = p[:, o:o + KV_WIDTH]
    o += KV_WIDTH
    bg_ref[...] = p[:, o:o + CONV_WIDTH]
    o += CONV_WIDTH
    u_ref[...] = p[:, o:o + CONV_WIDTH] * p[:, o + CONV_WIDTH:o + 2 * CONV_WIDTH]


def _inproj(x, g, w_in, tm):
    n = x.shape[0]
    row = lambda w: pl.BlockSpec((tm, w), lambda i: (i, 0))
    est = 2 * D_MODEL * IN_WIDTH * 2 + 2 * tm * (D_MODEL + IN_WIDTH) * 4 + 2 * tm * IN_WIDTH * 4
    return pl.pallas_call(
        _inproj_body,
        out_shape=(jax.ShapeDtypeStruct((n, ATTN_WIDTH), BF16),
                   jax.ShapeDtypeStruct((n, KV_WIDTH), F32),
                   jax.ShapeDtypeStruct((n, KV_WIDTH), F32),
                   jax.ShapeDtypeStruct((n, CONV_WIDTH), F32),
                   jax.ShapeDtypeStruct((n, CONV_WIDTH), F32)),
        grid=(n // tm,),
        in_specs=[row(D_MODEL), pl.BlockSpec((1, D_MODEL), lambda i: (0, 0)),
                  pl.BlockSpec((D_MODEL, IN_WIDTH), lambda i: (0, 0))],
        out_specs=(row(ATTN_WIDTH), row(KV_WIDTH), row(KV_WIDTH), row(CONV_WIDTH), row(CONV_WIDTH)),
        compiler_params=pltpu.CompilerParams(
            dimension_semantics=("arbitrary",), vmem_limit_bytes=_vmem_limit(est)),
        name="inproj",
    )(x, g.reshape(1, D_MODEL), w_in)


PAIR_WIDTH = 2 * HEAD_DIM
PAIRS_PER_KV = Q_PER_KV // 2
ATTN_SCALE = HEAD_DIM ** -0.5


def _mixer_ab_body(x_ref, q_ref, k_ref, v_ref, kh_ref, vh_ref, bg_ref, u_ref, uh_ref,
                   bias_ref, sink_ref, cw_ref, wo_ref, o_ref,
                   kbuf, vbuf, ubuf, mix, *, tq, cl):
    kw = WINDOW + cl
    low = lax.broadcasted_iota(jnp.int32, (1, KV_WIDTH), 1) < HEAD_DIM

    def stage(buf, src_ref, r0):
        a = src_ref[...]
        b = pltpu.roll(a, HEAD_DIM, axis=1)
        rows = a.shape[0]
        buf[0, 0, r0:r0 + rows, :] = jnp.where(low, a, 0.0).astype(BF16)
        buf[0, 1, r0:r0 + rows, :] = jnp.where(low, 0.0, b).astype(BF16)
        buf[1, 0, r0:r0 + rows, :] = jnp.where(low, b, 0.0).astype(BF16)
        buf[1, 1, r0:r0 + rows, :] = jnp.where(low, 0.0, a).astype(BF16)

    stage(kbuf, kh_ref, 0)
    stage(kbuf, k_ref, WINDOW)
    stage(vbuf, vh_ref, 0)
    stage(vbuf, v_ref, WINDOW)

    for c in range(tq // cl):
        r0 = c * cl
        for hh in range(N_KV_HEADS):
            ql = jnp.concatenate(
                [q_ref[r0:r0 + cl, (hh * PAIRS_PER_KV + pr) * PAIR_WIDTH:
                       (hh * PAIRS_PER_KV + pr + 1) * PAIR_WIDTH] for pr in range(PAIRS_PER_KV)],
                axis=0)
            acc = None
            for lh in range(2):
                kk = kbuf[hh, lh, r0:r0 + kw, :]
                s = lax.dot_general(ql, kk, (((1,), (1,)), ((), ())),
                                    preferred_element_type=F32) + bias_ref[hh, lh]
                sk = sink_ref[hh, lh]
                m = jnp.maximum(jnp.max(s, axis=-1, keepdims=True), sk)
                p = jnp.exp(s - m)
                den = jnp.sum(p, axis=-1, keepdims=True) + jnp.exp(sk - m)
                vv = vbuf[hh, lh, r0:r0 + kw, :]
                o = jnp.dot(p.astype(BF16), vv, preferred_element_type=F32) * (1.0 / den)
                acc = o if acc is None else acc + o
            for pr in range(PAIRS_PER_KV):
                c0 = (hh * PAIRS_PER_KV + pr) * PAIR_WIDTH
                mix[r0:r0 + cl, c0:c0 + PAIR_WIDTH] = acc[pr * cl:(pr + 1) * cl].astype(BF16)

    o_ref[...] = _conv_and_project(x_ref[...], bg_ref[...], u_ref[...], uh_ref[...],
                                   cw_ref, wo_ref, ubuf, mix)


def _conv_and_project(x, bg, u, uh, cw_ref, wo_ref, ubuf, mix):
    rows = u.shape[0]
    ubuf[0:CONV_HIST_ROWS, :] = uh
    ubuf[CONV_HIST_ROWS:CONV_HIST_ROWS + rows, :] = u
    conv = None
    for j in range(CONV_K):
        off = CONV_HIST_ROWS - (CONV_K - 1) + j
        term = cw_ref[j:j + 1, :] * ubuf[off:off + rows, :]
        conv = term if conv is None else conv + term
    mix[:, ATTN_WIDTH:MIX_WIDTH] = (bg * conv).astype(BF16)
    return x + jnp.dot(mix[...], wo_ref[...], preferred_element_type=F32)


def _mixer_full_body(x_ref, g_ref, win_ref, biast_ref, sinkt_ref, cw_ref, wo_ref,
                     o_ref, klast_ref, vlast_ref, ulast_ref,
                     qbuf, kdup, vta, vtb, ubuf, mix, sbuf, pbuf, rbuf, kcar, vcar, ucar, *, tq):
    b = pl.program_id(0)
    t = pl.program_id(1)
    cl = CHUNK
    kw = WINDOW + cl
    w = WINDOW + tq
    low = lax.broadcasted_iota(jnp.int32, (1, KV_WIDTH), 1) < HEAD_DIM

    @pl.when((b == 0) & (t == 0))
    def _():
        kcar[...] = jnp.zeros_like(kcar)
        vcar[...] = jnp.zeros_like(vcar)
        ucar[...] = jnp.zeros_like(ucar)

    x = x_ref[...]
    h = _rms(x, g_ref[...]).astype(BF16)

    def proj(c0, width):
        return jnp.dot(h, win_ref[:, c0:c0 + width], preferred_element_type=F32)

    qbuf[...] = (proj(0, ATTN_WIDTH) * ATTN_SCALE).astype(BF16)
    kv = proj(ATTN_WIDTH, 2 * KV_WIDTH)
    k = kv[:, 0:KV_WIDTH]
    v = kv[:, KV_WIDTH:2 * KV_WIDTH]
    c0 = ATTN_WIDTH + 2 * KV_WIDTH
    bg = proj(c0, CONV_WIDTH)
    u = proj(c0 + CONV_WIDTH, CONV_WIDTH) * proj(c0 + 2 * CONV_WIDTH, CONV_WIDTH)

    kh = kcar[...]
    vh = vcar[...]
    uh = jnp.where(t == 0, 0.0, ucar[...])
    k_tail = k[tq - WINDOW:tq, :]
    v_tail = v[tq - WINDOW:tq, :]
    u_tail = u[tq - CONV_HIST_ROWS:tq, :]
    kcar[...] = k_tail
    vcar[...] = v_tail
    ucar[...] = u_tail
    klast_ref[...] = k_tail
    vlast_ref[...] = v_tail
    ulast_ref[...] = u_tail

    kf = jnp.concatenate([kh, k], axis=0)
    kr = pltpu.roll(kf, HEAD_DIM, axis=1)
    kdup[0] = jnp.where(low, kf, kr).astype(BF16)
    kdup[1] = jnp.where(low, kr, kf).astype(BF16)

    vf = jnp.concatenate([vh, v, jnp.zeros((cl, KV_WIDTH), F32)], axis=0)
    vr = pltpu.roll(vf, HEAD_DIM, axis=1)
    for hh, vd in enumerate((jnp.where(low, vf, vr), jnp.where(low, vr, vf))):
        for blk in range(w // KV_WIDTH):
            b0 = blk * KV_WIDTH
            vta[hh, :, b0:b0 + KV_WIDTH] = vd[b0:b0 + KV_WIDTH].T.astype(BF16)
            vtb[hh, :, b0:b0 + KV_WIDTH] = vd[cl + b0:cl + b0 + KV_WIDTH].T.astype(BF16)

    n_cols = PAIRS_PER_KV * PAIR_WIDTH
    key = lax.broadcasted_iota(jnp.int32, (kw, n_cols), 0)
    rr = lax.broadcasted_iota(jnp.int32, (PAIR_WIDTH, PAIR_WIDTH), 0) < HEAD_DIM
    cc = lax.broadcasted_iota(jnp.int32, (PAIR_WIDTH, PAIR_WIDTH), 1) < HEAD_DIM
    diag = rr == cc
    for c in range(tq // cl):
        r0 = c * cl
        for hh in range(N_KV_HEADS):
            rows = []
            for pr in range(PAIRS_PER_KV):
                c0 = (hh * PAIRS_PER_KV + pr) * PAIR_WIDTH
                qp = qbuf[r0:r0 + cl, c0:c0 + PAIR_WIDTH]
                rows += [jnp.where(low, qp, 0), jnp.where(low, 0, qp)]
            qm = jnp.concatenate(rows, axis=0)
            s = lax.dot_general(kdup[hh, r0:r0 + kw, :], qm, (((1,), (1,)), ((), ())),
                                preferred_element_type=F32) + biast_ref[hh]
            if r0 < WINDOW:
                s = jnp.where(key + (t * tq + r0 - WINDOW) >= 0, s, NEG)
            j0 = (c * N_KV_HEADS + hh) * n_cols
            sbuf[:, j0:j0 + n_cols] = s

    s = sbuf[...]
    sk = sinkt_ref[...]
    m = jnp.maximum(jnp.max(s, axis=0, keepdims=True), sk)
    p = jnp.exp(s - m)
    den = jnp.sum(p, axis=0, keepdims=True) + jnp.exp(sk - m)
    pbuf[...] = p.astype(BF16)
    rbuf[...] = 1.0 / den

    for c in range(tq // cl):
        r0 = c * cl
        k0 = r0 if c % 2 == 0 else r0 - cl
        vt_ref = vta if c % 2 == 0 else vtb
        for hh in range(N_KV_HEADS):
            j0 = (c * N_KV_HEADS + hh) * n_cols
            ot = jnp.dot(vt_ref[hh, :, k0:k0 + kw], pbuf[:, j0:j0 + n_cols],
                         preferred_element_type=F32) * rbuf[:, j0:j0 + n_cols]
            for pr in range(PAIRS_PER_KV):
                blk = jnp.where(diag, ot[:, pr * PAIR_WIDTH:(pr + 1) * PAIR_WIDTH], 0.0).T
                c0 = (hh * PAIRS_PER_KV + pr) * PAIR_WIDTH
                mix[r0:r0 + cl, c0:c0 + PAIR_WIDTH] = (blk[0:cl] + blk[cl:2 * cl]).astype(BF16)

    o_ref[...] = _conv_and_project(x, bg, u, uh, cw_ref, wo_ref, ubuf, mix)


def _head_layout(bias, sinks, cl):
    kw = WINDOW + cl
    b5 = bias[:, :cl, :kw].reshape(N_KV_HEADS, PAIRS_PER_KV, 2, cl, kw)
    s5 = jnp.broadcast_to(sinks.reshape(N_KV_HEADS, PAIRS_PER_KV, 2, 1, 1),
                          (N_KV_HEADS, PAIRS_PER_KV, 2, cl, 1))
    return b5, s5


def _mixer_full(x, g, w_in, bias, sinks, conv_w, w_out, e, *, nb, t_len, tq, cast=()):
    n = x.shape[0]
    nt = t_len // tq
    jobs = [_CastJob(a, lead, nb * nt) for a, lead in cast]
    job_specs = [job.specs(lambda b, t: b * nt + t) for job in jobs]
    cl = CHUNK
    kw = WINDOW + cl
    assert tq % KV_WIDTH == 0 and tq >= WINDOW
    b5, s5 = _head_layout(bias, sinks, cl)
    n_cols = PAIRS_PER_KV * PAIR_WIDTH
    all_cols = (tq // cl) * N_KV_HEADS * n_cols
    bias_l = jnp.transpose(b5, (0, 4, 1, 2, 3)).reshape(N_KV_HEADS, kw, n_cols)
    sink_l = jnp.tile(s5.reshape(1, N_KV_HEADS * n_cols), (1, tq // cl))
    row = pl.BlockSpec((tq, D_MODEL), lambda b, t: (b * nt + t, 0))
    const = lambda a: pl.BlockSpec(a.shape, lambda b, t: (0,) * a.ndim)
    per_seq = lambda rows, width: pl.BlockSpec((rows, width), lambda b, t: (b, 0))
    est = 4 * tq * D_MODEL * 4 + (D_MODEL * IN_WIDTH + MIX_WIDTH * D_MODEL) * 2 \
        + kw * all_cols * 6 + 16 * tq * D_MODEL * 4
    whole = lambda a: pl.BlockSpec(a.shape, lambda b, t: (0, 0), pipeline_mode=pl.Buffered(1))
    return pl.pallas_call(
        _host_casts(functools.partial(_mixer_full_body, tq=tq), 7, 4, jobs),
        out_shape=(jax.ShapeDtypeStruct((n, D_MODEL), F32),
                   jax.ShapeDtypeStruct((nb * WINDOW, KV_WIDTH), F32),
                   jax.ShapeDtypeStruct((nb * WINDOW, KV_WIDTH), F32),
                   jax.ShapeDtypeStruct((nb * CONV_HIST_ROWS, CONV_WIDTH), F32))
        + tuple(job.out_shape for job in jobs),
        grid=(nb, nt),
        in_specs=[row, pl.BlockSpec((1, D_MODEL), lambda b, t: (0, 0)), whole(w_in),
                  const(bias_l), const(sink_l),
                  pl.BlockSpec((None, CONV_K, CONV_WIDTH), lambda b, t: (e, 0, 0)),
                  whole(w_out)] + [s[0] for s in job_specs],
        out_specs=(row, per_seq(WINDOW, KV_WIDTH), per_seq(WINDOW, KV_WIDTH),
                   per_seq(CONV_HIST_ROWS, CONV_WIDTH)) + tuple(s[1] for s in job_specs),
        scratch_shapes=[pltpu.VMEM((tq, ATTN_WIDTH), BF16),
                        pltpu.VMEM((N_KV_HEADS, WINDOW + tq, KV_WIDTH), BF16),
                        pltpu.VMEM((N_KV_HEADS, PAIR_WIDTH, WINDOW + tq), BF16),
                        pltpu.VMEM((N_KV_HEADS, PAIR_WIDTH, WINDOW + tq), BF16),
                        pltpu.VMEM((CONV_HIST_ROWS + tq, CONV_WIDTH), F32),
                        pltpu.VMEM((tq, MIX_WIDTH), BF16),
                        pltpu.VMEM((kw, all_cols), F32), pltpu.VMEM((kw, all_cols), BF16),
                        pltpu.VMEM((1, all_cols), F32),
                        pltpu.VMEM((WINDOW, KV_WIDTH), F32), pltpu.VMEM((WINDOW, KV_WIDTH), F32),
                        pltpu.VMEM((CONV_HIST_ROWS, CONV_WIDTH), F32)],
        compiler_params=pltpu.CompilerParams(
            dimension_semantics=("arbitrary", "arbitrary"), vmem_limit_bytes=_vmem_limit(est)),
        name="mixer_full",
    )(x, g.reshape(1, D_MODEL), w_in, bias_l, sink_l, conv_w, w_out, *[job.a for job in jobs])


def _mixer_ab(x, q, k, v, k_hist, v_hist, bg, u, u_hist, bias, sinks, conv_w, w_out, e,
              *, nb, t_len):
    n = x.shape[0]
    tq = cl = t_len
    assert t_len <= CHUNK
    row = lambda w: pl.BlockSpec((tq, w), lambda b: (b, 0))
    kh_spec = pl.BlockSpec((WINDOW, KV_WIDTH), lambda b: (b, 0))
    uh_spec = pl.BlockSpec((CONV_HIST_ROWS, CONV_WIDTH), lambda b: (b, 0))
    kw = WINDOW + cl
    b5, s5 = _head_layout(bias, sinks, cl)
    bias_l = jnp.transpose(b5, (0, 2, 1, 3, 4)).reshape(N_KV_HEADS, 2, PAIRS_PER_KV * cl, kw)
    sink_l = jnp.transpose(s5, (0, 2, 1, 3, 4)).reshape(N_KV_HEADS, 2, PAIRS_PER_KV * cl, 1)
    const = lambda a: pl.BlockSpec(a.shape, lambda b: (0,) * a.ndim)
    return pl.pallas_call(
        functools.partial(_mixer_ab_body, tq=tq, cl=cl),
        out_shape=jax.ShapeDtypeStruct((n, D_MODEL), F32),
        grid=(nb,),
        in_specs=[row(D_MODEL), row(ATTN_WIDTH), row(KV_WIDTH), row(KV_WIDTH),
                  kh_spec, kh_spec, row(CONV_WIDTH), row(CONV_WIDTH), uh_spec,
                  const(bias_l), const(sink_l),
                  pl.BlockSpec((None, CONV_K, CONV_WIDTH), lambda b: (e, 0, 0)),
                  pl.BlockSpec((MIX_WIDTH, D_MODEL), lambda b: (0, 0))],
        out_specs=row(D_MODEL),
        scratch_shapes=[pltpu.VMEM((N_KV_HEADS, 2, WINDOW + tq, KV_WIDTH), BF16),
                        pltpu.VMEM((N_KV_HEADS, 2, WINDOW + tq, KV_WIDTH), BF16),
                        pltpu.VMEM((CONV_HIST_ROWS + tq, CONV_WIDTH), F32),
                        pltpu.VMEM((tq, MIX_WIDTH), BF16)],
        compiler_params=pltpu.CompilerParams(dimension_semantics=("arbitrary",)),
        name="mixer_ab",
    )(x, q, k, v, k_hist, v_hist, bg, u, u_hist, bias_l, sink_l, conv_w, w_out)


def _pool_rows(x, hist, g, pw_ref, sc_ref, o_ref, r0, pos0):
    rows = x.shape[0]
    h = _rms(x, g)
    sums = []
    cur = jnp.concatenate([hist, h], axis=0)
    for gi, w in enumerate(POOL_SIZES):
        assert w == 2 ** (gi + 1)
        cur = cur + pltpu.roll(cur, w // 2, axis=0)
        sums.append(cur[POOL_HIST_ROWS:POOL_HIST_ROWS + rows, 0:POOL_GROUP])
        if gi + 1 < len(POOL_SIZES):
            cur = cur[:, POOL_GROUP:]

    pos = pos0 + lax.broadcasted_iota(jnp.int32, (rows, 1), 0)
    for gi, w in enumerate(POOL_SIZES):
        sl = slice(gi * POOL_GROUP, (gi + 1) * POOL_GROUP)
        cnt = jnp.minimum(pos + 1, w).astype(F32)
        d = (sums[gi] / cnt - h[:, sl]).astype(BF16)
        y = jnp.dot(d, pw_ref[gi], preferred_element_type=F32)
        o_ref[r0:r0 + rows, sl] = x[:, sl] + y * sc_ref[:, sl]
    return h[rows - POOL_HIST_ROWS:rows, :]


def _mixer_c_body(x_ref, hist_ref, g_ref, pw_ref, sc_ref, o_ref, st_ref, *, first_pos):
    st_ref[...] = _pool_rows(x_ref[...], hist_ref[...], g_ref[...], pw_ref, sc_ref, o_ref, 0, first_pos)


def _pool_specs(o):
    return [pl.BlockSpec((None, len(POOL_SIZES), POOL_GROUP, POOL_GROUP), lambda *a: (o, 0, 0, 0)),
            pl.BlockSpec((None, 1, D_MODEL), lambda *a: (o, 0, 0))]


def _mixer_c(x, hist, g, pool_w, scale, o, *, nb, t_len, first_pos):
    n = x.shape[0]
    row = pl.BlockSpec((t_len, D_MODEL), lambda b: (b, 0))
    hist_spec = pl.BlockSpec((POOL_HIST_ROWS, D_MODEL), lambda b: (b, 0))
    return pl.pallas_call(
        functools.partial(_mixer_c_body, first_pos=first_pos),
        out_shape=(jax.ShapeDtypeStruct((n, D_MODEL), F32),
                   jax.ShapeDtypeStruct((nb * POOL_HIST_ROWS, D_MODEL), F32)),
        grid=(nb,),
        in_specs=[row, hist_spec, pl.BlockSpec((1, D_MODEL), lambda b: (0, 0))] + _pool_specs(o),
        out_specs=(row, hist_spec),
        compiler_params=pltpu.CompilerParams(dimension_semantics=("arbitrary",)),
        name="mixer_c",
    )(x, hist, g.reshape(1, D_MODEL), pool_w, scale.reshape(scale.shape[0], 1, D_MODEL))


def _ffn_pool_body(x_ref, g1_ref, wg_ref, wu_ref, wd_ref, g2_ref, pw_ref, sc_ref,
                   o_ref, st_ref, a_ref, carry_ref, *, tm, sub, tiles_per_seq):
    i = pl.program_id(0)
    t = i % tiles_per_seq

    @pl.when(i == 0)
    def _():
        carry_ref[...] = jnp.zeros_like(carry_ref)

    hist = jnp.where(t == 0, 0.0, carry_ref[...])
    g1 = g1_ref[...]
    g2 = g2_ref[...]
    blocks = [_swiglu_rows(x_ref[r0:r0 + sub, :], g1, wg_ref, wu_ref, wd_ref, a_ref, r0)
              for r0 in range(0, tm, sub)]
    for s, xb in enumerate(blocks):
        hist = _pool_rows(xb, hist, g2, pw_ref, sc_ref, o_ref, s * sub, t * tm + s * sub)
    carry_ref[...] = hist
    st_ref[...] = hist


def _ffn_pool(x, g1, w16, g2, pool_w, scale, o, *, nb, t_len, tm, sub):
    n = x.shape[0]
    tiles_per_seq = t_len // tm
    whole = lambda a: pl.BlockSpec(a.shape, lambda i: (0, 0), pipeline_mode=pl.Buffered(1))
    vec = pl.BlockSpec((1, D_MODEL), lambda i: (0, 0))
    row = pl.BlockSpec((tm, D_MODEL), lambda i: (i, 0))
    est = 3 * D_MODEL * D_FF * 2 + 4 * tm * D_MODEL * 4 + tm * D_FF * 2 + 6 * tm * 1024 * 4
    return pl.pallas_call(
        functools.partial(_ffn_pool_body, tm=tm, sub=sub, tiles_per_seq=tiles_per_seq),
        out_shape=(jax.ShapeDtypeStruct((n, D_MODEL), F32),
                   jax.ShapeDtypeStruct((nb * POOL_HIST_ROWS, D_MODEL), F32)),
        grid=(n // tm,),
        in_specs=[row, vec, whole(w16[0]), whole(w16[1]), whole(w16[2]), vec] + _pool_specs(o),
        out_specs=(row, pl.BlockSpec((POOL_HIST_ROWS, D_MODEL), lambda i: (i // tiles_per_seq, 0))),
        scratch_shapes=[pltpu.VMEM((tm, D_FF), BF16), pltpu.VMEM((POOL_HIST_ROWS, D_MODEL), F32)],
        compiler_params=pltpu.CompilerParams(
            dimension_semantics=("arbitrary",), vmem_limit_bytes=_vmem_limit(est)),
        name="ffn_pool",
    )(x, g1.reshape(1, D_MODEL), *w16, g2.reshape(1, D_MODEL), pool_w,
      scale.reshape(scale.shape[0], 1, D_MODEL))


def kernel(x_prompt, x_sample, cache_attn_k, cache_attn_v, cache_conv, cache_pool, norm_g, ffn_w_gate, ffn_w_up, ffn_w_down, mix_w_in, mix_w_out, conv_w, attn_sinks, rel_bias_table, pool_w, pool_scale, final_norm_g):
    past_len = 1024
    nb, t_len, _ = x_prompt.shape
    nbs, ts_len, _ = x_sample.shape
    xp = x_prompt.reshape(nb * t_len, D_MODEL)
    xs = x_sample.reshape(nbs * ts_len, D_MODEL)
    tf = min(FFN_ROW_TILE, xp.shape[0])
    tq = min(MIXER_ROW_TILE, t_len)
    assert t_len % tf == 0
    ts = xs.shape[0]
    pw = pool_w.astype(BF16)
    bias = _relbias(rel_bias_table)
    ffn_w = (ffn_w_gate, ffn_w_up, ffn_w_down)

    w16 = {0: tuple(a[0, 0].astype(BF16) for a in ffn_w)}
    pending = list(range(1, 2 * DEPTH))
    mix16 = {}

    def ffn_weights(n):
        if n not in w16:
            pending.remove(n)
            w16[n] = tuple(a[n // 2, n % 2].astype(BF16) for a in ffn_w)
        return w16[n]

    def next_casts(n):
        if not pending or pending[0] <= n:
            return None, []
        m = pending.pop(0)
        return m, [(a, (m // 2, m % 2)) for a in ffn_w]

    def prompt_ffn(x, layer, j, final):
        n = 2 * layer + j
        m, cast = next_casts(n)
        if n == 0:
            cast = cast + [(a, (e,)) for e in range(mix_w_in.shape[0]) for a in (mix_w_in, mix_w_out)]
        y, *done = _ffn(x, norm_g[layer, 2 * j], ffn_weights(n), final_norm_g, final, tf, cast=cast)
        if m is not None:
            w16[m] = tuple(done[:3])
            done = done[3:]
        for e in range(len(done) // 2):
            mix16[e] = (done[2 * e], done[2 * e + 1])
        return y

    def sample_ffn(x, layer, j, final):
        return _ffn(x, norm_g[layer, 2 * j], ffn_weights(2 * layer + j), final_norm_g, final, ts)[0]

    def last_rows(a, seqs, rows, keep, tail_shape):
        return a.reshape((seqs, rows) + tail_shape)[:, rows - keep:]

    outs = {name: [] for name in ("kp", "vp", "cp", "pp", "ks", "vs", "cs", "ps")}
    for layer in range(DEPTH):
        last = layer == DEPTH - 1
        if layer % 2 == 0:
            e = layer // 2
            xp = prompt_ffn(xp, layer, 0, False)
            xs = sample_ffn(xs, layer, 0, False)
            if e not in mix16:
                mix16[e] = (mix_w_in[e].astype(BF16), mix_w_out[e].astype(BF16))
            w_in, w_out = mix16[e]
            m, cast = next_casts(2 * layer)
            xp, k, v, u, *done = _mixer_full(xp, norm_g[layer, 1], w_in, bias, attn_sinks[e], conv_w,
                                             w_out, e, nb=nb, t_len=t_len, tq=tq, cast=cast)
            if m is not None:
                w16[m] = tuple(done)
            keep = min(WINDOW, t_len)
            outs["kp"].append(last_rows(k, nb, WINDOW, keep, (N_KV_HEADS, HEAD_DIM)))
            outs["vp"].append(last_rows(v, nb, WINDOW, keep, (N_KV_HEADS, HEAD_DIM)))
            outs["cp"].append(last_rows(u, nb, CONV_HIST_ROWS, CONV_K - 1, (CONV_WIDTH,)))
            q, k, v, bg, u = _inproj(xs, norm_g[layer, 1], w_in, min(ROW_TILE, ts))
            k_hist = cache_attn_k[e].reshape(nbs * WINDOW, KV_WIDTH)
            v_hist = cache_attn_v[e].reshape(nbs * WINDOW, KV_WIDTH)
            u_hist = jnp.pad(cache_conv[e], ((0, 0), (CONV_HIST_ROWS - (CONV_K - 1), 0), (0, 0))
                             ).reshape(nbs * CONV_HIST_ROWS, CONV_WIDTH)
            xs = _mixer_ab(xs, q, k, v, k_hist, v_hist, bg, u, u_hist, bias, attn_sinks[e],
                           conv_w, w_out, e, nb=nbs, t_len=ts_len)
            keep = min(WINDOW, ts_len)
            outs["ks"].append(last_rows(k, nbs, ts_len, keep, (N_KV_HEADS, HEAD_DIM)))
            outs["vs"].append(last_rows(v, nbs, ts_len, keep, (N_KV_HEADS, HEAD_DIM)))
            outs["cs"].append(last_rows(u, nbs, ts_len, CONV_K - 1, (CONV_WIDTH,)))
        else:
            o = layer // 2
            xp, st = _ffn_pool(xp, norm_g[layer, 0], ffn_weights(2 * layer), norm_g[layer, 1],
                               pw, pool_scale, o, nb=nb, t_len=t_len, tm=tf, sub=tf // 2)
            outs["pp"].append(last_rows(st, nb, POOL_HIST_ROWS, POOL_MAX - 1, (D_MODEL,)))
            xs = sample_ffn(xs, layer, 0, False)
            hist = jnp.pad(cache_pool[o], ((0, 0), (POOL_HIST_ROWS - (POOL_MAX - 1), 0), (0, 0))
                           ).reshape(nbs * POOL_HIST_ROWS, D_MODEL)
            xs, st = _mixer_c(xs, hist, norm_g[layer, 1], pw, pool_scale, o,
                              nb=nbs, t_len=ts_len, first_pos=past_len)
            outs["ps"].append(last_rows(st, nbs, POOL_HIST_ROWS, POOL_MAX - 1, (D_MODEL,)))
        xp = prompt_ffn(xp, layer, 1, last)
        xs = sample_ffn(xs, layer, 1, last)
    stacked = {name: jnp.stack(v) for name, v in outs.items()}
    return (xp.reshape(nb, t_len, D_MODEL), xs.reshape(nbs, ts_len, D_MODEL),
            stacked["kp"], stacked["vp"], stacked["cp"], stacked["pp"],
            stacked["ks"], stacked["vs"], stacked["cs"], stacked["ps"])
```

```python
import functools
import math

import jax
import jax.numpy as jnp
from jax import lax
from jax.experimental import pallas as pl
from jax.experimental.pallas import tpu as pltpu

D_MODEL = 1024
DEPTH = 2
CHUNK = 64
N_HEADS = 8
N_KV_HEADS = 2
Q_PER_KV = N_HEADS // N_KV_HEADS
HEAD_DIM = 64
ATTN_WIDTH = N_HEADS * HEAD_DIM
KV_WIDTH = N_KV_HEADS * HEAD_DIM
WINDOW = 128
N_BUCKETS = 32
MAX_DISTANCE = 128
CONV_WIDTH = D_MODEL // 2
CONV_K = 3
IN_WIDTH = ATTN_WIDTH + 2 * KV_WIDTH + 3 * CONV_WIDTH
MIX_WIDTH = ATTN_WIDTH + CONV_WIDTH
POOL_SIZES = (2, 4, 8, 16)
POOL_GROUP = D_MODEL // len(POOL_SIZES)
POOL_MAX = max(POOL_SIZES)
D_FF = 2816
EPS = 1e-6
NEG = -1e30

V7X_VMEM_BYTES = 64 * 1024 * 1024
SUBLANES_F32 = 8
CONV_HIST_ROWS = SUBLANES_F32
POOL_HIST_ROWS = 16
ROW_TILE = 512
MIXER_ROW_TILE = 1024
FFN_ROW_TILE = 1024
FF_CHUNK = 512
FF_CHUNKS = tuple((c0, min(FF_CHUNK, D_FF - c0)) for c0 in range(0, D_FF, FF_CHUNK))

F32 = jnp.float32
BF16 = jnp.bfloat16


def _vmem_limit(nbytes):
    return int(min(V7X_VMEM_BYTES - (4 << 20), max(32 << 20, nbytes)))


def _rms(x, g):
    return x * lax.rsqrt(jnp.mean(x * x, axis=-1, keepdims=True) + EPS) * g


def _relbias_body(tab_ref, bkt_ref, o_ref):
    bkt = bkt_ref[...]
    for h in range(N_HEADS):
        acc = jnp.zeros(bkt.shape, F32)
        for b in range(N_BUCKETS):
            acc = jnp.where(bkt == b, tab_ref[b, h], acc)
        o_ref[h] = acc


def _rel_bucket(rel):
    half = N_BUCKETS // 2
    ret = jnp.where(rel > 0, half, 0)
    n = jnp.abs(rel)
    max_exact = half // 2
    nf = jnp.maximum(n, 1).astype(F32)
    large = max_exact + (jnp.log(nf / max_exact) / math.log(MAX_DISTANCE / max_exact)
                         * (half - max_exact)).astype(jnp.int32)
    large = jnp.minimum(large, half - 1)
    return ret + jnp.where(n < max_exact, n, large)


def _relbias(table):
    rel = (jnp.arange(WINDOW + CHUNK, dtype=jnp.int32) - WINDOW)[None, :] \
        - jnp.arange(CHUNK, dtype=jnp.int32)[:, None]
    bucket = _rel_bucket(rel).astype(jnp.int32)
    return pl.pallas_call(
        _relbias_body,
        out_shape=jax.ShapeDtypeStruct((N_HEADS, CHUNK, WINDOW + CHUNK), F32),
        in_specs=[pl.BlockSpec(memory_space=pltpu.SMEM),
                  pl.BlockSpec(memory_space=pltpu.VMEM)],
        out_specs=pl.BlockSpec(memory_space=pltpu.VMEM),
        name="relbias",
    )(table, bucket)


def _swiglu_rows(x, g, wg_ref, wu_ref, wd_ref, a_ref, r0):
    rows = x.shape[0]
    xg = (x * g).astype(BF16)
    r = lax.rsqrt(jnp.mean(x * x, axis=-1, keepdims=True) + EPS)
    scale = {cw: jnp.broadcast_to(r, (rows, cw)) for cw in {cw for _, cw in FF_CHUNKS}}
    for c0, cw in FF_CHUNKS:
        gate = jnp.dot(xg, wg_ref[:, c0:c0 + cw], preferred_element_type=F32) * scale[cw]
        up = jnp.dot(xg, wu_ref[:, c0:c0 + cw], preferred_element_type=F32) * scale[cw]
        a_ref[r0:r0 + rows, c0:c0 + cw] = (jax.nn.silu(gate) * up).astype(BF16)
    y = jnp.dot(a_ref[r0:r0 + rows, :], wd_ref[...], preferred_element_type=F32)
    return x + 0.5 * y


BF16_SUBLANES = 16


class _CastJob:
    def __init__(self, a, lead, steps):
        self.a, self.lead = a, tuple(lead)
        rows, cols = a.shape[-2:]
        self.chunk = next(c for c in range(BF16_SUBLANES, rows + 1, BF16_SUBLANES)
                          if rows % c == 0 and steps % (rows // c) == 0)
        self.repeat = steps // (rows // self.chunk)
        self.out_shape = jax.ShapeDtypeStruct((rows, cols), BF16)

    def specs(self, step_of):
        none = (None,) * len(self.lead)
        cols = self.a.shape[-1]
        return (pl.BlockSpec(none + (self.chunk, cols),
                             lambda *g: self.lead + (step_of(*g) // self.repeat, 0)),
                pl.BlockSpec((self.chunk, cols), lambda *g: (step_of(*g) // self.repeat, 0)))


def _host_casts(body, n_in, n_out, jobs):
    k = len(jobs)

    def hosted(*refs):
        ins, cast_in = refs[:n_in], refs[n_in:n_in + k]
        outs, cast_out = refs[n_in + k:n_in + k + n_out], refs[n_in + k + n_out:n_in + 2 * k + n_out]
        for src, dst in zip(cast_in, cast_out):
            dst[...] = src[...].astype(BF16)
        body(*ins, *outs, *refs[n_in + 2 * k + n_out:])
    return hosted


def _ffn_body(x_ref, g_ref, wg_ref, wu_ref, wd_ref, fg_ref, o_ref, a_ref, *, final_norm):
    out = _swiglu_rows(x_ref[...], g_ref[...], wg_ref, wu_ref, wd_ref, a_ref, 0)
    if final_norm:
        out = _rms(out, fg_ref[...])
    o_ref[...] = out


def _ffn(x, g, w16, fg, final_norm, tm, cast=()):
    n = x.shape[0]
    steps = n // tm
    jobs = [_CastJob(a, lead, steps) for a, lead in cast]
    job_specs = [job.specs(lambda i: i) for job in jobs]
    whole = lambda a: pl.BlockSpec(a.shape, lambda i: (0, 0), pipeline_mode=pl.Buffered(1))
    vec = pl.BlockSpec((1, D_MODEL), lambda i: (0, 0))
    row = pl.BlockSpec((tm, D_MODEL), lambda i: (i, 0))
    est = 3 * D_MODEL * D_FF * 2 + 4 * tm * D_MODEL * 4 + tm * D_FF * 2 + 5 * tm * 1024 * 4 \
        + sum(6 * job.chunk * job.a.shape[-1] * 2 for job in jobs)
    return pl.pallas_call(
        _host_casts(functools.partial(_ffn_body, final_norm=final_norm), 6, 1, jobs),
        out_shape=(jax.ShapeDtypeStruct((n, D_MODEL), F32),) + tuple(job.out_shape for job in jobs),
        grid=(steps,),
        in_specs=[row, vec, whole(w16[0]), whole(w16[1]), whole(w16[2]), vec] + [s[0] for s in job_specs],
        out_specs=(row,) + tuple(s[1] for s in job_specs),
        scratch_shapes=[pltpu.VMEM((tm, D_FF), BF16)],
        compiler_params=pltpu.CompilerParams(
            dimension_semantics=("arbitrary",), vmem_limit_bytes=_vmem_limit(est)),
        name="ffn",
    )(x, g.reshape(1, D_MODEL), *w16, fg.reshape(1, D_MODEL), *[job.a for job in jobs])


def _inproj_body(x_ref, g_ref, w_ref, q_ref, k_ref, v_ref, bg_ref, u_ref):
    h = _rms(x_ref[...], g_ref[...]).astype(BF16)
    p = jnp.dot(h, w_ref[...], preferred_element_type=F32)
    o = 0
    q_ref[...] = (p[:, o:o + ATTN_WIDTH] * (HEAD_DIM ** -0.5)).astype(BF16)
    o += ATTN_WIDTH
    k_ref[...] = p[:, o:o + KV_WIDTH]
    o += KV_WIDTH
    v_ref[...] = p[:, o:o + KV_WIDTH]
    o += KV_WIDTH
    bg_ref[...] = p[:, o:o + CONV_WIDTH]
    o += CONV_WIDTH
    u_ref[...] = p[:, o:o + CONV_WIDTH] * p[:, o + CONV_WIDTH:o + 2 * CONV_WIDTH]


def _inproj(x, g, w_in, tm):
    n = x.shape[0]
    row = lambda w: pl.BlockSpec((tm, w), lambda i: (i, 0))
    est = 2 * D_MODEL * IN_WIDTH * 2 + 2 * tm * (D_MODEL + IN_WIDTH) * 4 + 2 * tm * IN_WIDTH * 4
    return pl.pallas_call(
        _inproj_body,
        out_shape=(jax.ShapeDtypeStruct((n, ATTN_WIDTH), BF16),
                   jax.ShapeDtypeStruct((n, KV_WIDTH), F32),
                   jax.ShapeDtypeStruct((n, KV_WIDTH), F32),
                   jax.ShapeDtypeStruct((n, CONV_WIDTH), F32),
                   jax.ShapeDtypeStruct((n, CONV_WIDTH), F32)),
        grid=(n // tm,),
        in_specs=[row(D_MODEL), pl.BlockSpec((1, D_MODEL), lambda i: (0, 0)),
                  pl.BlockSpec((D_MODEL, IN_WIDTH), lambda i: (0, 0))],
        out_specs=(row(ATTN_WIDTH), row(KV_WIDTH), row(KV_WIDTH), row(CONV_WIDTH), row(CONV_WIDTH)),
        compiler_params=pltpu.CompilerParams(
            dimension_semantics=("arbitrary",), vmem_limit_bytes=_vmem_limit(est)),
        name="inproj",
    )(x, g.reshape(1, D_MODEL), w_in)


PAIR_WIDTH = 2 * HEAD_DIM
PAIRS_PER_KV = Q_PER_KV // 2
ATTN_SCALE = HEAD_DIM ** -0.5


def _mixer_ab_body(x_ref, q_ref, k_ref, v_ref, kh_ref, vh_ref, bg_ref, u_ref, uh_ref,
                   bias_ref, sink_ref, cw_ref, wo_ref, o_ref,
                   kbuf, vbuf, ubuf, mix, *, nb, cl):
    kw = WINDOW + cl
    low = lax.broadcasted_iota(jnp.int32, (1, KV_WIDTH), 1) < HEAD_DIM

    def stage(buf, a, r0):
        b = pltpu.roll(a, HEAD_DIM, axis=1)
        rows = a.shape[0]
        buf[0, 0, r0:r0 + rows, :] = jnp.where(low, a, 0.0).astype(BF16)
        buf[0, 1, r0:r0 + rows, :] = jnp.where(low, 0.0, b).astype(BF16)
        buf[1, 0, r0:r0 + rows, :] = jnp.where(low, b, 0.0).astype(BF16)
        buf[1, 1, r0:r0 + rows, :] = jnp.where(low, 0.0, a).astype(BF16)

    for sq in range(nb):
        r0, h0 = sq * cl, sq * WINDOW
        kb, vb = kbuf.at[sq], vbuf.at[sq]
        stage(kb, kh_ref[h0:h0 + WINDOW, :], 0)
        stage(kb, k_ref[r0:r0 + cl, :], WINDOW)
        stage(vb, vh_ref[h0:h0 + WINDOW, :], 0)
        stage(vb, v_ref[r0:r0 + cl, :], WINDOW)
        for hh in range(N_KV_HEADS):
            ql = jnp.concatenate(
                [q_ref[r0:r0 + cl, (hh * PAIRS_PER_KV + pr) * PAIR_WIDTH:
                       (hh * PAIRS_PER_KV + pr + 1) * PAIR_WIDTH] for pr in range(PAIRS_PER_KV)],
                axis=0)
            acc = None
            for lh in range(2):
                s = lax.dot_general(ql, kb[hh, lh], (((1,), (1,)), ((), ())),
                                    preferred_element_type=F32) + bias_ref[hh, lh]
                sk = sink_ref[hh, lh]
                m = jnp.maximum(jnp.max(s, axis=-1, keepdims=True), sk)
                p = jnp.exp(s - m)
                den = jnp.sum(p, axis=-1, keepdims=True) + jnp.exp(sk - m)
                o = jnp.dot(p.astype(BF16), vb[hh, lh], preferred_element_type=F32) * (1.0 / den)
                acc = o if acc is None else acc + o
            for pr in range(PAIRS_PER_KV):
                c0 = (hh * PAIRS_PER_KV + pr) * PAIR_WIDTH
                mix[r0:r0 + cl, c0:c0 + PAIR_WIDTH] = acc[pr * cl:(pr + 1) * cl].astype(BF16)
        h8 = sq * CONV_HIST_ROWS
        conv = _gated_conv(bg_ref[r0:r0 + cl, :], u_ref[r0:r0 + cl, :],
                           uh_ref[h8:h8 + CONV_HIST_ROWS, :], cw_ref, ubuf.at[sq])
        mix[r0:r0 + cl, ATTN_WIDTH:MIX_WIDTH] = conv.astype(BF16)

    o_ref[...] = x_ref[...] + jnp.dot(mix[...], wo_ref[...], preferred_element_type=F32)


def _gated_conv(bg, u, uh, cw_ref, ubuf):
    rows = u.shape[0]
    ubuf[0:CONV_HIST_ROWS, :] = uh
    ubuf[CONV_HIST_ROWS:CONV_HIST_ROWS + rows, :] = u
    conv = None
    for j in range(CONV_K):
        off = CONV_HIST_ROWS - (CONV_K - 1) + j
        term = cw_ref[j:j + 1, :] * ubuf[off:off + rows, :]
        conv = term if conv is None else conv + term
    return bg * conv


def _mixer_full_body(x_ref, g_ref, win_ref, biast_ref, sinkt_ref, cw_ref, wo_ref,
                     o_ref, klast_ref, vlast_ref, ulast_ref,
                     qbuf, kdup, vta, vtb, ubuf, mix, sbuf, pbuf, rbuf, kcar, vcar, ucar, *, tq):
    b = pl.program_id(0)
    t = pl.program_id(1)
    cl = CHUNK
    kw = WINDOW + cl
    w = WINDOW + tq
    low = lax.broadcasted_iota(jnp.int32, (1, KV_WIDTH), 1) < HEAD_DIM

    @pl.when((b == 0) & (t == 0))
    def _():
        kcar[...] = jnp.zeros_like(kcar)
        vcar[...] = jnp.zeros_like(vcar)
        ucar[...] = jnp.zeros_like(ucar)

    x = x_ref[...]
    h = _rms(x, g_ref[...]).astype(BF16)

    def proj(c0, width):
        return jnp.dot(h, win_ref[:, c0:c0 + width], preferred_element_type=F32)

    qbuf[...] = (proj(0, ATTN_WIDTH) * ATTN_SCALE).astype(BF16)
    kv = proj(ATTN_WIDTH, 2 * KV_WIDTH)
    k = kv[:, 0:KV_WIDTH]
    v = kv[:, KV_WIDTH:2 * KV_WIDTH]
    c0 = ATTN_WIDTH + 2 * KV_WIDTH
    bg = proj(c0, CONV_WIDTH)
    u = proj(c0 + CONV_WIDTH, CONV_WIDTH) * proj(c0 + 2 * CONV_WIDTH, CONV_WIDTH)

    kh = kcar[...]
    vh = vcar[...]
    uh = jnp.where(t == 0, 0.0, ucar[...])
    k_tail = k[tq - WINDOW:tq, :]
    v_tail = v[tq - WINDOW:tq, :]
    u_tail = u[tq - CONV_HIST_ROWS:tq, :]
    kcar[...] = k_tail
    vcar[...] = v_tail
    ucar[...] = u_tail
    klast_ref[...] = k_tail
    vlast_ref[...] = v_tail
    ulast_ref[...] = u_tail

    kf = jnp.concatenate([kh, k], axis=0)
    kr = pltpu.roll(kf, HEAD_DIM, axis=1)
    kdup[0] = jnp.where(low, kf, kr).astype(BF16)
    kdup[1] = jnp.where(low, kr, kf).astype(BF16)

    vf = jnp.concatenate([vh, v, jnp.zeros((cl, KV_WIDTH), F32)], axis=0)
    vr = pltpu.roll(vf, HEAD_DIM, axis=1)
    for hh, vd in enumerate((jnp.where(low, vf, vr), jnp.where(low, vr, vf))):
        for blk in range(w // KV_WIDTH):
            b0 = blk * KV_WIDTH
            vta[hh, :, b0:b0 + KV_WIDTH] = vd[b0:b0 + KV_WIDTH].T.astype(BF16)
            vtb[hh, :, b0:b0 + KV_WIDTH] = vd[cl + b0:cl + b0 + KV_WIDTH].T.astype(BF16)

    n_cols = PAIRS_PER_KV * PAIR_WIDTH
    key = lax.broadcasted_iota(jnp.int32, (kw, n_cols), 0)
    rr = lax.broadcasted_iota(jnp.int32, (PAIR_WIDTH, PAIR_WIDTH), 0) < HEAD_DIM
    cc = lax.broadcasted_iota(jnp.int32, (PAIR_WIDTH, PAIR_WIDTH), 1) < HEAD_DIM
    diag = rr == cc
    for c in range(tq // cl):
        r0 = c * cl
        for hh in range(N_KV_HEADS):
            rows = []
            for pr in range(PAIRS_PER_KV):
                c0 = (hh * PAIRS_PER_KV + pr) * PAIR_WIDTH
                qp = qbuf[r0:r0 + cl, c0:c0 + PAIR_WIDTH]
                rows += [jnp.where(low, qp, 0), jnp.where(low, 0, qp)]
            qm = jnp.concatenate(rows, axis=0)
            s = lax.dot_general(kdup[hh, r0:r0 + kw, :], qm, (((1,), (1,)), ((), ())),
                                preferred_element_type=F32) + biast_ref[hh]
            if r0 < WINDOW:
                s = jnp.where(key + (t * tq + r0 - WINDOW) >= 0, s, NEG)
            j0 = (c * N_KV_HEADS + hh) * n_cols
            sbuf[:, j0:j0 + n_cols] = s

    s = sbuf[...]
    sk = sinkt_ref[...]
    m = jnp.maximum(jnp.max(s, axis=0, keepdims=True), sk)
    p = jnp.exp(s - m)
    den = jnp.sum(p, axis=0, keepdims=True) + jnp.exp(sk - m)
    pbuf[...] = p.astype(BF16)
    rbuf[...] = 1.0 / den

    for c in range(tq // cl):
        r0 = c * cl
        k0 = r0 if c % 2 == 0 else r0 - cl
        vt_ref = vta if c % 2 == 0 else vtb
        for hh in range(N_KV_HEADS):
            j0 = (c * N_KV_HEADS + hh) * n_cols
            ot = jnp.dot(vt_ref[hh, :, k0:k0 + kw], pbuf[:, j0:j0 + n_cols],
                         preferred_element_type=F32) * rbuf[:, j0:j0 + n_cols]
            for pr in range(PAIRS_PER_KV):
                blk = jnp.where(diag, ot[:, pr * PAIR_WIDTH:(pr + 1) * PAIR_WIDTH], 0.0).T
                c0 = (hh * PAIRS_PER_KV + pr) * PAIR_WIDTH
                mix[r0:r0 + cl, c0:c0 + PAIR_WIDTH] = (blk[0:cl] + blk[cl:2 * cl]).astype(BF16)

    mix[:, ATTN_WIDTH:MIX_WIDTH] = _gated_conv(bg, u, uh, cw_ref, ubuf).astype(BF16)
    o_ref[...] = x + jnp.dot(mix[...], wo_ref[...], preferred_element_type=F32)


def _head_layout(bias, sinks, cl):
    kw = WINDOW + cl
    b5 = bias[:, :cl, :kw].reshape(N_KV_HEADS, PAIRS_PER_KV, 2, cl, kw)
    s5 = jnp.broadcast_to(sinks.reshape(N_KV_HEADS, PAIRS_PER_KV, 2, 1, 1),
                          (N_KV_HEADS, PAIRS_PER_KV, 2, cl, 1))
    return b5, s5


def _mixer_full(x, g, w_in, bias, sinks, conv_w, w_out, e, *, nb, t_len, tq, cast=()):
    n = x.shape[0]
    nt = t_len // tq
    jobs = [_CastJob(a, lead, nb * nt) for a, lead in cast]
    job_specs = [job.specs(lambda b, t: b * nt + t) for job in jobs]
    cl = CHUNK
    kw = WINDOW + cl
    assert tq % KV_WIDTH == 0 and tq >= WINDOW
    b5, s5 = _head_layout(bias, sinks, cl)
    n_cols = PAIRS_PER_KV * PAIR_WIDTH
    all_cols = (tq // cl) * N_KV_HEADS * n_cols
    bias_l = jnp.transpose(b5, (0, 4, 1, 2, 3)).reshape(N_KV_HEADS, kw, n_cols)
    sink_l = jnp.tile(s5.reshape(1, N_KV_HEADS * n_cols), (1, tq // cl))
    row = pl.BlockSpec((tq, D_MODEL), lambda b, t: (b * nt + t, 0))
    const = lambda a: pl.BlockSpec(a.shape, lambda b, t: (0,) * a.ndim)
    per_seq = lambda rows, width: pl.BlockSpec((rows, width), lambda b, t: (b, 0))
    est = 4 * tq * D_MODEL * 4 + (D_MODEL * IN_WIDTH + MIX_WIDTH * D_MODEL) * 2 \
        + kw * all_cols * 6 + 16 * tq * D_MODEL * 4
    whole = lambda a: pl.BlockSpec(a.shape, lambda b, t: (0, 0), pipeline_mode=pl.Buffered(1))
    return pl.pallas_call(
        _host_casts(functools.partial(_mixer_full_body, tq=tq), 7, 4, jobs),
        out_shape=(jax.ShapeDtypeStruct((n, D_MODEL), F32),
                   jax.ShapeDtypeStruct((nb * WINDOW, KV_WIDTH), F32),
                   jax.ShapeDtypeStruct((nb * WINDOW, KV_WIDTH), F32),
                   jax.ShapeDtypeStruct((nb * CONV_HIST_ROWS, CONV_WIDTH), F32))
        + tuple(job.out_shape for job in jobs),
        grid=(nb, nt),
        in_specs=[row, pl.BlockSpec((1, D_MODEL), lambda b, t: (0, 0)), whole(w_in),
                  const(bias_l), const(sink_l),
                  pl.BlockSpec((None, CONV_K, CONV_WIDTH), lambda b, t: (e, 0, 0)),
                  whole(w_out)] + [s[0] for s in job_specs],
        out_specs=(row, per_seq(WINDOW, KV_WIDTH), per_seq(WINDOW, KV_WIDTH),
                   per_seq(CONV_HIST_ROWS, CONV_WIDTH)) + tuple(s[1] for s in job_specs),
        scratch_shapes=[pltpu.VMEM((tq, ATTN_WIDTH), BF16),
                        pltpu.VMEM((N_KV_HEADS, WINDOW + tq, KV_WIDTH), BF16),
                        pltpu.VMEM((N_KV_HEADS, PAIR_WIDTH, WINDOW + tq), BF16),
                        pltpu.VMEM((N_KV_HEADS, PAIR_WIDTH, WINDOW + tq), BF16),
                        pltpu.VMEM((CONV_HIST_ROWS + tq, CONV_WIDTH), F32),
                        pltpu.VMEM((tq, MIX_WIDTH), BF16),
                        pltpu.VMEM((kw, all_cols), F32), pltpu.VMEM((kw, all_cols), BF16),
                        pltpu.VMEM((1, all_cols), F32),
                        pltpu.VMEM((WINDOW, KV_WIDTH), F32), pltpu.VMEM((WINDOW, KV_WIDTH), F32),
                        pltpu.VMEM((CONV_HIST_ROWS, CONV_WIDTH), F32)],
        compiler_params=pltpu.CompilerParams(
            dimension_semantics=("arbitrary", "arbitrary"), vmem_limit_bytes=_vmem_limit(est)),
        name="mixer_full",
    )(x, g.reshape(1, D_MODEL), w_in, bias_l, sink_l, conv_w, w_out, *[job.a for job in jobs])


def _mixer_ab(x, q, k, v, k_hist, v_hist, bg, u, u_hist, bias, sinks, conv_w, w_out, e,
              *, nb, t_len):
    n = x.shape[0]
    cl = t_len
    assert t_len <= CHUNK and t_len % BF16_SUBLANES == 0
    kw = WINDOW + cl
    b5, s5 = _head_layout(bias, sinks, cl)
    bias_l = jnp.transpose(b5, (0, 2, 1, 3, 4)).reshape(N_KV_HEADS, 2, PAIRS_PER_KV * cl, kw)
    sink_l = jnp.transpose(s5, (0, 2, 1, 3, 4)).reshape(N_KV_HEADS, 2, PAIRS_PER_KV * cl, 1)
    vmem = pl.BlockSpec(memory_space=pltpu.VMEM)
    return pl.pallas_call(
        functools.partial(_mixer_ab_body, nb=nb, cl=cl),
        out_shape=jax.ShapeDtypeStruct((n, D_MODEL), F32),
        in_specs=[vmem] * 13,
        out_specs=vmem,
        scratch_shapes=[pltpu.VMEM((nb, N_KV_HEADS, 2, kw, KV_WIDTH), BF16),
                        pltpu.VMEM((nb, N_KV_HEADS, 2, kw, KV_WIDTH), BF16),
                        pltpu.VMEM((nb, CONV_HIST_ROWS + cl, CONV_WIDTH), F32),
                        pltpu.VMEM((n, MIX_WIDTH), BF16)],
        name="mixer_ab",
    )(x, q, k, v, k_hist, v_hist, bg, u, u_hist, bias_l, sink_l, conv_w[e], w_out)


def _pool_rows(x, hist, g, pw_ref, sc_ref, o_ref, r0, pos0):
    rows = x.shape[0]
    h = _rms(x, g)
    sums = []
    cur = jnp.concatenate([hist, h], axis=0)
    for gi, w in enumerate(POOL_SIZES):
        assert w == 2 ** (gi + 1)
        cur = cur + pltpu.roll(cur, w // 2, axis=0)
        sums.append(cur[POOL_HIST_ROWS:POOL_HIST_ROWS + rows, 0:POOL_GROUP])
        if gi + 1 < len(POOL_SIZES):
            cur = cur[:, POOL_GROUP:]

    pos = pos0 + lax.broadcasted_iota(jnp.int32, (rows, 1), 0)
    for gi, w in enumerate(POOL_SIZES):
        sl = slice(gi * POOL_GROUP, (gi + 1) * POOL_GROUP)
        cnt = jnp.minimum(pos + 1, w).astype(F32)
        d = (sums[gi] / cnt - h[:, sl]).astype(BF16)
        y = jnp.dot(d, pw_ref[gi], preferred_element_type=F32)
        o_ref[r0:r0 + rows, sl] = x[:, sl] + y * sc_ref[:, sl]
    return h[rows - POOL_HIST_ROWS:rows, :]


def _mixer_c_body(x_ref, hist_ref, g_ref, pw_ref, sc_ref, o_ref, st_ref, *, nb, t_len, first_pos):
    for sq in range(nb):
        r0, h0 = sq * t_len, sq * POOL_HIST_ROWS
        st_ref[h0:h0 + POOL_HIST_ROWS, :] = _pool_rows(
            x_ref[r0:r0 + t_len, :], hist_ref[h0:h0 + POOL_HIST_ROWS, :], g_ref[...],
            pw_ref, sc_ref, o_ref, r0, first_pos)


def _pool_specs(o):
    return [pl.BlockSpec((None, len(POOL_SIZES), POOL_GROUP, POOL_GROUP), lambda *a: (o, 0, 0, 0)),
            pl.BlockSpec((None, 1, D_MODEL), lambda *a: (o, 0, 0))]


def _mixer_c(x, hist, g, pool_w, scale, o, *, nb, t_len, first_pos):
    n = x.shape[0]
    whole = lambda a: pl.BlockSpec(a.shape, lambda i: (0,) * a.ndim)
    g2 = g.reshape(1, D_MODEL)
    st_shape = jax.ShapeDtypeStruct((nb * POOL_HIST_ROWS, D_MODEL), F32)
    return pl.pallas_call(
        functools.partial(_mixer_c_body, nb=nb, t_len=t_len, first_pos=first_pos),
        out_shape=(jax.ShapeDtypeStruct((n, D_MODEL), F32), st_shape),
        grid=(1,),
        in_specs=[whole(x), whole(hist), whole(g2)] + _pool_specs(o),
        out_specs=(whole(x), whole(st_shape)),
        compiler_params=pltpu.CompilerParams(dimension_semantics=("arbitrary",)),
        name="mixer_c",
    )(x, hist, g2, pool_w, scale.reshape(scale.shape[0], 1, D_MODEL))


def _ffn_pool_body(x_ref, g1_ref, wg_ref, wu_ref, wd_ref, g2_ref, pw_ref, sc_ref,
                   o_ref, st_ref, a_ref, carry_ref, *, tm, sub, tiles_per_seq):
    i = pl.program_id(0)
    t = i % tiles_per_seq

    @pl.when(i == 0)
    def _():
        carry_ref[...] = jnp.zeros_like(carry_ref)

    hist = jnp.where(t == 0, 0.0, carry_ref[...])
    g1 = g1_ref[...]
    g2 = g2_ref[...]
    blocks = [_swiglu_rows(x_ref[r0:r0 + sub, :], g1, wg_ref, wu_ref, wd_ref, a_ref, r0)
              for r0 in range(0, tm, sub)]
    for s, xb in enumerate(blocks):
        hist = _pool_rows(xb, hist, g2, pw_ref, sc_ref, o_ref, s * sub, t * tm + s * sub)
    carry_ref[...] = hist
    st_ref[...] = hist


def _ffn_pool(x, g1, w16, g2, pool_w, scale, o, *, nb, t_len, tm, sub):
    n = x.shape[0]
    tiles_per_seq = t_len // tm
    whole = lambda a: pl.BlockSpec(a.shape, lambda i: (0, 0), pipeline_mode=pl.Buffered(1))
    vec = pl.BlockSpec((1, D_MODEL), lambda i: (0, 0))
    row = pl.BlockSpec((tm, D_MODEL), lambda i: (i, 0))
    est = 3 * D_MODEL * D_FF * 2 + 4 * tm * D_MODEL * 4 + tm * D_FF * 2 + 6 * tm * 1024 * 4
    return pl.pallas_call(
        functools.partial(_ffn_pool_body, tm=tm, sub=sub, tiles_per_seq=tiles_per_seq),
        out_shape=(jax.ShapeDtypeStruct((n, D_MODEL), F32),
                   jax.ShapeDtypeStruct((nb * POOL_HIST_ROWS, D_MODEL), F32)),
        grid=(n // tm,),
        in_specs=[row, vec, whole(w16[0]), whole(w16[1]), whole(w16[2]), vec] + _pool_specs(o),
        out_specs=(row, pl.BlockSpec((POOL_HIST_ROWS, D_MODEL), lambda i: (i // tiles_per_seq, 0))),
        scratch_shapes=[pltpu.VMEM((tm, D_FF), BF16), pltpu.VMEM((POOL_HIST_ROWS, D_MODEL), F32)],
        compiler_params=pltpu.CompilerParams(
            dimension_semantics=("arbitrary",), vmem_limit_bytes=_vmem_limit(est)),
        name="ffn_pool",
    )(x, g1.reshape(1, D_MODEL), *w16, g2.reshape(1, D_MODEL), pool_w,
      scale.reshape(scale.shape[0], 1, D_MODEL))


def kernel(x_prompt, x_sample, cache_attn_k, cache_attn_v, cache_conv, cache_pool, norm_g, ffn_w_gate, ffn_w_up, ffn_w_down, mix_w_in, mix_w_out, conv_w, attn_sinks, rel_bias_table, pool_w, pool_scale, final_norm_g):
    past_len = 1024
    nb, t_len, _ = x_prompt.shape
    nbs, ts_len, _ = x_sample.shape
    xp = x_prompt.reshape(nb * t_len, D_MODEL)
    xs = x_sample.reshape(nbs * ts_len, D_MODEL)
    tf = min(FFN_ROW_TILE, xp.shape[0])
    tq = min(MIXER_ROW_TILE, t_len)
    assert t_len % tf == 0
    ts = xs.shape[0]
    pw = pool_w.astype(BF16)
    bias = _relbias(rel_bias_table)
    ffn_w = (ffn_w_gate, ffn_w_up, ffn_w_down)

    w16 = {0: tuple(a[0, 0].astype(BF16) for a in ffn_w)}
    pending = list(range(1, 2 * DEPTH))
    mix16 = {}

    def ffn_weights(n):
        if n not in w16:
            pending.remove(n)
            w16[n] = tuple(a[n // 2, n % 2].astype(BF16) for a in ffn_w)
        return w16[n]

    def next_casts(n):
        if not pending or pending[0] <= n:
            return None, []
        m = pending.pop(0)
        return m, [(a, (m // 2, m % 2)) for a in ffn_w]

    def prompt_ffn(x, layer, j, final):
        n = 2 * layer + j
        m, cast = next_casts(n)
        if n == 0:
            cast = cast + [(a, (e,)) for e in range(mix_w_in.shape[0]) for a in (mix_w_in, mix_w_out)]
        y, *done = _ffn(x, norm_g[layer, 2 * j], ffn_weights(n), final_norm_g, final, tf, cast=cast)
        if m is not None:
            w16[m] = tuple(done[:3])
            done = done[3:]
        for e in range(len(done) // 2):
            mix16[e] = (done[2 * e], done[2 * e + 1])
        return y

    def sample_ffn(x, layer, j, final):
        return _ffn(x, norm_g[layer, 2 * j], ffn_weights(2 * layer + j), final_norm_g, final, ts)[0]

    def last_rows(a, seqs, rows, keep, tail_shape):
        return a.reshape((seqs, rows) + tail_shape)[:, rows - keep:]

    outs = {name: [] for name in ("kp", "vp", "cp", "pp", "ks", "vs", "cs", "ps")}
    for layer in range(DEPTH):
        last = layer == DEPTH - 1
        if layer % 2 == 0:
            e = layer // 2
            xp = prompt_ffn(xp, layer, 0, False)
            xs = sample_ffn(xs, layer, 0, False)
            if e not in mix16:
                mix16[e] = (mix_w_in[e].astype(BF16), mix_w_out[e].astype(BF16))
            w_in, w_out = mix16[e]
            m, cast = next_casts(2 * layer)
            xp, k, v, u, *done = _mixer_full(xp, norm_g[layer, 1], w_in, bias, attn_sinks[e], conv_w,
                                             w_out, e, nb=nb, t_len=t_len, tq=tq, cast=cast)
            if m is not None:
                w16[m] = tuple(done)
            keep = min(WINDOW, t_len)
            outs["kp"].append(last_rows(k, nb, WINDOW, keep, (N_KV_HEADS, HEAD_DIM)))
            outs["vp"].append(last_rows(v, nb, WINDOW, keep, (N_KV_HEADS, HEAD_DIM)))
            outs["cp"].append(last_rows(u, nb, CONV_HIST_ROWS, CONV_K - 1, (CONV_WIDTH,)))
            q, k, v, bg, u = _inproj(xs, norm_g[layer, 1], w_in, min(ROW_TILE, ts))
            k_hist = cache_attn_k[e].reshape(nbs * WINDOW, KV_WIDTH)
            v_hist = cache_attn_v[e].reshape(nbs * WINDOW, KV_WIDTH)
            u_hist = jnp.pad(cache_conv[e], ((0, 0), (CONV_HIST_ROWS - (CONV_K - 1), 0), (0, 0))
                             ).reshape(nbs * CONV_HIST_ROWS, CONV_WIDTH)
            xs = _mixer_ab(xs, q, k, v, k_hist, v_hist, bg, u, u_hist, bias, attn_sinks[e],
                           conv_w, w_out, e, nb=nbs, t_len=ts_len)
            keep = min(WINDOW, ts_len)
            outs["ks"].append(last_rows(k, nbs, ts_len, keep, (N_KV_HEADS, HEAD_DIM)))
            outs["vs"].append(last_rows(v, nbs, ts_len, keep, (N_KV_HEADS, HEAD_DIM)))
            outs["cs"].append(last_rows(u, nbs, ts_len, CONV_K - 1, (CONV_WIDTH,)))
        else:
            o = layer // 2
            xp, st = _ffn_pool(xp, norm_g[layer, 0], ffn_weights(2 * layer), norm_g[layer, 1],
                               pw, pool_scale, o, nb=nb, t_len=t_len, tm=tf, sub=tf // 2)
            outs["pp"].append(last_rows(st, nb, POOL_HIST_ROWS, POOL_MAX - 1, (D_MODEL,)))
            xs = sample_ffn(xs, layer, 0, False)
            hist = jnp.pad(cache_pool[o], ((0, 0), (POOL_HIST_ROWS - (POOL_MAX - 1), 0), (0, 0))
                           ).reshape(nbs * POOL_HIST_ROWS, D_MODEL)
            xs, st = _mixer_c(xs, hist, norm_g[layer, 1], pw, pool_scale, o,
                              nb=nbs, t_len=ts_len, first_pos=past_len)
            outs["ps"].append(last_rows(st, nbs, POOL_HIST_ROWS, POOL_MAX - 1, (D_MODEL,)))
        xp = prompt_ffn(xp, layer, 1, last)
        xs = sample_ffn(xs, layer, 1, last)
    stacked = {name: jnp.stack(v) for name, v in outs.items()}
    return (xp.reshape(nb, t_len, D_MODEL), xs.reshape(nbs, ts_len, D_MODEL),
            stacked["kp"], stacked["vp"], stacked["cp"], stacked["pp"],
            stacked["ks"], stacked["vs"], stacked["cs"], stacked["ps"])
```

```python
import functools
import math

import jax
import jax.numpy as jnp
from jax import lax
from jax.experimental import pallas as pl
from jax.experimental.pallas import tpu as pltpu

D_MODEL = 1024
DEPTH = 2
CHUNK = 64
N_HEADS = 8
N_KV_HEADS = 2
Q_PER_KV = N_HEADS // N_KV_HEADS
HEAD_DIM = 64
ATTN_WIDTH = N_HEADS * HEAD_DIM
KV_WIDTH = N_KV_HEADS * HEAD_DIM
WINDOW = 128
N_BUCKETS = 32
MAX_DISTANCE = 128
CONV_WIDTH = D_MODEL // 2
CONV_K = 3
IN_WIDTH = ATTN_WIDTH + 2 * KV_WIDTH + 3 * CONV_WIDTH
MIX_WIDTH = ATTN_WIDTH + CONV_WIDTH
POOL_SIZES = (2, 4, 8, 16)
POOL_GROUP = D_MODEL // len(POOL_SIZES)
POOL_MAX = max(POOL_SIZES)
D_FF = 2816
EPS = 1e-6
NEG = -1e30

V7X_VMEM_BYTES = 64 * 1024 * 1024
SUBLANES_F32 = 8
LANES = 128
CONV_HIST_ROWS = SUBLANES_F32
POOL_HIST_ROWS = 16
ROW_TILE = 512
MIXER_ROW_TILE = 1024
FFN_ROW_TILE = 1024
FF_CHUNK = 512
FF_CHUNKS = tuple((c0, min(FF_CHUNK, D_FF - c0)) for c0 in range(0, D_FF, FF_CHUNK))

F32 = jnp.float32
BF16 = jnp.bfloat16


def _vmem_limit(nbytes):
    return int(min(V7X_VMEM_BYTES - (4 << 20), max(32 << 20, nbytes)))


def _rms(x, g):
    return x * lax.rsqrt(jnp.mean(x * x, axis=-1, keepdims=True) + EPS) * g


def _relbias_body(tab_ref, bkt_ref, o_ref):
    bkt = bkt_ref[...]
    for h in range(N_HEADS):
        acc = jnp.zeros(bkt.shape, F32)
        for b in range(N_BUCKETS):
            acc = jnp.where(bkt == b, tab_ref[b, h], acc)
        o_ref[h] = acc


def _rel_bucket(rel):
    half = N_BUCKETS // 2
    ret = jnp.where(rel > 0, half, 0)
    n = jnp.abs(rel)
    max_exact = half // 2
    nf = jnp.maximum(n, 1).astype(F32)
    large = max_exact + (jnp.log(nf / max_exact) / math.log(MAX_DISTANCE / max_exact)
                         * (half - max_exact)).astype(jnp.int32)
    large = jnp.minimum(large, half - 1)
    return ret + jnp.where(n < max_exact, n, large)


def _relbias(table):
    rel = (jnp.arange(WINDOW + CHUNK, dtype=jnp.int32) - WINDOW)[None, :] \
        - jnp.arange(CHUNK, dtype=jnp.int32)[:, None]
    bucket = _rel_bucket(rel).astype(jnp.int32)
    return pl.pallas_call(
        _relbias_body,
        out_shape=jax.ShapeDtypeStruct((N_HEADS, CHUNK, WINDOW + CHUNK), F32),
        in_specs=[pl.BlockSpec(memory_space=pltpu.SMEM),
                  pl.BlockSpec(memory_space=pltpu.VMEM)],
        out_specs=pl.BlockSpec(memory_space=pltpu.VMEM),
        name="relbias",
    )(table, bucket)


def _swiglu_rows(x, g, wg_ref, wu_ref, wd_ref, a_ref, r0):
    rows = x.shape[0]
    xn = _rms(x, g).astype(BF16)
    for c0, cw in FF_CHUNKS:
        gate = jnp.dot(xn, wg_ref[:, c0:c0 + cw], preferred_element_type=F32)
        up = jnp.dot(xn, wu_ref[:, c0:c0 + cw], preferred_element_type=F32)
        a_ref[r0:r0 + rows, c0:c0 + cw] = (jax.nn.silu(gate) * up).astype(BF16)
    y = jnp.dot(a_ref[r0:r0 + rows, :], wd_ref[...], preferred_element_type=F32)
    return x + 0.5 * y


BF16_SUBLANES = 16


class _CastJob:
    def __init__(self, a, lead, steps):
        self.a, self.lead = a, tuple(lead)
        rows, cols = a.shape[-2:]
        self.chunk = next(c for c in range(BF16_SUBLANES, rows + 1, BF16_SUBLANES)
                          if rows % c == 0 and steps % (rows // c) == 0)
        self.repeat = steps // (rows // self.chunk)
        self.out_shape = jax.ShapeDtypeStruct((rows, cols), BF16)

    def specs(self, step_of):
        none = (None,) * len(self.lead)
        cols = self.a.shape[-1]
        return (pl.BlockSpec(none + (self.chunk, cols),
                             lambda *g: self.lead + (step_of(*g) // self.repeat, 0)),
                pl.BlockSpec((self.chunk, cols), lambda *g: (step_of(*g) // self.repeat, 0)))


def _host_casts(body, n_in, n_out, jobs):
    k = len(jobs)

    def hosted(*refs):
        ins, cast_in = refs[:n_in], refs[n_in:n_in + k]
        outs, cast_out = refs[n_in + k:n_in + k + n_out], refs[n_in + k + n_out:n_in + 2 * k + n_out]
        for src, dst in zip(cast_in, cast_out):
            dst[...] = src[...].astype(BF16)
        body(*ins, *outs, *refs[n_in + 2 * k + n_out:])
    return hosted


def _ffn_body(x_ref, g_ref, wg_ref, wu_ref, wd_ref, fg_ref, o_ref, a_ref, *, final_norm):
    out = _swiglu_rows(x_ref[...], g_ref[...], wg_ref, wu_ref, wd_ref, a_ref, 0)
    if final_norm:
        out = _rms(out, fg_ref[...])
    o_ref[...] = out


def _ffn(x, g, w16, fg, final_norm, tm, cast=()):
    n = x.shape[0]
    steps = n // tm
    jobs = [_CastJob(a, lead, steps) for a, lead in cast]
    job_specs = [job.specs(lambda i: i) for job in jobs]
    whole = lambda a: pl.BlockSpec(a.shape, lambda i: (0, 0), pipeline_mode=pl.Buffered(1))
    vec = pl.BlockSpec((1, D_MODEL), lambda i: (0, 0))
    row = pl.BlockSpec((tm, D_MODEL), lambda i: (i, 0))
    est = 3 * D_MODEL * D_FF * 2 + 4 * tm * D_MODEL * 4 + tm * D_FF * 2 + 5 * tm * 1024 * 4 \
        + sum(6 * job.chunk * job.a.shape[-1] * 2 for job in jobs)
    return pl.pallas_call(
        _host_casts(functools.partial(_ffn_body, final_norm=final_norm), 6, 1, jobs),
        out_shape=(jax.ShapeDtypeStruct((n, D_MODEL), F32),) + tuple(job.out_shape for job in jobs),
        grid=(steps,),
        in_specs=[row, vec, whole(w16[0]), whole(w16[1]), whole(w16[2]), vec] + [s[0] for s in job_specs],
        out_specs=(row,) + tuple(s[1] for s in job_specs),
        scratch_shapes=[pltpu.VMEM((tm, D_FF), BF16)],
        compiler_params=pltpu.CompilerParams(
            dimension_semantics=("arbitrary",), vmem_limit_bytes=_vmem_limit(est)),
        name="ffn",
    )(x, g.reshape(1, D_MODEL), *w16, fg.reshape(1, D_MODEL), *[job.a for job in jobs])


FF_STREAM_CHUNK = D_FF // 2
assert FF_STREAM_CHUNK % LANES == 0


def _ffn_stream_body(x_ref, g_ref, wg_ref, wu_ref, wd_ref, fg_ref, o_ref, xn_ref, *, final_norm):
    k = pl.program_id(0)

    @pl.when(k == 0)
    def _():
        x = x_ref[...]
        xn_ref[...] = _rms(x, g_ref[...]).astype(BF16)
        o_ref[...] = x

    xn = xn_ref[...]
    gate = jnp.dot(xn, wg_ref[...], preferred_element_type=F32)
    up = jnp.dot(xn, wu_ref[...], preferred_element_type=F32)
    a = (jax.nn.silu(gate) * up).astype(BF16)
    o_ref[...] += 0.5 * jnp.dot(a, wd_ref[...], preferred_element_type=F32)

    if final_norm:
        @pl.when(k == pl.num_programs(0) - 1)
        def _():
            o_ref[...] = _rms(o_ref[...], fg_ref[...])


def _ffn_stream(x, g, w16, fg, final_norm):
    n = x.shape[0]
    fc = FF_STREAM_CHUNK
    whole = pl.BlockSpec((n, D_MODEL), lambda k: (0, 0))
    vec = pl.BlockSpec((1, D_MODEL), lambda k: (0, 0))
    return pl.pallas_call(
        functools.partial(_ffn_stream_body, final_norm=final_norm),
        out_shape=jax.ShapeDtypeStruct((n, D_MODEL), F32),
        grid=(D_FF // fc,),
        in_specs=[whole, vec, pl.BlockSpec((D_MODEL, fc), lambda k: (0, k)),
                  pl.BlockSpec((D_MODEL, fc), lambda k: (0, k)),
                  pl.BlockSpec((fc, D_MODEL), lambda k: (k, 0)), vec],
        out_specs=whole,
        scratch_shapes=[pltpu.VMEM((n, D_MODEL), BF16)],
        compiler_params=pltpu.CompilerParams(dimension_semantics=("arbitrary",)),
        name="ffn_stream",
    )(x, g.reshape(1, D_MODEL), *w16, fg.reshape(1, D_MODEL))


def _inproj_body(x_ref, g_ref, w_ref, q_ref, k_ref, v_ref, bg_ref, u_ref):
    h = _rms(x_ref[...], g_ref[...]).astype(BF16)
    p = jnp.dot(h, w_ref[...], preferred_element_type=F32)
    o = 0
    q_ref[...] = (p[:, o:o + ATTN_WIDTH] * (HEAD_DIM ** -0.5)).astype(BF16)
    o += ATTN_WIDTH
    k_ref[...] = p[:, o:o + KV_WIDTH]
    o += KV_WIDTH
    v_ref[...] = p[:, o:o + KV_WIDTH]
    o += KV_WIDTH
    bg_ref[...] = p[:, o:o + CONV_WIDTH]
    o += CONV_WIDTH
    u_ref[...] = p[:, o:o + CONV_WIDTH] * p[:, o + CONV_WIDTH:o + 2 * CONV_WIDTH]


def _inproj(x, g, w_in, tm):
    n = x.shape[0]
    row = lambda w: pl.BlockSpec((tm, w), lambda i: (i, 0))
    est = 2 * D_MODEL * IN_WIDTH * 2 + 2 * tm * (D_MODEL + IN_WIDTH) * 4 + 2 * tm * IN_WIDTH * 4
    return pl.pallas_call(
        _inproj_body,
        out_shape=(jax.ShapeDtypeStruct((n, ATTN_WIDTH), BF16),
                   jax.ShapeDtypeStruct((n, KV_WIDTH), F32),
                   jax.ShapeDtypeStruct((n, KV_WIDTH), F32),
                   jax.ShapeDtypeStruct((n, CONV_WIDTH), F32),
                   jax.ShapeDtypeStruct((n, CONV_WIDTH), F32)),
        grid=(n // tm,),
        in_specs=[row(D_MODEL), pl.BlockSpec((1, D_MODEL), lambda i: (0, 0)),
                  pl.BlockSpec((D_MODEL, IN_WIDTH), lambda i: (0, 0))],
        out_specs=(row(ATTN_WIDTH), row(KV_WIDTH), row(KV_WIDTH), row(CONV_WIDTH), row(CONV_WIDTH)),
        compiler_params=pltpu.CompilerParams(
            dimension_semantics=("arbitrary",), vmem_limit_bytes=_vmem_limit(est)),
        name="inproj",
    )(x, g.reshape(1, D_MODEL), w_in)


PAIR_WIDTH = 2 * HEAD_DIM
PAIRS_PER_KV = Q_PER_KV // 2
ATTN_SCALE = HEAD_DIM ** -0.5


def _mixer_ab_body(x_ref, q_ref, k_ref, v_ref, kh_ref, vh_ref, bg_ref, u_ref, uh_ref,
                   bias_ref, sink_ref, cw_ref, wo_ref, o_ref,
                   kbuf, vbuf, ubuf, mix, *, nb, cl):
    kw = WINDOW + cl
    low = lax.broadcasted_iota(jnp.int32, (1, KV_WIDTH), 1) < HEAD_DIM

    def stage(buf, a, r0):
        b = pltpu.roll(a, HEAD_DIM, axis=1)
        rows = a.shape[0]
        buf[0, 0, r0:r0 + rows, :] = jnp.where(low, a, 0.0).astype(BF16)
        buf[0, 1, r0:r0 + rows, :] = jnp.where(low, 0.0, b).astype(BF16)
        buf[1, 0, r0:r0 + rows, :] = jnp.where(low, b, 0.0).astype(BF16)
        buf[1, 1, r0:r0 + rows, :] = jnp.where(low, 0.0, a).astype(BF16)

    for sq in range(nb):
        r0, h0 = sq * cl, sq * WINDOW
        kb, vb = kbuf.at[sq], vbuf.at[sq]
        stage(kb, kh_ref[h0:h0 + WINDOW, :], 0)
        stage(kb, k_ref[r0:r0 + cl, :], WINDOW)
        stage(vb, vh_ref[h0:h0 + WINDOW, :], 0)
        stage(vb, v_ref[r0:r0 + cl, :], WINDOW)
        for hh in range(N_KV_HEADS):
            ql = jnp.concatenate(
                [q_ref[r0:r0 + cl, (hh * PAIRS_PER_KV + pr) * PAIR_WIDTH:
                       (hh * PAIRS_PER_KV + pr + 1) * PAIR_WIDTH] for pr in range(PAIRS_PER_KV)],
                axis=0)
            acc = None
            for lh in range(2):
                s = lax.dot_general(ql, kb[hh, lh], (((1,), (1,)), ((), ())),
                                    preferred_element_type=F32) + bias_ref[hh, lh]
                sk = sink_ref[hh, lh]
                m = jnp.maximum(jnp.max(s, axis=-1, keepdims=True), sk)
                p = jnp.exp(s - m)
                den = jnp.sum(p, axis=-1, keepdims=True) + jnp.exp(sk - m)
                o = jnp.dot(p.astype(BF16), vb[hh, lh], preferred_element_type=F32) * (1.0 / den)
                acc = o if acc is None else acc + o
            for pr in range(PAIRS_PER_KV):
                c0 = (hh * PAIRS_PER_KV + pr) * PAIR_WIDTH
                mix[r0:r0 + cl, c0:c0 + PAIR_WIDTH] = acc[pr * cl:(pr + 1) * cl].astype(BF16)
        h8 = sq * CONV_HIST_ROWS
        conv = _gated_conv(bg_ref[r0:r0 + cl, :], u_ref[r0:r0 + cl, :],
                           uh_ref[h8:h8 + CONV_HIST_ROWS, :], cw_ref, ubuf.at[sq])
        mix[r0:r0 + cl, ATTN_WIDTH:MIX_WIDTH] = conv.astype(BF16)

    o_ref[...] = x_ref[...] + jnp.dot(mix[...], wo_ref[...], preferred_element_type=F32)


def _gated_conv(bg, u, uh, cw_ref, ubuf):
    rows = u.shape[0]
    ubuf[0:CONV_HIST_ROWS, :] = uh
    ubuf[CONV_HIST_ROWS:CONV_HIST_ROWS + rows, :] = u
    conv = None
    for j in range(CONV_K):
        off = CONV_HIST_ROWS - (CONV_K - 1) + j
        term = cw_ref[j:j + 1, :] * ubuf[off:off + rows, :]
        conv = term if conv is None else conv + term
    return bg * conv


def _mixer_full_body(x_ref, g_ref, win_ref, biast_ref, sinkt_ref, cw_ref, wo_ref,
                     o_ref, klast_ref, vlast_ref, ulast_ref,
                     qbuf, kdup, vta, vtb, ubuf, mix, sbuf, pbuf, rbuf, kcar, vcar, ucar, *, tq):
    b = pl.program_id(0)
    t = pl.program_id(1)
    cl = CHUNK
    kw = WINDOW + cl
    w = WINDOW + tq
    low = lax.broadcasted_iota(jnp.int32, (1, KV_WIDTH), 1) < HEAD_DIM

    @pl.when((b == 0) & (t == 0))
    def _():
        kcar[...] = jnp.zeros_like(kcar)
        vcar[...] = jnp.zeros_like(vcar)
        ucar[...] = jnp.zeros_like(ucar)

    x = x_ref[...]
    h = _rms(x, g_ref[...]).astype(BF16)

    def proj(c0, width):
        return jnp.dot(h, win_ref[:, c0:c0 + width], preferred_element_type=F32)

    qbuf[...] = (proj(0, ATTN_WIDTH) * ATTN_SCALE).astype(BF16)
    kv = proj(ATTN_WIDTH, 2 * KV_WIDTH)
    k = kv[:, 0:KV_WIDTH]
    v = kv[:, KV_WIDTH:2 * KV_WIDTH]
    c0 = ATTN_WIDTH + 2 * KV_WIDTH
    bg = proj(c0, CONV_WIDTH)
    u = proj(c0 + CONV_WIDTH, CONV_WIDTH) * proj(c0 + 2 * CONV_WIDTH, CONV_WIDTH)

    kh = kcar[...]
    vh = vcar[...]
    uh = jnp.where(t == 0, 0.0, ucar[...])
    k_tail = k[tq - WINDOW:tq, :]
    v_tail = v[tq - WINDOW:tq, :]
    u_tail = u[tq - CONV_HIST_ROWS:tq, :]
    kcar[...] = k_tail
    vcar[...] = v_tail
    ucar[...] = u_tail
    klast_ref[...] = k_tail
    vlast_ref[...] = v_tail
    ulast_ref[...] = u_tail

    kf = jnp.concatenate([kh, k], axis=0)
    kr = pltpu.roll(kf, HEAD_DIM, axis=1)
    kdup[0] = jnp.where(low, kf, kr).astype(BF16)
    kdup[1] = jnp.where(low, kr, kf).astype(BF16)

    vf = jnp.concatenate([vh, v, jnp.zeros((cl, KV_WIDTH), F32)], axis=0)
    vr = pltpu.roll(vf, HEAD_DIM, axis=1)
    for hh, vd in enumerate((jnp.where(low, vf, vr), jnp.where(low, vr, vf))):
        for blk in range(w // KV_WIDTH):
            b0 = blk * KV_WIDTH
            vta[hh, :, b0:b0 + KV_WIDTH] = vd[b0:b0 + KV_WIDTH].T.astype(BF16)
            vtb[hh, :, b0:b0 + KV_WIDTH] = vd[cl + b0:cl + b0 + KV_WIDTH].T.astype(BF16)

    n_cols = PAIRS_PER_KV * PAIR_WIDTH
    key = lax.broadcasted_iota(jnp.int32, (kw, n_cols), 0)
    rr = lax.broadcasted_iota(jnp.int32, (PAIR_WIDTH, PAIR_WIDTH), 0) < HEAD_DIM
    cc = lax.broadcasted_iota(jnp.int32, (PAIR_WIDTH, PAIR_WIDTH), 1) < HEAD_DIM
    diag = rr == cc
    for c in range(tq // cl):
        r0 = c * cl
        for hh in range(N_KV_HEADS):
            rows = []
            for pr in range(PAIRS_PER_KV):
                c0 = (hh * PAIRS_PER_KV + pr) * PAIR_WIDTH
                qp = qbuf[r0:r0 + cl, c0:c0 + PAIR_WIDTH]
                rows += [jnp.where(low, qp, 0), jnp.where(low, 0, qp)]
            qm = jnp.concatenate(rows, axis=0)
            s = lax.dot_general(kdup[hh, r0:r0 + kw, :], qm, (((1,), (1,)), ((), ())),
                                preferred_element_type=F32) + biast_ref[hh]
            if r0 < WINDOW:
                s = jnp.where(key + (t * tq + r0 - WINDOW) >= 0, s, NEG)
            j0 = (c * N_KV_HEADS + hh) * n_cols
            sbuf[:, j0:j0 + n_cols] = s

    s = sbuf[...]
    sk = sinkt_ref[...]
    m = jnp.maximum(jnp.max(s, axis=0, keepdims=True), sk)
    p = jnp.exp(s - m)
    den = jnp.sum(p, axis=0, keepdims=True) + jnp.exp(sk - m)
    pbuf[...] = p.astype(BF16)
    rbuf[...] = 1.0 / den

    for c in range(tq // cl):
        r0 = c * cl
        k0 = r0 if c % 2 == 0 else r0 - cl
        vt_ref = vta if c % 2 == 0 else vtb
        for hh in range(N_KV_HEADS):
            j0 = (c * N_KV_HEADS + hh) * n_cols
            ot = jnp.dot(vt_ref[hh, :, k0:k0 + kw], pbuf[:, j0:j0 + n_cols],
                         preferred_element_type=F32) * rbuf[:, j0:j0 + n_cols]
            for pr in range(PAIRS_PER_KV):
                blk = jnp.where(diag, ot[:, pr * PAIR_WIDTH:(pr + 1) * PAIR_WIDTH], 0.0).T
                c0 = (hh * PAIRS_PER_KV + pr) * PAIR_WIDTH
                mix[r0:r0 + cl, c0:c0 + PAIR_WIDTH] = (blk[0:cl] + blk[cl:2 * cl]).astype(BF16)

    mix[:, ATTN_WIDTH:MIX_WIDTH] = _gated_conv(bg, u, uh, cw_ref, ubuf).astype(BF16)
    o_ref[...] = x + jnp.dot(mix[...], wo_ref[...], preferred_element_type=F32)


def _head_layout(bias, sinks, cl):
    kw = WINDOW + cl
    b5 = bias[:, :cl, :kw].reshape(N_KV_HEADS, PAIRS_PER_KV, 2, cl, kw)
    s5 = jnp.broadcast_to(sinks.reshape(N_KV_HEADS, PAIRS_PER_KV, 2, 1, 1),
                          (N_KV_HEADS, PAIRS_PER_KV, 2, cl, 1))
    return b5, s5


def _mixer_full(x, g, w_in, bias, sinks, conv_w, w_out, e, *, nb, t_len, tq, cast=()):
    n = x.shape[0]
    nt = t_len // tq
    jobs = [_CastJob(a, lead, nb * nt) for a, lead in cast]
    job_specs = [job.specs(lambda b, t: b * nt + t) for job in jobs]
    cl = CHUNK
    kw = WINDOW + cl
    assert tq % KV_WIDTH == 0 and tq >= WINDOW
    b5, s5 = _head_layout(bias, sinks, cl)
    n_cols = PAIRS_PER_KV * PAIR_WIDTH
    all_cols = (tq // cl) * N_KV_HEADS * n_cols
    bias_l = jnp.transpose(b5, (0, 4, 1, 2, 3)).reshape(N_KV_HEADS, kw, n_cols)
    sink_l = jnp.tile(s5.reshape(1, N_KV_HEADS * n_cols), (1, tq // cl))
    row = pl.BlockSpec((tq, D_MODEL), lambda b, t: (b * nt + t, 0))
    const = lambda a: pl.BlockSpec(a.shape, lambda b, t: (0,) * a.ndim)
    per_seq = lambda rows, width: pl.BlockSpec((rows, width), lambda b, t: (b, 0))
    est = 4 * tq * D_MODEL * 4 + (D_MODEL * IN_WIDTH + MIX_WIDTH * D_MODEL) * 2 \
        + kw * all_cols * 6 + 16 * tq * D_MODEL * 4
    whole = lambda a: pl.BlockSpec(a.shape, lambda b, t: (0, 0), pipeline_mode=pl.Buffered(1))
    return pl.pallas_call(
        _host_casts(functools.partial(_mixer_full_body, tq=tq), 7, 4, jobs),
        out_shape=(jax.ShapeDtypeStruct((n, D_MODEL), F32),
                   jax.ShapeDtypeStruct((nb * WINDOW, KV_WIDTH), F32),
                   jax.ShapeDtypeStruct((nb * WINDOW, KV_WIDTH), F32),
                   jax.ShapeDtypeStruct((nb * CONV_HIST_ROWS, CONV_WIDTH), F32))
        + tuple(job.out_shape for job in jobs),
        grid=(nb, nt),
        in_specs=[row, pl.BlockSpec((1, D_MODEL), lambda b, t: (0, 0)), whole(w_in),
                  const(bias_l), const(sink_l),
                  pl.BlockSpec((None, CONV_K, CONV_WIDTH), lambda b, t: (e, 0, 0)),
                  whole(w_out)] + [s[0] for s in job_specs],
        out_specs=(row, per_seq(WINDOW, KV_WIDTH), per_seq(WINDOW, KV_WIDTH),
                   per_seq(CONV_HIST_ROWS, CONV_WIDTH)) + tuple(s[1] for s in job_specs),
        scratch_shapes=[pltpu.VMEM((tq, ATTN_WIDTH), BF16),
                        pltpu.VMEM((N_KV_HEADS, WINDOW + tq, KV_WIDTH), BF16),
                        pltpu.VMEM((N_KV_HEADS, PAIR_WIDTH, WINDOW + tq), BF16),
                        pltpu.VMEM((N_KV_HEADS, PAIR_WIDTH, WINDOW + tq), BF16),
                        pltpu.VMEM((CONV_HIST_ROWS + tq, CONV_WIDTH), F32),
                        pltpu.VMEM((tq, MIX_WIDTH), BF16),
                        pltpu.VMEM((kw, all_cols), F32), pltpu.VMEM((kw, all_cols), BF16),
                        pltpu.VMEM((1, all_cols), F32),
                        pltpu.VMEM((WINDOW, KV_WIDTH), F32), pltpu.VMEM((WINDOW, KV_WIDTH), F32),
                        pltpu.VMEM((CONV_HIST_ROWS, CONV_WIDTH), F32)],
        compiler_params=pltpu.CompilerParams(
            dimension_semantics=("arbitrary", "arbitrary"), vmem_limit_bytes=_vmem_limit(est)),
        name="mixer_full",
    )(x, g.reshape(1, D_MODEL), w_in, bias_l, sink_l, conv_w, w_out, *[job.a for job in jobs])


def _mixer_ab(x, q, k, v, k_hist, v_hist, bg, u, u_hist, bias, sinks, conv_w, w_out, e,
              *, nb, t_len):
    n = x.shape[0]
    cl = t_len
    assert t_len <= CHUNK and t_len % BF16_SUBLANES == 0
    kw = WINDOW + cl
    b5, s5 = _head_layout(bias, sinks, cl)
    bias_l = jnp.transpose(b5, (0, 2, 1, 3, 4)).reshape(N_KV_HEADS, 2, PAIRS_PER_KV * cl, kw)
    sink_l = jnp.transpose(s5, (0, 2, 1, 3, 4)).reshape(N_KV_HEADS, 2, PAIRS_PER_KV * cl, 1)
    vmem = pl.BlockSpec(memory_space=pltpu.VMEM)
    return pl.pallas_call(
        functools.partial(_mixer_ab_body, nb=nb, cl=cl),
        out_shape=jax.ShapeDtypeStruct((n, D_MODEL), F32),
        in_specs=[vmem] * 13,
        out_specs=vmem,
        scratch_shapes=[pltpu.VMEM((nb, N_KV_HEADS, 2, kw, KV_WIDTH), BF16),
                        pltpu.VMEM((nb, N_KV_HEADS, 2, kw, KV_WIDTH), BF16),
                        pltpu.VMEM((nb, CONV_HIST_ROWS + cl, CONV_WIDTH), F32),
                        pltpu.VMEM((n, MIX_WIDTH), BF16)],
        name="mixer_ab",
    )(x, q, k, v, k_hist, v_hist, bg, u, u_hist, bias_l, sink_l, conv_w[e], w_out)


def _pool_rows(x, hist, g, pw_ref, sc_ref, o_ref, r0, pos0):
    rows = x.shape[0]
    h = _rms(x, g)
    sums = []
    cur = jnp.concatenate([hist, h], axis=0)
    for gi, w in enumerate(POOL_SIZES):
        assert w == 2 ** (gi + 1)
        cur = cur + pltpu.roll(cur, w // 2, axis=0)
        sums.append(cur[POOL_HIST_ROWS:POOL_HIST_ROWS + rows, 0:POOL_GROUP])
        if gi + 1 < len(POOL_SIZES):
            cur = cur[:, POOL_GROUP:]

    pos = pos0 + lax.broadcasted_iota(jnp.int32, (rows, 1), 0)
    for gi, w in enumerate(POOL_SIZES):
        sl = slice(gi * POOL_GROUP, (gi + 1) * POOL_GROUP)
        cnt = jnp.minimum(pos + 1, w).astype(F32)
        d = (sums[gi] / cnt - h[:, sl]).astype(BF16)
        y = jnp.dot(d, pw_ref[gi], preferred_element_type=F32)
        o_ref[r0:r0 + rows, sl] = x[:, sl] + y * sc_ref[:, sl]
    return h[rows - POOL_HIST_ROWS:rows, :]


def _mixer_c_body(x_ref, hist_ref, g_ref, pw_ref, sc_ref, o_ref, st_ref, *, nb, t_len, first_pos):
    for sq in range(nb):
        r0, h0 = sq * t_len, sq * POOL_HIST_ROWS
        st_ref[h0:h0 + POOL_HIST_ROWS, :] = _pool_rows(
            x_ref[r0:r0 + t_len, :], hist_ref[h0:h0 + POOL_HIST_ROWS, :], g_ref[...],
            pw_ref, sc_ref, o_ref, r0, first_pos)


def _pool_specs(o):
    return [pl.BlockSpec((None, len(POOL_SIZES), POOL_GROUP, POOL_GROUP), lambda *a: (o, 0, 0, 0)),
            pl.BlockSpec((None, 1, D_MODEL), lambda *a: (o, 0, 0))]


def _mixer_c(x, hist, g, pool_w, scale, o, *, nb, t_len, first_pos):
    n = x.shape[0]
    whole = lambda a: pl.BlockSpec(a.shape, lambda i: (0,) * a.ndim)
    g2 = g.reshape(1, D_MODEL)
    st_shape = jax.ShapeDtypeStruct((nb * POOL_HIST_ROWS, D_MODEL), F32)
    return pl.pallas_call(
        functools.partial(_mixer_c_body, nb=nb, t_len=t_len, first_pos=first_pos),
        out_shape=(jax.ShapeDtypeStruct((n, D_MODEL), F32), st_shape),
        grid=(1,),
        in_specs=[whole(x), whole(hist), whole(g2)] + _pool_specs(o),
        out_specs=(whole(x), whole(st_shape)),
        compiler_params=pltpu.CompilerParams(dimension_semantics=("arbitrary",)),
        name="mixer_c",
    )(x, hist, g2, pool_w, scale.reshape(scale.shape[0], 1, D_MODEL))


def _ffn_pool_body(x_ref, g1_ref, wg_ref, wu_ref, wd_ref, g2_ref, pw_ref, sc_ref,
                   o_ref, st_ref, a_ref, carry_ref, *, tm, sub, tiles_per_seq):
    i = pl.program_id(0)
    t = i % tiles_per_seq

    @pl.when(i == 0)
    def _():
        carry_ref[...] = jnp.zeros_like(carry_ref)

    hist = jnp.where(t == 0, 0.0, carry_ref[...])
    g1 = g1_ref[...]
    g2 = g2_ref[...]
    blocks = [_swiglu_rows(x_ref[r0:r0 + sub, :], g1, wg_ref, wu_ref, wd_ref, a_ref, r0)
              for r0 in range(0, tm, sub)]
    for s, xb in enumerate(blocks):
        hist = _pool_rows(xb, hist, g2, pw_ref, sc_ref, o_ref, s * sub, t * tm + s * sub)
    carry_ref[...] = hist
    st_ref[...] = hist


def _ffn_pool(x, g1, w16, g2, pool_w, scale, o, *, nb, t_len, tm, sub):
    n = x.shape[0]
    tiles_per_seq = t_len // tm
    whole = lambda a: pl.BlockSpec(a.shape, lambda i: (0, 0), pipeline_mode=pl.Buffered(1))
    vec = pl.BlockSpec((1, D_MODEL), lambda i: (0, 0))
    row = pl.BlockSpec((tm, D_MODEL), lambda i: (i, 0))
    est = 3 * D_MODEL * D_FF * 2 + 4 * tm * D_MODEL * 4 + tm * D_FF * 2 + 6 * tm * 1024 * 4
    return pl.pallas_call(
        functools.partial(_ffn_pool_body, tm=tm, sub=sub, tiles_per_seq=tiles_per_seq),
        out_shape=(jax.ShapeDtypeStruct((n, D_MODEL), F32),
                   jax.ShapeDtypeStruct((nb * POOL_HIST_ROWS, D_MODEL), F32)),
        grid=(n // tm,),
        in_specs=[row, vec, whole(w16[0]), whole(w16[1]), whole(w16[2]), vec] + _pool_specs(o),
        out_specs=(row, pl.BlockSpec((POOL_HIST_ROWS, D_MODEL), lambda i: (i // tiles_per_seq, 0))),
        scratch_shapes=[pltpu.VMEM((tm, D_FF), BF16), pltpu.VMEM((POOL_HIST_ROWS, D_MODEL), F32)],
        compiler_params=pltpu.CompilerParams(
            dimension_semantics=("arbitrary",), vmem_limit_bytes=_vmem_limit(est)),
        name="ffn_pool",
    )(x, g1.reshape(1, D_MODEL), *w16, g2.reshape(1, D_MODEL), pool_w,
      scale.reshape(scale.shape[0], 1, D_MODEL))


def kernel(x_prompt, x_sample, cache_attn_k, cache_attn_v, cache_conv, cache_pool, norm_g, ffn_w_gate, ffn_w_up, ffn_w_down, mix_w_in, mix_w_out, conv_w, attn_sinks, rel_bias_table, pool_w, pool_scale, final_norm_g):
    past_len = 1024
    nb, t_len, _ = x_prompt.shape
    nbs, ts_len, _ = x_sample.shape
    xp = x_prompt.reshape(nb * t_len, D_MODEL)
    xs = x_sample.reshape(nbs * ts_len, D_MODEL)
    tf = min(FFN_ROW_TILE, xp.shape[0])
    tq = min(MIXER_ROW_TILE, t_len)
    assert t_len % tf == 0
    ts = xs.shape[0]
    pw = pool_w.astype(BF16)
    bias = _relbias(rel_bias_table)
    ffn_w = (ffn_w_gate, ffn_w_up, ffn_w_down)

    w16 = {0: tuple(a[0, 0].astype(BF16) for a in ffn_w)}
    pending = list(range(1, 2 * DEPTH))
    mix16 = {}

    def ffn_weights(n):
        if n not in w16:
            pending.remove(n)
            w16[n] = tuple(a[n // 2, n % 2].astype(BF16) for a in ffn_w)
        return w16[n]

    def next_casts(n):
        if not pending or pending[0] <= n:
            return None, []
        m = pending.pop(0)
        return m, [(a, (m // 2, m % 2)) for a in ffn_w]

    def prompt_ffn(x, layer, j, final):
        n = 2 * layer + j
        m, cast = next_casts(n)
        if n == 0:
            cast = cast + [(a, (e,)) for e in range(mix_w_in.shape[0]) for a in (mix_w_in, mix_w_out)]
        y, *done = _ffn(x, norm_g[layer, 2 * j], ffn_weights(n), final_norm_g, final, tf, cast=cast)
        if m is not None:
            w16[m] = tuple(done[:3])
            done = done[3:]
        for e in range(len(done) // 2):
            mix16[e] = (done[2 * e], done[2 * e + 1])
        return y

    def sample_ffn(x, layer, j, final):
        return _ffn_stream(x, norm_g[layer, 2 * j], ffn_weights(2 * layer + j), final_norm_g, final)

    def last_rows(a, seqs, rows, keep, tail_shape):
        return a.reshape((seqs, rows) + tail_shape)[:, rows - keep:]

    outs = {name: [] for name in ("kp", "vp", "cp", "pp", "ks", "vs", "cs", "ps")}
    for layer in range(DEPTH):
        last = layer == DEPTH - 1
        if layer % 2 == 0:
            e = layer // 2
            xp = prompt_ffn(xp, layer, 0, False)
            xs = sample_ffn(xs, layer, 0, False)
            if e not in mix16:
                mix16[e] = (mix_w_in[e].astype(BF16), mix_w_out[e].astype(BF16))
            w_in, w_out = mix16[e]
            m, cast = next_casts(2 * layer)
            xp, k, v, u, *done = _mixer_full(xp, norm_g[layer, 1], w_in, bias, attn_sinks[e], conv_w,
                                             w_out, e, nb=nb, t_len=t_len, tq=tq, cast=cast)
            if m is not None:
                w16[m] = tuple(done)
            keep = min(WINDOW, t_len)
            outs["kp"].append(last_rows(k, nb, WINDOW, keep, (N_KV_HEADS, HEAD_DIM)))
            outs["vp"].append(last_rows(v, nb, WINDOW, keep, (N_KV_HEADS, HEAD_DIM)))
            outs["cp"].append(last_rows(u, nb, CONV_HIST_ROWS, CONV_K - 1, (CONV_WIDTH,)))
            q, k, v, bg, u = _inproj(xs, norm_g[layer, 1], w_in, min(ROW_TILE, ts))
            k_hist = cache_attn_k[e].reshape(nbs * WINDOW, KV_WIDTH)
            v_hist = cache_attn_v[e].reshape(nbs * WINDOW, KV_WIDTH)
            u_hist = jnp.pad(cache_conv[e], ((0, 0), (CONV_HIST_ROWS - (CONV_K - 1), 0), (0, 0))
                             ).reshape(nbs * CONV_HIST_ROWS, CONV_WIDTH)
            xs = _mixer_ab(xs, q, k, v, k_hist, v_hist, bg, u, u_hist, bias, attn_sinks[e],
                           conv_w, w_out, e, nb=nbs, t_len=ts_len)
            keep = min(WINDOW, ts_len)
            outs["ks"].append(last_rows(k, nbs, ts_len, keep, (N_KV_HEADS, HEAD_DIM)))
            outs["vs"].append(last_rows(v, nbs, ts_len, keep, (N_KV_HEADS, HEAD_DIM)))
            outs["cs"].append(last_rows(u, nbs, ts_len, CONV_K - 1, (CONV_WIDTH,)))
        else:
            o = layer // 2
            xp, st = _ffn_pool(xp, norm_g[layer, 0], ffn_weights(2 * layer), norm_g[layer, 1],
                               pw, pool_scale, o, nb=nb, t_len=t_len, tm=tf, sub=tf // 4)
            outs["pp"].append(last_rows(st, nb, POOL_HIST_ROWS, POOL_MAX - 1, (D_MODEL,)))
            xs = sample_ffn(xs, layer, 0, False)
            hist = jnp.pad(cache_pool[o], ((0, 0), (POOL_HIST_ROWS - (POOL_MAX - 1), 0), (0, 0))
                           ).reshape(nbs * POOL_HIST_ROWS, D_MODEL)
            xs, st = _mixer_c(xs, hist, norm_g[layer, 1], pw, pool_scale, o,
                              nb=nbs, t_len=ts_len, first_pos=past_len)
            outs["ps"].append(last_rows(st, nbs, POOL_HIST_ROWS, POOL_MAX - 1, (D_MODEL,)))
        xp = prompt_ffn(xp, layer, 1, last)
        xs = sample_ffn(xs, layer, 1, last)
    stacked = {name: jnp.stack(v) for name, v in outs.items()}
    return (xp.reshape(nb, t_len, D_MODEL), xs.reshape(nbs, ts_len, D_MODEL),
            stacked["kp"], stacked["vp"], stacked["cp"], stacked["pp"],
            stacked["ks"], stacked["vs"], stacked["cs"], stacked["ps"])
```

```python
import functools
import math

import jax
import jax.numpy as jnp
from jax import lax
from jax.experimental import pallas as pl
from jax.experimental.pallas import tpu as pltpu

D_MODEL = 1024
DEPTH = 2
CHUNK = 64
N_HEADS = 8
N_KV_HEADS = 2
Q_PER_KV = N_HEADS // N_KV_HEADS
HEAD_DIM = 64
ATTN_WIDTH = N_HEADS * HEAD_DIM
KV_WIDTH = N_KV_HEADS * HEAD_DIM
WINDOW = 128
N_BUCKETS = 32
MAX_DISTANCE = 128
CONV_WIDTH = D_MODEL // 2
CONV_K = 3
IN_WIDTH = ATTN_WIDTH + 2 * KV_WIDTH + 3 * CONV_WIDTH
MIX_WIDTH = ATTN_WIDTH + CONV_WIDTH
POOL_SIZES = (2, 4, 8, 16)
POOL_GROUP = D_MODEL // len(POOL_SIZES)
POOL_MAX = max(POOL_SIZES)
D_FF = 2816
EPS = 1e-6
NEG = -1e30

V7X_VMEM_BYTES = 64 * 1024 * 1024
SUBLANES_F32 = 8
LANES = 128
CONV_HIST_ROWS = SUBLANES_F32
POOL_HIST_ROWS = 16
ROW_TILE = 512
MIXER_ROW_TILE = 1024
FFN_ROW_TILE = 1024
FF_CHUNK = 256
FF_CHUNKS = tuple((c0, min(FF_CHUNK, D_FF - c0)) for c0 in range(0, D_FF, FF_CHUNK))

F32 = jnp.float32
BF16 = jnp.bfloat16


def _vmem_limit(nbytes):
    return int(min(V7X_VMEM_BYTES - (4 << 20), max(32 << 20, nbytes)))


def _rms(x, g):
    return x * lax.rsqrt(jnp.mean(x * x, axis=-1, keepdims=True) + EPS) * g


def _relbias_body(tab_ref, bkt_ref, o_ref):
    bkt = bkt_ref[...]
    for h in range(N_HEADS):
        acc = jnp.zeros(bkt.shape, F32)
        for b in range(N_BUCKETS):
            acc = jnp.where(bkt == b, tab_ref[b, h], acc)
        o_ref[h] = acc


def _rel_bucket(rel):
    half = N_BUCKETS // 2
    ret = jnp.where(rel > 0, half, 0)
    n = jnp.abs(rel)
    max_exact = half // 2
    nf = jnp.maximum(n, 1).astype(F32)
    large = max_exact + (jnp.log(nf / max_exact) / math.log(MAX_DISTANCE / max_exact)
                         * (half - max_exact)).astype(jnp.int32)
    large = jnp.minimum(large, half - 1)
    return ret + jnp.where(n < max_exact, n, large)


def _relbias(table):
    rel = (jnp.arange(WINDOW + CHUNK, dtype=jnp.int32) - WINDOW)[None, :] \
        - jnp.arange(CHUNK, dtype=jnp.int32)[:, None]
    bucket = _rel_bucket(rel).astype(jnp.int32)
    return pl.pallas_call(
        _relbias_body,
        out_shape=jax.ShapeDtypeStruct((N_HEADS, CHUNK, WINDOW + CHUNK), F32),
        in_specs=[pl.BlockSpec(memory_space=pltpu.SMEM),
                  pl.BlockSpec(memory_space=pltpu.VMEM)],
        out_specs=pl.BlockSpec(memory_space=pltpu.VMEM),
        name="relbias",
    )(table, bucket)


def _swiglu_rows(x, g, wg_ref, wu_ref, wd_ref, a_ref, r0):
    rows = x.shape[0]
    xn = _rms(x, g).astype(BF16)
    for c0, cw in FF_CHUNKS:
        gate = jnp.dot(xn, wg_ref[:, c0:c0 + cw], preferred_element_type=F32)
        up = jnp.dot(xn, wu_ref[:, c0:c0 + cw], preferred_element_type=F32)
        a_ref[r0:r0 + rows, c0:c0 + cw] = (jax.nn.silu(gate) * up).astype(BF16)
    y = jnp.dot(a_ref[r0:r0 + rows, :], wd_ref[...], preferred_element_type=F32)
    return x + 0.5 * y


BF16_SUBLANES = 16


class _CastJob:
    def __init__(self, a, lead, steps):
        self.a, self.lead = a, tuple(lead)
        rows, cols = a.shape[-2:]
        self.chunk = next(c for c in range(BF16_SUBLANES, rows + 1, BF16_SUBLANES)
                          if rows % c == 0 and steps % (rows // c) == 0)
        self.repeat = steps // (rows // self.chunk)
        self.out_shape = jax.ShapeDtypeStruct((rows, cols), BF16)

    def specs(self, step_of):
        none = (None,) * len(self.lead)
        cols = self.a.shape[-1]
        return (pl.BlockSpec(none + (self.chunk, cols),
                             lambda *g: self.lead + (step_of(*g) // self.repeat, 0)),
                pl.BlockSpec((self.chunk, cols), lambda *g: (step_of(*g) // self.repeat, 0)))


def _host_casts(body, n_in, n_out, jobs):
    k = len(jobs)

    def hosted(*refs):
        ins, cast_in = refs[:n_in], refs[n_in:n_in + k]
        outs, cast_out = refs[n_in + k:n_in + k + n_out], refs[n_in + k + n_out:n_in + 2 * k + n_out]
        for src, dst in zip(cast_in, cast_out):
            dst[...] = src[...].astype(BF16)
        body(*ins, *outs, *refs[n_in + 2 * k + n_out:])
    return hosted


def _ffn_body(x_ref, g_ref, wg_ref, wu_ref, wd_ref, fg_ref, o_ref, a_ref, *, final_norm):
    out = _swiglu_rows(x_ref[...], g_ref[...], wg_ref, wu_ref, wd_ref, a_ref, 0)
    if final_norm:
        out = _rms(out, fg_ref[...])
    o_ref[...] = out


def _ffn(x, g, w16, fg, final_norm, tm, cast=()):
    n = x.shape[0]
    steps = n // tm
    jobs = [_CastJob(a, lead, steps) for a, lead in cast]
    job_specs = [job.specs(lambda i: i) for job in jobs]
    whole = lambda a: pl.BlockSpec(a.shape, lambda i: (0, 0), pipeline_mode=pl.Buffered(1))
    vec = pl.BlockSpec((1, D_MODEL), lambda i: (0, 0))
    row = pl.BlockSpec((tm, D_MODEL), lambda i: (i, 0))
    est = 3 * D_MODEL * D_FF * 2 + 4 * tm * D_MODEL * 4 + tm * D_FF * 2 + 5 * tm * 1024 * 4 \
        + sum(6 * job.chunk * job.a.shape[-1] * 2 for job in jobs)
    return pl.pallas_call(
        _host_casts(functools.partial(_ffn_body, final_norm=final_norm), 6, 1, jobs),
        out_shape=(jax.ShapeDtypeStruct((n, D_MODEL), F32),) + tuple(job.out_shape for job in jobs),
        grid=(steps,),
        in_specs=[row, vec, whole(w16[0]), whole(w16[1]), whole(w16[2]), vec] + [s[0] for s in job_specs],
        out_specs=(row,) + tuple(s[1] for s in job_specs),
        scratch_shapes=[pltpu.VMEM((tm, D_FF), BF16)],
        compiler_params=pltpu.CompilerParams(
            dimension_semantics=("arbitrary",), vmem_limit_bytes=_vmem_limit(est)),
        name="ffn",
    )(x, g.reshape(1, D_MODEL), *w16, fg.reshape(1, D_MODEL), *[job.a for job in jobs])


FF_STREAM_CHUNK = 256
assert FF_STREAM_CHUNK % LANES == 0 and D_FF % FF_STREAM_CHUNK == 0


def _ffn_stream_body(x_ref, g_ref, wg_ref, wu_ref, wd_ref, fg_ref, o_ref, xn_ref, *, final_norm):
    k = pl.program_id(0)

    @pl.when(k == 0)
    def _():
        x = x_ref[...]
        xn_ref[...] = _rms(x, g_ref[...]).astype(BF16)
        o_ref[...] = x

    xn = xn_ref[...]
    gate = jnp.dot(xn, wg_ref[...], preferred_element_type=F32)
    up = jnp.dot(xn, wu_ref[...], preferred_element_type=F32)
    a = (jax.nn.silu(gate) * up).astype(BF16)
    o_ref[...] += 0.5 * jnp.dot(a, wd_ref[...], preferred_element_type=F32)

    if final_norm:
        @pl.when(k == pl.num_programs(0) - 1)
        def _():
            o_ref[...] = _rms(o_ref[...], fg_ref[...])


def _ffn_stream(x, g, w16, fg, final_norm):
    n = x.shape[0]
    fc = FF_STREAM_CHUNK
    whole = pl.BlockSpec((n, D_MODEL), lambda k: (0, 0))
    vec = pl.BlockSpec((1, D_MODEL), lambda k: (0, 0))
    return pl.pallas_call(
        functools.partial(_ffn_stream_body, final_norm=final_norm),
        out_shape=jax.ShapeDtypeStruct((n, D_MODEL), F32),
        grid=(D_FF // fc,),
        in_specs=[whole, vec, pl.BlockSpec((D_MODEL, fc), lambda k: (0, k)),
                  pl.BlockSpec((D_MODEL, fc), lambda k: (0, k)),
                  pl.BlockSpec((fc, D_MODEL), lambda k: (k, 0)), vec],
        out_specs=whole,
        scratch_shapes=[pltpu.VMEM((n, D_MODEL), BF16)],
        compiler_params=pltpu.CompilerParams(dimension_semantics=("arbitrary",)),
        name="ffn_stream",
    )(x, g.reshape(1, D_MODEL), *w16, fg.reshape(1, D_MODEL))


def _inproj_body(x_ref, g_ref, w_ref, q_ref, k_ref, v_ref, bg_ref, u_ref):
    h = _rms(x_ref[...], g_ref[...]).astype(BF16)
    p = jnp.dot(h, w_ref[...], preferred_element_type=F32)
    o = 0
    q_ref[...] = (p[:, o:o + ATTN_WIDTH] * (HEAD_DIM ** -0.5)).astype(BF16)
    o += ATTN_WIDTH
    k_ref[...] = p[:, o:o + KV_WIDTH]
    o += KV_WIDTH
    v_ref[...] = p[:, o:o + KV_WIDTH]
    o += KV_WIDTH
    bg_ref[...] = p[:, o:o + CONV_WIDTH]
    o += CONV_WIDTH
    u_ref[...] = p[:, o:o + CONV_WIDTH] * p[:, o + CONV_WIDTH:o + 2 * CONV_WIDTH]


def _inproj(x, g, w_in, tm):
    n = x.shape[0]
    row = lambda w: pl.BlockSpec((tm, w), lambda i: (i, 0))
    est = 2 * D_MODEL * IN_WIDTH * 2 + 2 * tm * (D_MODEL + IN_WIDTH) * 4 + 2 * tm * IN_WIDTH * 4
    return pl.pallas_call(
        _inproj_body,
        out_shape=(jax.ShapeDtypeStruct((n, ATTN_WIDTH), BF16),
                   jax.ShapeDtypeStruct((n, KV_WIDTH), F32),
                   jax.ShapeDtypeStruct((n, KV_WIDTH), F32),
                   jax.ShapeDtypeStruct((n, CONV_WIDTH), F32),
                   jax.ShapeDtypeStruct((n, CONV_WIDTH), F32)),
        grid=(n // tm,),
        in_specs=[row(D_MODEL), pl.BlockSpec((1, D_MODEL), lambda i: (0, 0)),
                  pl.BlockSpec((D_MODEL, IN_WIDTH), lambda i: (0, 0))],
        out_specs=(row(ATTN_WIDTH), row(KV_WIDTH), row(KV_WIDTH), row(CONV_WIDTH), row(CONV_WIDTH)),
        compiler_params=pltpu.CompilerParams(
            dimension_semantics=("arbitrary",), vmem_limit_bytes=_vmem_limit(est)),
        name="inproj",
    )(x, g.reshape(1, D_MODEL), w_in)


PAIR_WIDTH = 2 * HEAD_DIM
PAIRS_PER_KV = Q_PER_KV // 2
ATTN_SCALE = HEAD_DIM ** -0.5


def _mixer_ab_body(x_ref, q_ref, k_ref, v_ref, kh_ref, vh_ref, bg_ref, u_ref, uh_ref,
                   bias_ref, sink_ref, cw_ref, wo_ref, o_ref,
                   kbuf, vbuf, ubuf, mix, sbuf, pbuf, rbuf, *, nb, cl):
    kw = WINDOW + cl
    low = lax.broadcasted_iota(jnp.int32, (1, KV_WIDTH), 1) < HEAD_DIM

    def stage(buf, a, r0):
        b = pltpu.roll(a, HEAD_DIM, axis=1)
        rows = a.shape[0]
        buf[0, 0, r0:r0 + rows, :] = jnp.where(low, a, 0.0).astype(BF16)
        buf[0, 1, r0:r0 + rows, :] = jnp.where(low, 0.0, b).astype(BF16)
        buf[1, 0, r0:r0 + rows, :] = jnp.where(low, b, 0.0).astype(BF16)
        buf[1, 1, r0:r0 + rows, :] = jnp.where(low, 0.0, a).astype(BF16)

    blk = PAIRS_PER_KV * cl
    blocks = [(sq, hh, lh) for sq in range(nb) for hh in range(N_KV_HEADS) for lh in range(2)]
    for sq in range(nb):
        r0, h0 = sq * cl, sq * WINDOW
        kb, vb = kbuf.at[sq], vbuf.at[sq]
        stage(kb, kh_ref[h0:h0 + WINDOW, :], 0)
        stage(kb, k_ref[r0:r0 + cl, :], WINDOW)
        stage(vb, vh_ref[h0:h0 + WINDOW, :], 0)
        stage(vb, v_ref[r0:r0 + cl, :], WINDOW)
        for hh in range(N_KV_HEADS):
            ql = jnp.concatenate(
                [q_ref[r0:r0 + cl, (hh * PAIRS_PER_KV + pr) * PAIR_WIDTH:
                       (hh * PAIRS_PER_KV + pr + 1) * PAIR_WIDTH] for pr in range(PAIRS_PER_KV)],
                axis=0)
            for lh in range(2):
                b0 = blocks.index((sq, hh, lh)) * blk
                sbuf[b0:b0 + blk, :] = lax.dot_general(
                    ql, kb[hh, lh], (((1,), (1,)), ((), ())),
                    preferred_element_type=F32) + bias_ref[hh, lh]

    s = sbuf[...]
    sk = jnp.concatenate([sink_ref[hh, lh] for _, hh, lh in blocks], axis=0)
    m = jnp.maximum(jnp.max(s, axis=-1, keepdims=True), sk)
    p = jnp.exp(s - m)
    den = jnp.sum(p, axis=-1, keepdims=True) + jnp.exp(sk - m)
    pbuf[...] = p.astype(BF16)
    rbuf[...] = 1.0 / den

    for sq in range(nb):
        r0 = sq * cl
        for hh in range(N_KV_HEADS):
            acc = None
            for lh in range(2):
                b0 = blocks.index((sq, hh, lh)) * blk
                o = jnp.dot(pbuf[b0:b0 + blk, :], vbuf[sq, hh, lh],
                            preferred_element_type=F32) * rbuf[b0:b0 + blk, :]
                acc = o if acc is None else acc + o
            for pr in range(PAIRS_PER_KV):
                c0 = (hh * PAIRS_PER_KV + pr) * PAIR_WIDTH
                mix[r0:r0 + cl, c0:c0 + PAIR_WIDTH] = acc[pr * cl:(pr + 1) * cl].astype(BF16)
        h8 = sq * CONV_HIST_ROWS
        conv = _gated_conv(bg_ref[r0:r0 + cl, :], u_ref[r0:r0 + cl, :],
                           uh_ref[h8:h8 + CONV_HIST_ROWS, :], cw_ref, ubuf.at[sq])
        mix[r0:r0 + cl, ATTN_WIDTH:MIX_WIDTH] = conv.astype(BF16)

    o_ref[...] = x_ref[...] + jnp.dot(mix[...], wo_ref[...], preferred_element_type=F32)


def _gated_conv(bg, u, uh, cw_ref, ubuf):
    rows = u.shape[0]
    ubuf[0:CONV_HIST_ROWS, :] = uh
    ubuf[CONV_HIST_ROWS:CONV_HIST_ROWS + rows, :] = u
    conv = None
    for j in range(CONV_K):
        off = CONV_HIST_ROWS - (CONV_K - 1) + j
        term = cw_ref[j:j + 1, :] * ubuf[off:off + rows, :]
        conv = term if conv is None else conv + term
    return bg * conv


def _mixer_full_body(x_ref, g_ref, win_ref, biast_ref, sinkt_ref, cw_ref, wo_ref,
                     o_ref, klast_ref, vlast_ref, ulast_ref,
                     qbuf, kdup, vta, vtb, ubuf, mix, sbuf, pbuf, rbuf, kcar, vcar, ucar, *, tq):
    b = pl.program_id(0)
    t = pl.program_id(1)
    cl = CHUNK
    kw = WINDOW + cl
    w = WINDOW + tq
    low = lax.broadcasted_iota(jnp.int32, (1, KV_WIDTH), 1) < HEAD_DIM

    @pl.when((b == 0) & (t == 0))
    def _():
        kcar[...] = jnp.zeros_like(kcar)
        vcar[...] = jnp.zeros_like(vcar)
        ucar[...] = jnp.zeros_like(ucar)

    x = x_ref[...]
    h = _rms(x, g_ref[...]).astype(BF16)

    def proj(c0, width):
        return jnp.dot(h, win_ref[:, c0:c0 + width], preferred_element_type=F32)

    qbuf[...] = (proj(0, ATTN_WIDTH) * ATTN_SCALE).astype(BF16)
    kv = proj(ATTN_WIDTH, 2 * KV_WIDTH)
    k = kv[:, 0:KV_WIDTH]
    v = kv[:, KV_WIDTH:2 * KV_WIDTH]
    c0 = ATTN_WIDTH + 2 * KV_WIDTH
    bg = proj(c0, CONV_WIDTH)
    u = proj(c0 + CONV_WIDTH, CONV_WIDTH) * proj(c0 + 2 * CONV_WIDTH, CONV_WIDTH)

    kh = kcar[...]
    vh = vcar[...]
    uh = jnp.where(t == 0, 0.0, ucar[...])
    k_tail = k[tq - WINDOW:tq, :]
    v_tail = v[tq - WINDOW:tq, :]
    u_tail = u[tq - CONV_HIST_ROWS:tq, :]
    kcar[...] = k_tail
    vcar[...] = v_tail
    ucar[...] = u_tail
    klast_ref[...] = k_tail
    vlast_ref[...] = v_tail
    ulast_ref[...] = u_tail

    kf = jnp.concatenate([kh, k], axis=0)
    kr = pltpu.roll(kf, HEAD_DIM, axis=1)
    kdup[0] = jnp.where(low, kf, kr).astype(BF16)
    kdup[1] = jnp.where(low, kr, kf).astype(BF16)

    vf = jnp.concatenate([vh, v, jnp.zeros((cl, KV_WIDTH), F32)], axis=0)
    vr = pltpu.roll(vf, HEAD_DIM, axis=1)
    for hh, vd in enumerate((jnp.where(low, vf, vr), jnp.where(low, vr, vf))):
        for blk in range(w // KV_WIDTH):
            b0 = blk * KV_WIDTH
            vta[hh, :, b0:b0 + KV_WIDTH] = vd[b0:b0 + KV_WIDTH].T.astype(BF16)
            vtb[hh, :, b0:b0 + KV_WIDTH] = vd[cl + b0:cl + b0 + KV_WIDTH].T.astype(BF16)

    n_cols = PAIRS_PER_KV * PAIR_WIDTH
    key = lax.broadcasted_iota(jnp.int32, (kw, n_cols), 0)
    rr = lax.broadcasted_iota(jnp.int32, (PAIR_WIDTH, PAIR_WIDTH), 0) < HEAD_DIM
    cc = lax.broadcasted_iota(jnp.int32, (PAIR_WIDTH, PAIR_WIDTH), 1) < HEAD_DIM
    diag = rr == cc
    for c in range(tq // cl):
        r0 = c * cl
        for hh in range(N_KV_HEADS):
            rows = []
            for pr in range(PAIRS_PER_KV):
                c0 = (hh * PAIRS_PER_KV + pr) * PAIR_WIDTH
                qp = qbuf[r0:r0 + cl, c0:c0 + PAIR_WIDTH]
                rows += [jnp.where(low, qp, 0), jnp.where(low, 0, qp)]
            qm = jnp.concatenate(rows, axis=0)
            s = lax.dot_general(kdup[hh, r0:r0 + kw, :], qm, (((1,), (1,)), ((), ())),
                                preferred_element_type=F32) + biast_ref[hh]
            if r0 < WINDOW:
                s = jnp.where(key + (t * tq + r0 - WINDOW) >= 0, s, NEG)
            j0 = (c * N_KV_HEADS + hh) * n_cols
            sbuf[:, j0:j0 + n_cols] = s

    s = sbuf[...]
    sk = sinkt_ref[...]
    m = jnp.maximum(jnp.max(s, axis=0, keepdims=True), sk)
    p = jnp.exp(s - m)
    den = jnp.sum(p, axis=0, keepdims=True) + jnp.exp(sk - m)
    pbuf[...] = p.astype(BF16)
    rbuf[...] = 1.0 / den

    for c in range(tq // cl):
        r0 = c * cl
        k0 = r0 if c % 2 == 0 else r0 - cl
        vt_ref = vta if c % 2 == 0 else vtb
        for hh in range(N_KV_HEADS):
            j0 = (c * N_KV_HEADS + hh) * n_cols
            ot = jnp.dot(vt_ref[hh, :, k0:k0 + kw], pbuf[:, j0:j0 + n_cols],
                         preferred_element_type=F32) * rbuf[:, j0:j0 + n_cols]
            for pr in range(PAIRS_PER_KV):
                blk = jnp.where(diag, ot[:, pr * PAIR_WIDTH:(pr + 1) * PAIR_WIDTH], 0.0).T
                c0 = (hh * PAIRS_PER_KV + pr) * PAIR_WIDTH
                mix[r0:r0 + cl, c0:c0 + PAIR_WIDTH] = (blk[0:cl] + blk[cl:2 * cl]).astype(BF16)

    mix[:, ATTN_WIDTH:MIX_WIDTH] = _gated_conv(bg, u, uh, cw_ref, ubuf).astype(BF16)
    o_ref[...] = x + jnp.dot(mix[...], wo_ref[...], preferred_element_type=F32)


def _head_layout(bias, sinks, cl):
    kw = WINDOW + cl
    b5 = bias[:, :cl, :kw].reshape(N_KV_HEADS, PAIRS_PER_KV, 2, cl, kw)
    s5 = jnp.broadcast_to(sinks.reshape(N_KV_HEADS, PAIRS_PER_KV, 2, 1, 1),
                          (N_KV_HEADS, PAIRS_PER_KV, 2, cl, 1))
    return b5, s5


def _mixer_full(x, g, w_in, bias, sinks, conv_w, w_out, e, *, nb, t_len, tq, cast=()):
    n = x.shape[0]
    nt = t_len // tq
    jobs = [_CastJob(a, lead, nb * nt) for a, lead in cast]
    job_specs = [job.specs(lambda b, t: b * nt + t) for job in jobs]
    cl = CHUNK
    kw = WINDOW + cl
    assert tq % KV_WIDTH == 0 and tq >= WINDOW
    b5, s5 = _head_layout(bias, sinks, cl)
    n_cols = PAIRS_PER_KV * PAIR_WIDTH
    all_cols = (tq // cl) * N_KV_HEADS * n_cols
    bias_l = jnp.transpose(b5, (0, 4, 1, 2, 3)).reshape(N_KV_HEADS, kw, n_cols)
    sink_l = jnp.tile(s5.reshape(1, N_KV_HEADS * n_cols), (1, tq // cl))
    row = pl.BlockSpec((tq, D_MODEL), lambda b, t: (b * nt + t, 0))
    const = lambda a: pl.BlockSpec(a.shape, lambda b, t: (0,) * a.ndim)
    per_seq = lambda rows, width: pl.BlockSpec((rows, width), lambda b, t: (b, 0))
    est = 4 * tq * D_MODEL * 4 + (D_MODEL * IN_WIDTH + MIX_WIDTH * D_MODEL) * 2 \
        + kw * all_cols * 6 + 16 * tq * D_MODEL * 4
    whole = lambda a: pl.BlockSpec(a.shape, lambda b, t: (0, 0), pipeline_mode=pl.Buffered(1))
    return pl.pallas_call(
        _host_casts(functools.partial(_mixer_full_body, tq=tq), 7, 4, jobs),
        out_shape=(jax.ShapeDtypeStruct((n, D_MODEL), F32),
                   jax.ShapeDtypeStruct((nb * WINDOW, KV_WIDTH), F32),
                   jax.ShapeDtypeStruct((nb * WINDOW, KV_WIDTH), F32),
                   jax.ShapeDtypeStruct((nb * CONV_HIST_ROWS, CONV_WIDTH), F32))
        + tuple(job.out_shape for job in jobs),
        grid=(nb, nt),
        in_specs=[row, pl.BlockSpec((1, D_MODEL), lambda b, t: (0, 0)), whole(w_in),
                  const(bias_l), const(sink_l),
                  pl.BlockSpec((None, CONV_K, CONV_WIDTH), lambda b, t: (e, 0, 0)),
                  whole(w_out)] + [s[0] for s in job_specs],
        out_specs=(row, per_seq(WINDOW, KV_WIDTH), per_seq(WINDOW, KV_WIDTH),
                   per_seq(CONV_HIST_ROWS, CONV_WIDTH)) + tuple(s[1] for s in job_specs),
        scratch_shapes=[pltpu.VMEM((tq, ATTN_WIDTH), BF16),
                        pltpu.VMEM((N_KV_HEADS, WINDOW + tq, KV_WIDTH), BF16),
                        pltpu.VMEM((N_KV_HEADS, PAIR_WIDTH, WINDOW + tq), BF16),
                        pltpu.VMEM((N_KV_HEADS, PAIR_WIDTH, WINDOW + tq), BF16),
                        pltpu.VMEM((CONV_HIST_ROWS + tq, CONV_WIDTH), F32),
                        pltpu.VMEM((tq, MIX_WIDTH), BF16),
                        pltpu.VMEM((kw, all_cols), F32), pltpu.VMEM((kw, all_cols), BF16),
                        pltpu.VMEM((1, all_cols), F32),
                        pltpu.VMEM((WINDOW, KV_WIDTH), F32), pltpu.VMEM((WINDOW, KV_WIDTH), F32),
                        pltpu.VMEM((CONV_HIST_ROWS, CONV_WIDTH), F32)],
        compiler_params=pltpu.CompilerParams(
            dimension_semantics=("arbitrary", "arbitrary"), vmem_limit_bytes=_vmem_limit(est)),
        name="mixer_full",
    )(x, g.reshape(1, D_MODEL), w_in, bias_l, sink_l, conv_w, w_out, *[job.a for job in jobs])


def _mixer_ab(x, q, k, v, k_hist, v_hist, bg, u, u_hist, bias, sinks, conv_w, w_out, e,
              *, nb, t_len):
    n = x.shape[0]
    cl = t_len
    assert t_len <= CHUNK and t_len % BF16_SUBLANES == 0
    kw = WINDOW + cl
    b5, s5 = _head_layout(bias, sinks, cl)
    bias_l = jnp.transpose(b5, (0, 2, 1, 3, 4)).reshape(N_KV_HEADS, 2, PAIRS_PER_KV * cl, kw)
    sink_l = jnp.transpose(s5, (0, 2, 1, 3, 4)).reshape(N_KV_HEADS, 2, PAIRS_PER_KV * cl, 1)
    vmem = pl.BlockSpec(memory_space=pltpu.VMEM)
    score_rows = nb * N_KV_HEADS * 2 * PAIRS_PER_KV * cl
    return pl.pallas_call(
        functools.partial(_mixer_ab_body, nb=nb, cl=cl),
        out_shape=jax.ShapeDtypeStruct((n, D_MODEL), F32),
        in_specs=[vmem] * 13,
        out_specs=vmem,
        scratch_shapes=[pltpu.VMEM((nb, N_KV_HEADS, 2, kw, KV_WIDTH), BF16),
                        pltpu.VMEM((nb, N_KV_HEADS, 2, kw, KV_WIDTH), BF16),
                        pltpu.VMEM((nb, CONV_HIST_ROWS + cl, CONV_WIDTH), F32),
                        pltpu.VMEM((n, MIX_WIDTH), BF16),
                        pltpu.VMEM((score_rows, kw), F32), pltpu.VMEM((score_rows, kw), BF16),
                        pltpu.VMEM((score_rows, 1), F32)],
        name="mixer_ab",
    )(x, q, k, v, k_hist, v_hist, bg, u, u_hist, bias_l, sink_l, conv_w[e], w_out)


def _pool_rows(x, hist, g, pw_ref, sc_ref, o_ref, r0, pos0):
    rows = x.shape[0]
    h = _rms(x, g)
    sums = []
    cur = jnp.concatenate([hist, h], axis=0)
    for gi, w in enumerate(POOL_SIZES):
        assert w == 2 ** (gi + 1)
        cur = cur + pltpu.roll(cur, w // 2, axis=0)
        sums.append(cur[POOL_HIST_ROWS:POOL_HIST_ROWS + rows, 0:POOL_GROUP])
        if gi + 1 < len(POOL_SIZES):
            cur = cur[:, POOL_GROUP:]

    pos = pos0 + lax.broadcasted_iota(jnp.int32, (rows, 1), 0)
    for gi, w in enumerate(POOL_SIZES):
        sl = slice(gi * POOL_GROUP, (gi + 1) * POOL_GROUP)
        cnt = jnp.minimum(pos + 1, w).astype(F32)
        d = (sums[gi] / cnt - h[:, sl]).astype(BF16)
        y = jnp.dot(d, pw_ref[gi], preferred_element_type=F32)
        o_ref[r0:r0 + rows, sl] = x[:, sl] + y * sc_ref[:, sl]
    return h[rows - POOL_HIST_ROWS:rows, :]


def _mixer_c_body(x_ref, hist_ref, g_ref, pw_ref, sc_ref, o_ref, st_ref, *, nb, t_len, first_pos):
    for sq in range(nb):
        r0, h0 = sq * t_len, sq * POOL_HIST_ROWS
        st_ref[h0:h0 + POOL_HIST_ROWS, :] = _pool_rows(
            x_ref[r0:r0 + t_len, :], hist_ref[h0:h0 + POOL_HIST_ROWS, :], g_ref[...],
            pw_ref, sc_ref, o_ref, r0, first_pos)


def _pool_specs(o):
    return [pl.BlockSpec((None, len(POOL_SIZES), POOL_GROUP, POOL_GROUP), lambda *a: (o, 0, 0, 0)),
            pl.BlockSpec((None, 1, D_MODEL), lambda *a: (o, 0, 0))]


def _mixer_c(x, hist, g, pool_w, scale, o, *, nb, t_len, first_pos):
    n = x.shape[0]
    whole = lambda a: pl.BlockSpec(a.shape, lambda i: (0,) * a.ndim)
    g2 = g.reshape(1, D_MODEL)
    st_shape = jax.ShapeDtypeStruct((nb * POOL_HIST_ROWS, D_MODEL), F32)
    return pl.pallas_call(
        functools.partial(_mixer_c_body, nb=nb, t_len=t_len, first_pos=first_pos),
        out_shape=(jax.ShapeDtypeStruct((n, D_MODEL), F32), st_shape),
        grid=(1,),
        in_specs=[whole(x), whole(hist), whole(g2)] + _pool_specs(o),
        out_specs=(whole(x), whole(st_shape)),
        compiler_params=pltpu.CompilerParams(dimension_semantics=("arbitrary",)),
        name="mixer_c",
    )(x, hist, g2, pool_w, scale.reshape(scale.shape[0], 1, D_MODEL))


def _ffn_pool_body(x_ref, g1_ref, wg_ref, wu_ref, wd_ref, g2_ref, pw_ref, sc_ref,
                   o_ref, st_ref, a_ref, carry_ref, *, tm, sub, tiles_per_seq):
    i = pl.program_id(0)
    t = i % tiles_per_seq

    @pl.when(i == 0)
    def _():
        carry_ref[...] = jnp.zeros_like(carry_ref)

    hist = jnp.where(t == 0, 0.0, carry_ref[...])
    g1 = g1_ref[...]
    g2 = g2_ref[...]
    blocks = [_swiglu_rows(x_ref[r0:r0 + sub, :], g1, wg_ref, wu_ref, wd_ref, a_ref, r0)
              for r0 in range(0, tm, sub)]
    for s, xb in enumerate(blocks):
        hist = _pool_rows(xb, hist, g2, pw_ref, sc_ref, o_ref, s * sub, t * tm + s * sub)
    carry_ref[...] = hist
    st_ref[...] = hist


def _ffn_pool(x, g1, w16, g2, pool_w, scale, o, *, nb, t_len, tm, sub):
    n = x.shape[0]
    tiles_per_seq = t_len // tm
    whole = lambda a: pl.BlockSpec(a.shape, lambda i: (0, 0), pipeline_mode=pl.Buffered(1))
    vec = pl.BlockSpec((1, D_MODEL), lambda i: (0, 0))
    row = pl.BlockSpec((tm, D_MODEL), lambda i: (i, 0))
    est = 3 * D_MODEL * D_FF * 2 + 4 * tm * D_MODEL * 4 + tm * D_FF * 2 + 6 * tm * 1024 * 4
    return pl.pallas_call(
        functools.partial(_ffn_pool_body, tm=tm, sub=sub, tiles_per_seq=tiles_per_seq),
        out_shape=(jax.ShapeDtypeStruct((n, D_MODEL), F32),
                   jax.ShapeDtypeStruct((nb * POOL_HIST_ROWS, D_MODEL), F32)),
        grid=(n // tm,),
        in_specs=[row, vec, whole(w16[0]), whole(w16[1]), whole(w16[2]), vec] + _pool_specs(o),
        out_specs=(row, pl.BlockSpec((POOL_HIST_ROWS, D_MODEL), lambda i: (i // tiles_per_seq, 0))),
        scratch_shapes=[pltpu.VMEM((tm, D_FF), BF16), pltpu.VMEM((POOL_HIST_ROWS, D_MODEL), F32)],
        compiler_params=pltpu.CompilerParams(
            dimension_semantics=("arbitrary",), vmem_limit_bytes=_vmem_limit(est)),
        name="ffn_pool",
    )(x, g1.reshape(1, D_MODEL), *w16, g2.reshape(1, D_MODEL), pool_w,
      scale.reshape(scale.shape[0], 1, D_MODEL))


def kernel(x_prompt, x_sample, cache_attn_k, cache_attn_v, cache_conv, cache_pool, norm_g, ffn_w_gate, ffn_w_up, ffn_w_down, mix_w_in, mix_w_out, conv_w, attn_sinks, rel_bias_table, pool_w, pool_scale, final_norm_g):
    past_len = 1024
    nb, t_len, _ = x_prompt.shape
    nbs, ts_len, _ = x_sample.shape
    xp = x_prompt.reshape(nb * t_len, D_MODEL)
    xs = x_sample.reshape(nbs * ts_len, D_MODEL)
    tf = min(FFN_ROW_TILE, xp.shape[0])
    tq = min(MIXER_ROW_TILE, t_len)
    assert t_len % tf == 0
    ts = xs.shape[0]
    pw = pool_w.astype(BF16)
    bias = _relbias(rel_bias_table)
    ffn_w = (ffn_w_gate, ffn_w_up, ffn_w_down)

    w16 = {0: tuple(a[0, 0].astype(BF16) for a in ffn_w)}
    pending = list(range(1, 2 * DEPTH))
    mix16 = {}

    def ffn_weights(n):
        if n not in w16:
            pending.remove(n)
            w16[n] = tuple(a[n // 2, n % 2].astype(BF16) for a in ffn_w)
        return w16[n]

    def next_casts(n):
        if not pending or pending[0] <= n:
            return None, []
        m = pending.pop(0)
        return m, [(a, (m // 2, m % 2)) for a in ffn_w]

    def prompt_ffn(x, layer, j, final):
        n = 2 * layer + j
        m, cast = next_casts(n)
        if n == 0:
            cast = cast + [(a, (e,)) for e in range(mix_w_in.shape[0]) for a in (mix_w_in, mix_w_out)]
        y, *done = _ffn(x, norm_g[layer, 2 * j], ffn_weights(n), final_norm_g, final, tf, cast=cast)
        if m is not None:
            w16[m] = tuple(done[:3])
            done = done[3:]
        for e in range(len(done) // 2):
            mix16[e] = (done[2 * e], done[2 * e + 1])
        return y

    def sample_ffn(x, layer, j, final):
        return _ffn_stream(x, norm_g[layer, 2 * j], ffn_weights(2 * layer + j), final_norm_g, final)

    def last_rows(a, seqs, rows, keep, tail_shape):
        return a.reshape((seqs, rows) + tail_shape)[:, rows - keep:]

    outs = {name: [] for name in ("kp", "vp", "cp", "pp", "ks", "vs", "cs", "ps")}
    for layer in range(DEPTH):
        last = layer == DEPTH - 1
        if layer % 2 == 0:
            e = layer // 2
            xp = prompt_ffn(xp, layer, 0, False)
            xs = sample_ffn(xs, layer, 0, False)
            if e not in mix16:
                mix16[e] = (mix_w_in[e].astype(BF16), mix_w_out[e].astype(BF16))
            w_in, w_out = mix16[e]
            m, cast = next_casts(2 * layer)
            xp, k, v, u, *done = _mixer_full(xp, norm_g[layer, 1], w_in, bias, attn_sinks[e], conv_w,
                                             w_out, e, nb=nb, t_len=t_len, tq=tq, cast=cast)
            if m is not None:
                w16[m] = tuple(done)
            keep = min(WINDOW, t_len)
            outs["kp"].append(last_rows(k, nb, WINDOW, keep, (N_KV_HEADS, HEAD_DIM)))
            outs["vp"].append(last_rows(v, nb, WINDOW, keep, (N_KV_HEADS, HEAD_DIM)))
            outs["cp"].append(last_rows(u, nb, CONV_HIST_ROWS, CONV_K - 1, (CONV_WIDTH,)))
            q, k, v, bg, u = _inproj(xs, norm_g[layer, 1], w_in, min(ROW_TILE, ts))
            k_hist = cache_attn_k[e].reshape(nbs * WINDOW, KV_WIDTH)
            v_hist = cache_attn_v[e].reshape(nbs * WINDOW, KV_WIDTH)
            u_hist = jnp.pad(cache_conv[e], ((0, 0), (CONV_HIST_ROWS - (CONV_K - 1), 0), (0, 0))
                             ).reshape(nbs * CONV_HIST_ROWS, CONV_WIDTH)
            xs = _mixer_ab(xs, q, k, v, k_hist, v_hist, bg, u, u_hist, bias, attn_sinks[e],
                           conv_w, w_out, e, nb=nbs, t_len=ts_len)
            keep = min(WINDOW, ts_len)
            outs["ks"].append(last_rows(k, nbs, ts_len, keep, (N_KV_HEADS, HEAD_DIM)))
            outs["vs"].append(last_rows(v, nbs, ts_len, keep, (N_KV_HEADS, HEAD_DIM)))
            outs["cs"].append(last_rows(u, nbs, ts_len, CONV_K - 1, (CONV_WIDTH,)))
        else:
            o = layer // 2
            xp, st = _ffn_pool(xp, norm_g[layer, 0], ffn_weights(2 * layer), norm_g[layer, 1],
                               pw, pool_scale, o, nb=nb, t_len=t_len, tm=tf, sub=tf // 4)
            outs["pp"].append(last_rows(st, nb, POOL_HIST_ROWS, POOL_MAX - 1, (D_MODEL,)))
            xs = sample_ffn(xs, layer, 0, False)
            hist = jnp.pad(cache_pool[o], ((0, 0), (POOL_HIST_ROWS - (POOL_MAX - 1), 0), (0, 0))
                           ).reshape(nbs * POOL_HIST_ROWS, D_MODEL)
            xs, st = _mixer_c(xs, hist, norm_g[layer, 1], pw, pool_scale, o,
                              nb=nbs, t_len=ts_len, first_pos=past_len)
            outs["ps"].append(last_rows(st, nbs, POOL_HIST_ROWS, POOL_MAX - 1, (D_MODEL,)))
        xp = prompt_ffn(xp, layer, 1, last)
        xs = sample_ffn(xs, layer, 1, last)
    stacked = {name: jnp.stack(v) for name, v in outs.items()}
    return (xp.reshape(nb, t_len, D_MODEL), xs.reshape(nbs, ts_len, D_MODEL),
            stacked["kp"], stacked["vp"], stacked["cp"], stacked["pp"],
            stacked["ks"], stacked["vs"], stacked["cs"], stacked["ps"])
```

```python
import functools
import math

import jax
import jax.numpy as jnp
from jax import lax
from jax.experimental import pallas as pl
from jax.experimental.pallas import tpu as pltpu

D_MODEL = 1024
DEPTH = 2
CHUNK = 64
N_HEADS = 8
N_KV_HEADS = 2
Q_PER_KV = N_HEADS // N_KV_HEADS
HEAD_DIM = 64
ATTN_WIDTH = N_HEADS * HEAD_DIM
KV_WIDTH = N_KV_HEADS * HEAD_DIM
WINDOW = 128
N_BUCKETS = 32
MAX_DISTANCE = 128
CONV_WIDTH = D_MODEL // 2
CONV_K = 3
IN_WIDTH = ATTN_WIDTH + 2 * KV_WIDTH + 3 * CONV_WIDTH
MIX_WIDTH = ATTN_WIDTH + CONV_WIDTH
POOL_SIZES = (2, 4, 8, 16)
POOL_GROUP = D_MODEL // len(POOL_SIZES)
POOL_MAX = max(POOL_SIZES)
D_FF = 2816
EPS = 1e-6
NEG = -1e30

V7X_VMEM_BYTES = 64 * 1024 * 1024
SUBLANES_F32 = 8
BF16_SUBLANES = 16
LANES = 128
CONV_HIST_ROWS = SUBLANES_F32
POOL_HIST_ROWS = 16
ROW_TILE = 512
MIXER_ROW_TILE = 1024
FFN_ROW_TILE = 1024
FF_CHUNK = 256
FF_CHUNKS = tuple((c0, min(FF_CHUNK, D_FF - c0)) for c0 in range(0, D_FF, FF_CHUNK))
POOL_SUB_BLOCKS = 4

F32 = jnp.float32
BF16 = jnp.bfloat16


def _vmem_limit(nbytes):
    return int(min(V7X_VMEM_BYTES - (4 << 20), max(32 << 20, nbytes)))


def _rms(x, g):
    return x * lax.rsqrt(jnp.mean(x * x, axis=-1, keepdims=True) + EPS) * g


PAIR_WIDTH = 2 * HEAD_DIM
PAIRS_PER_KV = Q_PER_KV // 2
SCORE_COLS = PAIRS_PER_KV * PAIR_WIDTH


def _attn_tables_body(tab_ref, sink_ref, bkt_t_ref, bkt_r_ref,
                      bias_t_ref, sink_t_ref, bias_r_ref, sink_r_ref, *, cl_s, n_chunks):
    col_head = lax.broadcasted_iota(jnp.int32, (1, SCORE_COLS), 1) // HEAD_DIM
    row_pair = lax.broadcasted_iota(jnp.int32, (PAIRS_PER_KV * cl_s, 1), 0) // cl_s

    def per_column(hh, value):
        out = jnp.full((1, SCORE_COLS), value(hh * Q_PER_KV), F32)
        for g in range(1, Q_PER_KV):
            out = jnp.where(col_head == g, value(hh * Q_PER_KV + g), out)
        return out

    def per_row(hh, lh, value):
        out = jnp.full((PAIRS_PER_KV * cl_s, 1), value(hh * Q_PER_KV + lh), F32)
        for pr in range(1, PAIRS_PER_KV):
            out = jnp.where(row_pair == pr, value(hh * Q_PER_KV + 2 * pr + lh), out)
        return out

    bkt_t = bkt_t_ref[...]
    bkt_r = bkt_r_ref[...]
    for hh in range(N_KV_HEADS):
        acc = jnp.zeros(bkt_t.shape, F32)
        for b in range(N_BUCKETS):
            acc = jnp.where(bkt_t == b, per_column(hh, lambda h: tab_ref[b, h]), acc)
        bias_t_ref[hh] = acc
        sink_row = per_column(hh, lambda h: sink_ref[h])
        for c in range(n_chunks):
            j0 = (c * N_KV_HEADS + hh) * SCORE_COLS
            sink_t_ref[:, j0:j0 + SCORE_COLS] = sink_row
        for lh in range(2):
            acc = jnp.zeros(bkt_r.shape, F32)
            for b in range(N_BUCKETS):
                acc = jnp.where(bkt_r == b, per_row(hh, lh, lambda h: tab_ref[b, h]), acc)
            bias_r_ref[hh, lh] = acc
            sink_r_ref[hh, lh] = per_row(hh, lh, lambda h: sink_ref[h])


def _rel_bucket(rel):
    half = N_BUCKETS // 2
    ret = jnp.where(rel > 0, half, 0)
    n = jnp.abs(rel)
    max_exact = half // 2
    nf = jnp.maximum(n, 1).astype(F32)
    large = max_exact + (jnp.log(nf / max_exact) / math.log(MAX_DISTANCE / max_exact)
                         * (half - max_exact)).astype(jnp.int32)
    large = jnp.minimum(large, half - 1)
    return ret + jnp.where(n < max_exact, n, large)


def _attn_tables(table, sinks, cl_s, n_chunks):
    rel = (jnp.arange(WINDOW + CHUNK, dtype=jnp.int32) - WINDOW)[None, :] \
        - jnp.arange(CHUNK, dtype=jnp.int32)[:, None]
    bucket = _rel_bucket(rel).astype(jnp.int32)
    bkt_t = jnp.tile(bucket.T, (1, Q_PER_KV))
    kw_s = WINDOW + cl_s
    bkt_r = jnp.tile(bucket[:cl_s, :kw_s], (PAIRS_PER_KV, 1))
    smem = pl.BlockSpec(memory_space=pltpu.SMEM)
    vmem = pl.BlockSpec(memory_space=pltpu.VMEM)
    return pl.pallas_call(
        functools.partial(_attn_tables_body, cl_s=cl_s, n_chunks=n_chunks),
        out_shape=(jax.ShapeDtypeStruct((N_KV_HEADS, WINDOW + CHUNK, SCORE_COLS), F32),
                   jax.ShapeDtypeStruct((1, n_chunks * N_KV_HEADS * SCORE_COLS), F32),
                   jax.ShapeDtypeStruct((N_KV_HEADS, 2, PAIRS_PER_KV * cl_s, kw_s), F32),
                   jax.ShapeDtypeStruct((N_KV_HEADS, 2, PAIRS_PER_KV * cl_s, 1), F32)),
        in_specs=[smem, smem, vmem, vmem],
        out_specs=(vmem, vmem, vmem, vmem),
        name="attn_tables",
    )(table, sinks, bkt_t, bkt_r)


def _swiglu_rows(x, g, wg_ref, wu_ref, wd_ref, a_ref, r0):
    rows = x.shape[0]
    xn = _rms(x, g).astype(BF16)
    for c0, cw in FF_CHUNKS:
        gate = jnp.dot(xn, wg_ref[:, c0:c0 + cw], preferred_element_type=F32)
        up = jnp.dot(xn, wu_ref[:, c0:c0 + cw], preferred_element_type=F32)
        a_ref[r0:r0 + rows, c0:c0 + cw] = (jax.nn.silu(gate) * up).astype(BF16)
    y = jnp.dot(a_ref[r0:r0 + rows, :], wd_ref[...], preferred_element_type=F32)
    return x + 0.5 * y


class _CastJob:
    def __init__(self, a, lead, steps):
        self.a, self.lead = a, tuple(lead)
        rows, cols = a.shape[-2:]
        self.chunk = next(c for c in range(BF16_SUBLANES, rows + 1, BF16_SUBLANES)
                          if rows % c == 0 and steps % (rows // c) == 0)
        self.repeat = steps // (rows // self.chunk)
        self.out_shape = jax.ShapeDtypeStruct((rows, cols), BF16)

    def specs(self, step_of):
        none = (None,) * len(self.lead)
        cols = self.a.shape[-1]
        return (pl.BlockSpec(none + (self.chunk, cols),
                             lambda *g: self.lead + (step_of(*g) // self.repeat, 0)),
                pl.BlockSpec((self.chunk, cols), lambda *g: (step_of(*g) // self.repeat, 0)))


def _host_casts(body, n_in, n_out, jobs):
    k = len(jobs)

    def hosted(*refs):
        ins, cast_in = refs[:n_in], refs[n_in:n_in + k]
        outs, cast_out = refs[n_in + k:n_in + k + n_out], refs[n_in + k + n_out:n_in + 2 * k + n_out]
        for src, dst in zip(cast_in, cast_out):
            dst[...] = src[...].astype(BF16)
        body(*ins, *outs, *refs[n_in + 2 * k + n_out:])
    return hosted


def _ffn_body(x_ref, g_ref, wg_ref, wu_ref, wd_ref, fg_ref, o_ref, a_ref, *, final_norm):
    out = _swiglu_rows(x_ref[...], g_ref[...], wg_ref, wu_ref, wd_ref, a_ref, 0)
    if final_norm:
        out = _rms(out, fg_ref[...])
    o_ref[...] = out


def _ffn(x, g, w16, fg, final_norm, tm, cast=()):
    n = x.shape[0]
    steps = n // tm
    jobs = [_CastJob(a, lead, steps) for a, lead in cast]
    job_specs = [job.specs(lambda i: i) for job in jobs]
    whole = lambda a: pl.BlockSpec(a.shape, lambda i: (0, 0), pipeline_mode=pl.Buffered(1))
    vec = pl.BlockSpec((1, D_MODEL), lambda i: (0, 0))
    row = pl.BlockSpec((tm, D_MODEL), lambda i: (i, 0))
    est = 3 * D_MODEL * D_FF * 2 + 4 * tm * D_MODEL * 4 + tm * D_FF * 2 + 5 * tm * 1024 * 4 \
        + sum(6 * job.chunk * job.a.shape[-1] * 2 for job in jobs)
    return pl.pallas_call(
        _host_casts(functools.partial(_ffn_body, final_norm=final_norm), 6, 1, jobs),
        out_shape=(jax.ShapeDtypeStruct((n, D_MODEL), F32),) + tuple(job.out_shape for job in jobs),
        grid=(steps,),
        in_specs=[row, vec, whole(w16[0]), whole(w16[1]), whole(w16[2]), vec] + [s[0] for s in job_specs],
        out_specs=(row,) + tuple(s[1] for s in job_specs),
        scratch_shapes=[pltpu.VMEM((tm, D_FF), BF16)],
        compiler_params=pltpu.CompilerParams(
            dimension_semantics=("arbitrary",), vmem_limit_bytes=_vmem_limit(est)),
        name="ffn",
    )(x, g.reshape(1, D_MODEL), *w16, fg.reshape(1, D_MODEL), *[job.a for job in jobs])


def _ffn_stream_body(x_ref, g_ref, wg_hbm, wu_hbm, wd_hbm, fg_ref, o_ref, wg, wu, wd, a_ref, sems,
                     *, final_norm):
    copies = [pltpu.make_async_copy(src, dst, sems.at[i])
              for i, (src, dst) in enumerate(((wg_hbm, wg), (wu_hbm, wu), (wd_hbm, wd)))]
    for c in copies:
        c.start()
    x = x_ref[...]
    xn = _rms(x, g_ref[...]).astype(BF16)
    copies[0].wait()
    copies[1].wait()
    for c0, cw in FF_CHUNKS:
        gate = jnp.dot(xn, wg[:, c0:c0 + cw], preferred_element_type=F32)
        up = jnp.dot(xn, wu[:, c0:c0 + cw], preferred_element_type=F32)
        a_ref[:, c0:c0 + cw] = (jax.nn.silu(gate) * up).astype(BF16)
    copies[2].wait()
    out = x + 0.5 * jnp.dot(a_ref[...], wd[...], preferred_element_type=F32)
    if final_norm:
        out = _rms(out, fg_ref[...])
    o_ref[...] = out


def _ffn_stream(x, g, w16, fg, final_norm):
    n = x.shape[0]
    vmem = pl.BlockSpec(memory_space=pltpu.VMEM)
    hbm = pl.BlockSpec(memory_space=pl.ANY)
    est = 3 * D_MODEL * D_FF * 2 + n * (D_FF * 2 + 10 * D_MODEL * 4)
    return pl.pallas_call(
        functools.partial(_ffn_stream_body, final_norm=final_norm),
        out_shape=jax.ShapeDtypeStruct((n, D_MODEL), F32),
        in_specs=[vmem, vmem, hbm, hbm, hbm, vmem],
        out_specs=vmem,
        scratch_shapes=[pltpu.VMEM((D_MODEL, D_FF), BF16), pltpu.VMEM((D_MODEL, D_FF), BF16),
                        pltpu.VMEM((D_FF, D_MODEL), BF16), pltpu.VMEM((n, D_FF), BF16),
                        pltpu.SemaphoreType.DMA((3,))],
        compiler_params=pltpu.CompilerParams(vmem_limit_bytes=_vmem_limit(est)),
        name="ffn_stream",
    )(x, g.reshape(1, D_MODEL), *w16, fg.reshape(1, D_MODEL))


ATTN_SCALE = HEAD_DIM ** -0.5


def _inproj_body(x_ref, g_ref, w_ref, q_ref, k_ref, v_ref, bg_ref, u_ref):
    h = _rms(x_ref[...], g_ref[...]).astype(BF16)
    p = jnp.dot(h, w_ref[...], preferred_element_type=F32)
    o = 0
    q_ref[...] = (p[:, o:o + ATTN_WIDTH] * ATTN_SCALE).astype(BF16)
    o += ATTN_WIDTH
    k_ref[...] = p[:, o:o + KV_WIDTH]
    o += KV_WIDTH
    v_ref[...] = p[:, o:o + KV_WIDTH]
    o += KV_WIDTH
    bg_ref[...] = p[:, o:o + CONV_WIDTH]
    o += CONV_WIDTH
    u_ref[...] = p[:, o:o + CONV_WIDTH] * p[:, o + CONV_WIDTH:o + 2 * CONV_WIDTH]


def _inproj(x, g, w_in, tm):
    n = x.shape[0]
    row = lambda w: pl.BlockSpec((tm, w), lambda i: (i, 0))
    est = 2 * D_MODEL * IN_WIDTH * 2 + 2 * tm * (D_MODEL + IN_WIDTH) * 4 + 2 * tm * IN_WIDTH * 4
    return pl.pallas_call(
        _inproj_body,
        out_shape=(jax.ShapeDtypeStruct((n, ATTN_WIDTH), BF16),
                   jax.ShapeDtypeStruct((n, KV_WIDTH), F32),
                   jax.ShapeDtypeStruct((n, KV_WIDTH), F32),
                   jax.ShapeDtypeStruct((n, CONV_WIDTH), F32),
                   jax.ShapeDtypeStruct((n, CONV_WIDTH), F32)),
        grid=(n // tm,),
        in_specs=[row(D_MODEL), pl.BlockSpec((1, D_MODEL), lambda i: (0, 0)),
                  pl.BlockSpec((D_MODEL, IN_WIDTH), lambda i: (0, 0))],
        out_specs=(row(ATTN_WIDTH), row(KV_WIDTH), row(KV_WIDTH), row(CONV_WIDTH), row(CONV_WIDTH)),
        compiler_params=pltpu.CompilerParams(
            dimension_semantics=("arbitrary",), vmem_limit_bytes=_vmem_limit(est)),
        name="inproj",
    )(x, g.reshape(1, D_MODEL), w_in)


def _mixer_ab_body(x_ref, q_ref, k_ref, v_ref, kh_ref, vh_ref, bg_ref, u_ref, uh_ref,
                   bias_ref, sink_ref, cw_ref, wo_ref, o_ref,
                   kbuf, vbuf, ubuf, mix, sbuf, pbuf, rbuf, *, nb, cl, e):
    cw_ref = cw_ref.at[e]
    low =lax.broadcasted_iota(jnp.int32, (1, KV_WIDTH), 1) < HEAD_DIM

    def stage(buf, a, r0):
        b = pltpu.roll(a, HEAD_DIM, axis=1)
        rows = a.shape[0]
        buf[0, 0, r0:r0 + rows, :] = jnp.where(low, a, 0.0).astype(BF16)
        buf[0, 1, r0:r0 + rows, :] = jnp.where(low, 0.0, b).astype(BF16)
        buf[1, 0, r0:r0 + rows, :] = jnp.where(low, b, 0.0).astype(BF16)
        buf[1, 1, r0:r0 + rows, :] = jnp.where(low, 0.0, a).astype(BF16)

    blk = PAIRS_PER_KV * cl
    blocks = [(sq, hh, lh) for sq in range(nb) for hh in range(N_KV_HEADS) for lh in range(2)]
    for sq in range(nb):
        r0, h0 = sq * cl, sq * WINDOW
        kb, vb = kbuf.at[sq], vbuf.at[sq]
        stage(kb, kh_ref[h0:h0 + WINDOW, :], 0)
        stage(kb, k_ref[r0:r0 + cl, :], WINDOW)
        stage(vb, vh_ref[h0:h0 + WINDOW, :], 0)
        stage(vb, v_ref[r0:r0 + cl, :], WINDOW)
        for hh in range(N_KV_HEADS):
            ql = jnp.concatenate(
                [q_ref[r0:r0 + cl, (hh * PAIRS_PER_KV + pr) * PAIR_WIDTH:
                       (hh * PAIRS_PER_KV + pr + 1) * PAIR_WIDTH] for pr in range(PAIRS_PER_KV)],
                axis=0)
            for lh in range(2):
                b0 = blocks.index((sq, hh, lh)) * blk
                sbuf[b0:b0 + blk, :] = lax.dot_general(
                    ql, kb[hh, lh], (((1,), (1,)), ((), ())),
                    preferred_element_type=F32) + bias_ref[hh, lh]

    s = sbuf[...]
    sk = jnp.concatenate([sink_ref[hh, lh] for _, hh, lh in blocks], axis=0)
    m = jnp.maximum(jnp.max(s, axis=-1, keepdims=True), sk)
    p = jnp.exp(s - m)
    den = jnp.sum(p, axis=-1, keepdims=True) + jnp.exp(sk - m)
    pbuf[...] = p.astype(BF16)
    rbuf[...] = 1.0 / den

    for sq in range(nb):
        r0 = sq * cl
        for hh in range(N_KV_HEADS):
            acc = None
            for lh in range(2):
                b0 = blocks.index((sq, hh, lh)) * blk
                o = jnp.dot(pbuf[b0:b0 + blk, :], vbuf[sq, hh, lh],
                            preferred_element_type=F32) * rbuf[b0:b0 + blk, :]
                acc = o if acc is None else acc + o
            for pr in range(PAIRS_PER_KV):
                c0 = (hh * PAIRS_PER_KV + pr) * PAIR_WIDTH
                mix[r0:r0 + cl, c0:c0 + PAIR_WIDTH] = acc[pr * cl:(pr + 1) * cl].astype(BF16)
        conv = _gated_conv(bg_ref[r0:r0 + cl, :], u_ref[r0:r0 + cl, :], uh_ref[sq], cw_ref,
                           ubuf.at[sq])
        mix[r0:r0 + cl, ATTN_WIDTH:MIX_WIDTH] = conv.astype(BF16)

    o_ref[...] = x_ref[...] + jnp.dot(mix[...], wo_ref[...], preferred_element_type=F32)


def _gated_conv(bg, u, uh, cw_ref, ubuf):
    rows = u.shape[0]
    ubuf[CONV_HIST_ROWS - uh.shape[0]:CONV_HIST_ROWS, :] = uh
    ubuf[CONV_HIST_ROWS:CONV_HIST_ROWS + rows, :] = u
    conv = None
    for j in range(CONV_K):
        off = CONV_HIST_ROWS - (CONV_K - 1) + j
        term = cw_ref[j:j + 1, :] * ubuf[off:off + rows, :]
        conv = term if conv is None else conv + term
    return bg * conv


def _mixer_full_body(x_ref, g_ref, win_ref, biast_ref, sinkt_ref, cw_ref, wo_ref,
                     o_ref, klast_ref, vlast_ref, ulast_ref,
                     qbuf, kdup, vta, vtb, ubuf, mix, sbuf, pbuf, rbuf, kcar, vcar, ucar, *, tq):
    b = pl.program_id(0)
    t = pl.program_id(1)
    cl = CHUNK
    kw = WINDOW + cl
    w = WINDOW + tq
    low = lax.broadcasted_iota(jnp.int32, (1, KV_WIDTH), 1) < HEAD_DIM

    @pl.when((b == 0) & (t == 0))
    def _():
        kcar[...] = jnp.zeros_like(kcar)
        vcar[...] = jnp.zeros_like(vcar)
        ucar[...] = jnp.zeros_like(ucar)

    x = x_ref[...]
    h = _rms(x, g_ref[...]).astype(BF16)

    def proj(c0, width):
        return jnp.dot(h, win_ref[:, c0:c0 + width], preferred_element_type=F32)

    qbuf[...] = (proj(0, ATTN_WIDTH) * ATTN_SCALE).astype(BF16)
    kv = proj(ATTN_WIDTH, 2 * KV_WIDTH)
    k = kv[:, 0:KV_WIDTH]
    v = kv[:, KV_WIDTH:2 * KV_WIDTH]
    c0 = ATTN_WIDTH + 2 * KV_WIDTH
    bg = proj(c0, CONV_WIDTH)
    u = proj(c0 + CONV_WIDTH, CONV_WIDTH) * proj(c0 + 2 * CONV_WIDTH, CONV_WIDTH)

    kh = kcar[...]
    vh = vcar[...]
    uh = jnp.where(t == 0, 0.0, ucar[...])
    k_tail = k[tq - WINDOW:tq, :]
    v_tail = v[tq - WINDOW:tq, :]
    u_tail = u[tq - CONV_HIST_ROWS:tq, :]
    kcar[...] = k_tail
    vcar[...] = v_tail
    ucar[...] = u_tail
    klast_ref[...] = k_tail
    vlast_ref[...] = v_tail
    ulast_ref[...] = u_tail

    kf = jnp.concatenate([kh, k], axis=0)
    kr = pltpu.roll(kf, HEAD_DIM, axis=1)
    kdup[0] = jnp.where(low, kf, kr).astype(BF16)
    kdup[1] = jnp.where(low, kr, kf).astype(BF16)

    vf = jnp.concatenate([vh, v, jnp.zeros((cl, KV_WIDTH), F32)], axis=0)
    vr = pltpu.roll(vf, HEAD_DIM, axis=1)
    for hh, vd in enumerate((jnp.where(low, vf, vr), jnp.where(low, vr, vf))):
        for blk in range(w // KV_WIDTH):
            b0 = blk * KV_WIDTH
            vta[hh, :, b0:b0 + KV_WIDTH] = vd[b0:b0 + KV_WIDTH].T.astype(BF16)
            vtb[hh, :, b0:b0 + KV_WIDTH] = vd[cl + b0:cl + b0 + KV_WIDTH].T.astype(BF16)

    n_cols = SCORE_COLS
    key = lax.broadcasted_iota(jnp.int32, (kw, n_cols), 0)
    rr = lax.broadcasted_iota(jnp.int32, (PAIR_WIDTH, PAIR_WIDTH), 0) < HEAD_DIM
    cc = lax.broadcasted_iota(jnp.int32, (PAIR_WIDTH, PAIR_WIDTH), 1) < HEAD_DIM
    diag = rr == cc
    for c in range(tq // cl):
        r0 = c * cl
        for hh in range(N_KV_HEADS):
            rows = []
            for pr in range(PAIRS_PER_KV):
                c0 = (hh * PAIRS_PER_KV + pr) * PAIR_WIDTH
                qp = qbuf[r0:r0 + cl, c0:c0 + PAIR_WIDTH]
                rows += [jnp.where(low, qp, 0), jnp.where(low, 0, qp)]
            qm = jnp.concatenate(rows, axis=0)
            s = lax.dot_general(kdup[hh, r0:r0 + kw, :], qm, (((1,), (1,)), ((), ())),
                                preferred_element_type=F32) + biast_ref[hh]
            if r0 < WINDOW:
                s = jnp.where(key + (t * tq + r0 - WINDOW) >= 0, s, NEG)
            j0 = (c * N_KV_HEADS + hh) * n_cols
            sbuf[:, j0:j0 + n_cols] = s

    s = sbuf[...]
    sk = sinkt_ref[...]
    m = jnp.maximum(jnp.max(s, axis=0, keepdims=True), sk)
    p = jnp.exp(s - m)
    den = jnp.sum(p, axis=0, keepdims=True) + jnp.exp(sk - m)
    pbuf[...] = p.astype(BF16)
    rbuf[...] = 1.0 / den

    for c in range(tq // cl):
        r0 = c * cl
        k0 = r0 if c % 2 == 0 else r0 - cl
        vt_ref = vta if c % 2 == 0 else vtb
        for hh in range(N_KV_HEADS):
            j0 = (c * N_KV_HEADS + hh) * n_cols
            ot = jnp.dot(vt_ref[hh, :, k0:k0 + kw], pbuf[:, j0:j0 + n_cols],
                         preferred_element_type=F32) * rbuf[:, j0:j0 + n_cols]
            for pr in range(PAIRS_PER_KV):
                blk = jnp.where(diag, ot[:, pr * PAIR_WIDTH:(pr + 1) * PAIR_WIDTH], 0.0).T
                c0 = (hh * PAIRS_PER_KV + pr) * PAIR_WIDTH
                mix[r0:r0 + cl, c0:c0 + PAIR_WIDTH] = (blk[0:cl] + blk[cl:2 * cl]).astype(BF16)

    mix[:, ATTN_WIDTH:MIX_WIDTH] = _gated_conv(bg, u, uh, cw_ref, ubuf).astype(BF16)
    o_ref[...] = x + jnp.dot(mix[...], wo_ref[...], preferred_element_type=F32)


def _mixer_full(x, g, w_in, bias_l, sink_l, conv_w, w_out, e, *, nb, t_len, tq, cast=()):
    n = x.shape[0]
    nt = t_len // tq
    jobs = [_CastJob(a, lead, nb * nt) for a, lead in cast]
    job_specs = [job.specs(lambda b, t: b * nt + t) for job in jobs]
    cl = CHUNK
    kw = WINDOW + cl
    assert tq % KV_WIDTH == 0 and tq >= WINDOW
    all_cols = (tq // cl) * N_KV_HEADS * SCORE_COLS
    assert sink_l.shape == (1, all_cols)
    row = pl.BlockSpec((tq, D_MODEL), lambda b, t: (b * nt + t, 0))
    const = lambda a: pl.BlockSpec(a.shape, lambda b, t: (0,) * a.ndim)
    per_seq = lambda rows, width: pl.BlockSpec((rows, width), lambda b, t: (b, 0))
    est = 4 * tq * D_MODEL * 4 + (D_MODEL * IN_WIDTH + MIX_WIDTH * D_MODEL) * 2 \
        + kw * all_cols * 6 + 16 * tq * D_MODEL * 4
    whole = lambda a: pl.BlockSpec(a.shape, lambda b, t: (0, 0), pipeline_mode=pl.Buffered(1))
    return pl.pallas_call(
        _host_casts(functools.partial(_mixer_full_body, tq=tq), 7, 4, jobs),
        out_shape=(jax.ShapeDtypeStruct((n, D_MODEL), F32),
                   jax.ShapeDtypeStruct((nb * WINDOW, KV_WIDTH), F32),
                   jax.ShapeDtypeStruct((nb * WINDOW, KV_WIDTH), F32),
                   jax.ShapeDtypeStruct((nb * CONV_HIST_ROWS, CONV_WIDTH), F32))
        + tuple(job.out_shape for job in jobs),
        grid=(nb, nt),
        in_specs=[row, pl.BlockSpec((1, D_MODEL), lambda b, t: (0, 0)), whole(w_in),
                  const(bias_l), const(sink_l),
                  pl.BlockSpec((None, CONV_K, CONV_WIDTH), lambda b, t: (e, 0, 0)),
                  whole(w_out)] + [s[0] for s in job_specs],
        out_specs=(row, per_seq(WINDOW, KV_WIDTH), per_seq(WINDOW, KV_WIDTH),
                   per_seq(CONV_HIST_ROWS, CONV_WIDTH)) + tuple(s[1] for s in job_specs),
        scratch_shapes=[pltpu.VMEM((tq, ATTN_WIDTH), BF16),
                        pltpu.VMEM((N_KV_HEADS, WINDOW + tq, KV_WIDTH), BF16),
                        pltpu.VMEM((N_KV_HEADS, PAIR_WIDTH, WINDOW + tq), BF16),
                        pltpu.VMEM((N_KV_HEADS, PAIR_WIDTH, WINDOW + tq), BF16),
                        pltpu.VMEM((CONV_HIST_ROWS + tq, CONV_WIDTH), F32),
                        pltpu.VMEM((tq, MIX_WIDTH), BF16),
                        pltpu.VMEM((kw, all_cols), F32), pltpu.VMEM((kw, all_cols), BF16),
                        pltpu.VMEM((1, all_cols), F32),
                        pltpu.VMEM((WINDOW, KV_WIDTH), F32), pltpu.VMEM((WINDOW, KV_WIDTH), F32),
                        pltpu.VMEM((CONV_HIST_ROWS, CONV_WIDTH), F32)],
        compiler_params=pltpu.CompilerParams(
            dimension_semantics=("arbitrary", "arbitrary"), vmem_limit_bytes=_vmem_limit(est)),
        name="mixer_full",
    )(x, g.reshape(1, D_MODEL), w_in, bias_l, sink_l, conv_w, w_out, *[job.a for job in jobs])


def _mixer_ab(x, q, k, v, k_hist, v_hist, bg, u, u_hist, bias_l, sink_l, conv_w, w_out, e,
              *, nb, t_len):
    n = x.shape[0]
    cl = t_len
    assert t_len <= CHUNK and t_len % BF16_SUBLANES == 0
    kw = WINDOW + cl
    vmem = pl.BlockSpec(memory_space=pltpu.VMEM)
    score_rows = nb * N_KV_HEADS * 2 * PAIRS_PER_KV * cl
    return pl.pallas_call(
        functools.partial(_mixer_ab_body, nb=nb, cl=cl, e=e),
        out_shape=jax.ShapeDtypeStruct((n, D_MODEL), F32),
        in_specs=[vmem] * 13,
        out_specs=vmem,
        scratch_shapes=[pltpu.VMEM((nb, N_KV_HEADS, 2, kw, KV_WIDTH), BF16),
                        pltpu.VMEM((nb, N_KV_HEADS, 2, kw, KV_WIDTH), BF16),
                        pltpu.VMEM((nb, CONV_HIST_ROWS + cl, CONV_WIDTH), F32),
                        pltpu.VMEM((n, MIX_WIDTH), BF16),
                        pltpu.VMEM((score_rows, kw), F32), pltpu.VMEM((score_rows, kw), BF16),
                        pltpu.VMEM((score_rows, 1), F32)],
        name="mixer_ab",
    )(x, q, k, v, k_hist, v_hist, bg, u, u_hist, bias_l, sink_l, conv_w, w_out)


def _pool_rows(x, hist, g, pw_ref, sc_ref, o_ref, r0, pos0):
    rows = x.shape[0]
    h = _rms(x, g)
    sums = []
    cur = jnp.concatenate([hist, h], axis=0)
    for gi, w in enumerate(POOL_SIZES):
        assert w == 2 ** (gi + 1)
        cur = cur + pltpu.roll(cur, w // 2, axis=0)
        sums.append(cur[POOL_HIST_ROWS:POOL_HIST_ROWS + rows, 0:POOL_GROUP])
        if gi + 1 < len(POOL_SIZES):
            cur = cur[:, POOL_GROUP:]

    pos = pos0 + lax.broadcasted_iota(jnp.int32, (rows, 1), 0)
    for gi, w in enumerate(POOL_SIZES):
        sl = slice(gi * POOL_GROUP, (gi + 1) * POOL_GROUP)
        cnt = jnp.minimum(pos + 1, w).astype(F32)
        d = (sums[gi] / cnt - h[:, sl]).astype(BF16)
        y = jnp.dot(d, pw_ref[gi], preferred_element_type=F32)
        o_ref[r0:r0 + rows, sl] = x[:, sl] + y * sc_ref[:, sl]
    return h[rows - POOL_HIST_ROWS:rows, :]


def _mixer_c_body(x_ref, hist_ref, g_ref, pw_ref, sc_ref, o_ref, st_ref, hbuf,
                  *, nb, t_len, first_pos):
    n_hist = hist_ref.shape[1]
    pad = POOL_HIST_ROWS - n_hist
    for sq in range(nb):
        r0, h0 = sq * t_len, sq * POOL_HIST_ROWS
        hb = hbuf.at[sq]
        hb[0:pad, :] = jnp.zeros((pad, D_MODEL), F32)
        hb[pad:POOL_HIST_ROWS, :] = hist_ref[sq]
        st_ref[h0:h0 + POOL_HIST_ROWS, :] = _pool_rows(
            x_ref[r0:r0 + t_len, :], hb[...], g_ref[...], pw_ref, sc_ref, o_ref, r0, first_pos)


def _pool_specs(o):
    return [pl.BlockSpec((None, len(POOL_SIZES), POOL_GROUP, POOL_GROUP), lambda *a: (o, 0, 0, 0)),
            pl.BlockSpec((None, 1, D_MODEL), lambda *a: (o, 0, 0))]


def _mixer_c(x, hist, g, pool_w, scale, o, *, nb, t_len, first_pos):
    n = x.shape[0]
    whole = lambda a: pl.BlockSpec(a.shape, lambda i: (0,) * a.ndim)
    g2 = g.reshape(1, D_MODEL)
    st_shape = jax.ShapeDtypeStruct((nb * POOL_HIST_ROWS, D_MODEL), F32)
    return pl.pallas_call(
        functools.partial(_mixer_c_body, nb=nb, t_len=t_len, first_pos=first_pos),
        out_shape=(jax.ShapeDtypeStruct((n, D_MODEL), F32), st_shape),
        grid=(1,),
        in_specs=[whole(x), whole(hist), whole(g2)] + _pool_specs(o),
        out_specs=(whole(x), whole(st_shape)),
        scratch_shapes=[pltpu.VMEM((nb, POOL_HIST_ROWS, D_MODEL), F32)],
        compiler_params=pltpu.CompilerParams(dimension_semantics=("arbitrary",)),
        name="mixer_c",
    )(x, hist, g2, pool_w, scale.reshape(scale.shape[0], 1, D_MODEL))


def _ffn_pool_body(x_ref, g1_ref, wg_ref, wu_ref, wd_ref, g2_ref, pw_ref, sc_ref,
                   o_ref, st_ref, a_ref, carry_ref, *, tm, sub, tiles_per_seq):
    i = pl.program_id(0)
    t = i % tiles_per_seq

    @pl.when(i == 0)
    def _():
        carry_ref[...] = jnp.zeros_like(carry_ref)

    hist = jnp.where(t == 0, 0.0, carry_ref[...])
    g1 = g1_ref[...]
    g2 = g2_ref[...]
    blocks = [_swiglu_rows(x_ref[r0:r0 + sub, :], g1, wg_ref, wu_ref, wd_ref, a_ref, r0)
              for r0 in range(0, tm, sub)]
    for s, xb in enumerate(blocks):
        hist = _pool_rows(xb, hist, g2, pw_ref, sc_ref, o_ref, s * sub, t * tm + s * sub)
    carry_ref[...] = hist
    st_ref[...] = hist


def _ffn_pool(x, g1, w16, g2, pool_w, scale, o, *, nb, t_len, tm, sub):
    n = x.shape[0]
    tiles_per_seq = t_len // tm
    whole = lambda a: pl.BlockSpec(a.shape, lambda i: (0, 0), pipeline_mode=pl.Buffered(1))
    vec = pl.BlockSpec((1, D_MODEL), lambda i: (0, 0))
    row = pl.BlockSpec((tm, D_MODEL), lambda i: (i, 0))
    est = 3 * D_MODEL * D_FF * 2 + 4 * tm * D_MODEL * 4 + tm * D_FF * 2 + 6 * tm * 1024 * 4
    return pl.pallas_call(
        functools.partial(_ffn_pool_body, tm=tm, sub=sub, tiles_per_seq=tiles_per_seq),
        out_shape=(jax.ShapeDtypeStruct((n, D_MODEL), F32),
                   jax.ShapeDtypeStruct((nb * POOL_HIST_ROWS, D_MODEL), F32)),
        grid=(n // tm,),
        in_specs=[row, vec, whole(w16[0]), whole(w16[1]), whole(w16[2]), vec] + _pool_specs(o),
        out_specs=(row, pl.BlockSpec((POOL_HIST_ROWS, D_MODEL), lambda i: (i // tiles_per_seq, 0))),
        scratch_shapes=[pltpu.VMEM((tm, D_FF), BF16), pltpu.VMEM((POOL_HIST_ROWS, D_MODEL), F32)],
        compiler_params=pltpu.CompilerParams(
            dimension_semantics=("arbitrary",), vmem_limit_bytes=_vmem_limit(est)),
        name="ffn_pool",
    )(x, g1.reshape(1, D_MODEL), *w16, g2.reshape(1, D_MODEL), pool_w,
      scale.reshape(scale.shape[0], 1, D_MODEL))


def kernel(x_prompt, x_sample, cache_attn_k, cache_attn_v, cache_conv, cache_pool, norm_g, ffn_w_gate, ffn_w_up, ffn_w_down, mix_w_in, mix_w_out, conv_w, attn_sinks, rel_bias_table, pool_w, pool_scale, final_norm_g):
    past_len = 1024
    nb, t_len, _ = x_prompt.shape
    nbs, ts_len, _ = x_sample.shape
    xp = x_prompt.reshape(nb * t_len, D_MODEL)
    xs = x_sample.reshape(nbs * ts_len, D_MODEL)
    tf = min(FFN_ROW_TILE, xp.shape[0])
    tq = min(MIXER_ROW_TILE, t_len)
    assert t_len % tf == 0
    ts = xs.shape[0]
    pw = pool_w.astype(BF16)
    assert ts_len <= CHUNK
    ffn_w = (ffn_w_gate, ffn_w_up, ffn_w_down)

    w16 = {0: tuple(a[0, 0].astype(BF16) for a in ffn_w)}
    pending = list(range(1, 2 * DEPTH))
    mix16 = {}

    def ffn_weights(n):
        if n not in w16:
            pending.remove(n)
            w16[n] = tuple(a[n // 2, n % 2].astype(BF16) for a in ffn_w)
        return w16[n]

    def next_casts(n):
        if not pending or pending[0] <= n:
            return None, []
        m = pending.pop(0)
        return m, [(a, (m // 2, m % 2)) for a in ffn_w]

    def prompt_ffn(x, layer, j, final):
        n = 2 * layer + j
        m, cast = next_casts(n)
        if n == 0:
            cast = cast + [(a, (e,)) for e in range(mix_w_in.shape[0]) for a in (mix_w_in, mix_w_out)]
        y, *done = _ffn(x, norm_g[layer, 2 * j], ffn_weights(n), final_norm_g, final, tf, cast=cast)
        if m is not None:
            w16[m] = tuple(done[:3])
            done = done[3:]
        for e in range(len(done) // 2):
            mix16[e] = (done[2 * e], done[2 * e + 1])
        return y

    def sample_ffn(x, layer, j, final):
        return _ffn_stream(x, norm_g[layer, 2 * j], ffn_weights(2 * layer + j), final_norm_g, final)

    def last_rows(a, seqs, rows, keep, tail_shape):
        return a.reshape((seqs, rows) + tail_shape)[:, rows - keep:]

    outs = {name: [] for name in ("kp", "vp", "cp", "pp", "ks", "vs", "cs", "ps")}
    for layer in range(DEPTH):
        last = layer == DEPTH - 1
        if layer % 2 == 0:
            e = layer // 2
            xp = prompt_ffn(xp, layer, 0, False)
            xs = sample_ffn(xs, layer, 0, False)
            if e not in mix16:
                mix16[e] = (mix_w_in[e].astype(BF16), mix_w_out[e].astype(BF16))
            w_in, w_out = mix16[e]
            m, cast = next_casts(2 * layer)
            bias_t, sink_t, bias_r, sink_r = _attn_tables(rel_bias_table, attn_sinks[e], ts_len,
                                                          tq // CHUNK)
            xp, k, v, u, *done = _mixer_full(xp, norm_g[layer, 1], w_in, bias_t, sink_t, conv_w,
                                             w_out, e, nb=nb, t_len=t_len, tq=tq, cast=cast)
            if m is not None:
                w16[m] = tuple(done)
            keep = min(WINDOW, t_len)
            outs["kp"].append(last_rows(k, nb, WINDOW, keep, (N_KV_HEADS, HEAD_DIM)))
            outs["vp"].append(last_rows(v, nb, WINDOW, keep, (N_KV_HEADS, HEAD_DIM)))
            outs["cp"].append(last_rows(u, nb, CONV_HIST_ROWS, CONV_K - 1, (CONV_WIDTH,)))
            q, k, v, bg, u = _inproj(xs, norm_g[layer, 1], w_in, min(ROW_TILE, ts))
            k_hist = cache_attn_k[e].reshape(nbs * WINDOW, KV_WIDTH)
            v_hist = cache_attn_v[e].reshape(nbs * WINDOW, KV_WIDTH)
            xs = _mixer_ab(xs, q, k, v, k_hist, v_hist, bg, u, cache_conv[e], bias_r, sink_r,
                           conv_w, w_out, e, nb=nbs, t_len=ts_len)
            keep = min(WINDOW, ts_len)
            outs["ks"].append(last_rows(k, nbs, ts_len, keep, (N_KV_HEADS, HEAD_DIM)))
            outs["vs"].append(last_rows(v, nbs, ts_len, keep, (N_KV_HEADS, HEAD_DIM)))
            outs["cs"].append(last_rows(u, nbs, ts_len, CONV_K - 1, (CONV_WIDTH,)))
        else:
            o = layer // 2
            xp, st = _ffn_pool(xp, norm_g[layer, 0], ffn_weights(2 * layer), norm_g[layer, 1],
                               pw, pool_scale, o, nb=nb, t_len=t_len, tm=tf,
                               sub=tf // POOL_SUB_BLOCKS)
            outs["pp"].append(last_rows(st, nb, POOL_HIST_ROWS, POOL_MAX - 1, (D_MODEL,)))
            xs = sample_ffn(xs, layer, 0, False)
            xs, st = _mixer_c(xs, cache_pool[o], norm_g[layer, 1], pw, pool_scale, o,
                              nb=nbs, t_len=ts_len, first_pos=past_len)
            outs["ps"].append(last_rows(st, nbs, POOL_HIST_ROWS, POOL_MAX - 1, (D_MODEL,)))
        xp = prompt_ffn(xp, layer, 1, last)
        xs = sample_ffn(xs, layer, 1, last)
    stacked = {name: jnp.stack(v) for name, v in outs.items()}
    return (xp.reshape(nb, t_len, D_MODEL), xs.reshape(nbs, ts_len, D_MODEL),
            stacked["kp"], stacked["vp"], stacked["cp"], stacked["pp"],
            stacked["ks"], stacked["vs"], stacked["cs"], stacked["ps"])
```

```python
import functools
import math

import jax
import jax.numpy as jnp
from jax import lax
from jax.experimental import pallas as pl
from jax.experimental.pallas import tpu as pltpu

D_MODEL = 1024
DEPTH = 2
CHUNK = 64
N_HEADS = 8
N_KV_HEADS = 2
Q_PER_KV = N_HEADS // N_KV_HEADS
HEAD_DIM = 64
ATTN_WIDTH = N_HEADS * HEAD_DIM
KV_WIDTH = N_KV_HEADS * HEAD_DIM
WINDOW = 128
N_BUCKETS = 32
MAX_DISTANCE = 128
CONV_WIDTH = D_MODEL // 2
CONV_K = 3
IN_WIDTH = ATTN_WIDTH + 2 * KV_WIDTH + 3 * CONV_WIDTH
MIX_WIDTH = ATTN_WIDTH + CONV_WIDTH
POOL_SIZES = (2, 4, 8, 16)
POOL_GROUP = D_MODEL // len(POOL_SIZES)
POOL_MAX = max(POOL_SIZES)
D_FF = 2816
EPS = 1e-6
NEG = -1e30

V7X_VMEM_BYTES = 64 * 1024 * 1024
SUBLANES_F32 = 8
BF16_SUBLANES = 16
LANES = 128
CONV_HIST_ROWS = SUBLANES_F32
POOL_HIST_ROWS = 16
ROW_TILE = 512
MIXER_ROW_TILE = 1024
FFN_ROW_TILE = 1024
FF_CHUNK = 256
FF_CHUNKS = tuple((c0, min(FF_CHUNK, D_FF - c0)) for c0 in range(0, D_FF, FF_CHUNK))
POOL_SUB_BLOCKS = 4

F32 = jnp.float32
BF16 = jnp.bfloat16


def _vmem_limit(nbytes):
    return int(min(V7X_VMEM_BYTES - (4 << 20), max(32 << 20, nbytes)))


def _rms(x, g):
    return x * lax.rsqrt(jnp.mean(x * x, axis=-1, keepdims=True) + EPS) * g


PAIR_WIDTH = 2 * HEAD_DIM
PAIRS_PER_KV = Q_PER_KV // 2
SCORE_COLS = PAIRS_PER_KV * PAIR_WIDTH


def _attn_tables_body(tab_ref, sink_ref, bkt_t_ref, bkt_r_ref,
                      bias_t_ref, sink_t_ref, bias_r_ref, sink_r_ref, *, cl_s, n_chunks):
    col_head = lax.broadcasted_iota(jnp.int32, (1, SCORE_COLS), 1) // HEAD_DIM
    row_pair = lax.broadcasted_iota(jnp.int32, (PAIRS_PER_KV * cl_s, 1), 0) // cl_s

    def per_column(hh, value):
        out = jnp.full((1, SCORE_COLS), value(hh * Q_PER_KV), F32)
        for g in range(1, Q_PER_KV):
            out = jnp.where(col_head == g, value(hh * Q_PER_KV + g), out)
        return out

    def per_row(hh, lh, value):
        out = jnp.full((PAIRS_PER_KV * cl_s, 1), value(hh * Q_PER_KV + lh), F32)
        for pr in range(1, PAIRS_PER_KV):
            out = jnp.where(row_pair == pr, value(hh * Q_PER_KV + 2 * pr + lh), out)
        return out

    bkt_t = bkt_t_ref[...]
    bkt_r = bkt_r_ref[...]
    for hh in range(N_KV_HEADS):
        acc = jnp.zeros(bkt_t.shape, F32)
        for b in range(N_BUCKETS):
            acc = jnp.where(bkt_t == b, per_column(hh, lambda h: tab_ref[b, h]), acc)
        bias_t_ref[hh] = acc
        sink_row = per_column(hh, lambda h: sink_ref[h])
        for c in range(n_chunks):
            j0 = (c * N_KV_HEADS + hh) * SCORE_COLS
            sink_t_ref[:, j0:j0 + SCORE_COLS] = sink_row
        for lh in range(2):
            acc = jnp.zeros(bkt_r.shape, F32)
            for b in range(N_BUCKETS):
                acc = jnp.where(bkt_r == b, per_row(hh, lh, lambda h: tab_ref[b, h]), acc)
            bias_r_ref[hh, lh] = acc
            sink_r_ref[hh, lh] = per_row(hh, lh, lambda h: sink_ref[h])


def _rel_bucket(rel):
    half = N_BUCKETS // 2
    ret = jnp.where(rel > 0, half, 0)
    n = jnp.abs(rel)
    max_exact = half // 2
    nf = jnp.maximum(n, 1).astype(F32)
    large = max_exact + (jnp.log(nf / max_exact) / math.log(MAX_DISTANCE / max_exact)
                         * (half - max_exact)).astype(jnp.int32)
    large = jnp.minimum(large, half - 1)
    return ret + jnp.where(n < max_exact, n, large)


def _attn_tables(table, sinks, cl_s, n_chunks):
    rel = (jnp.arange(WINDOW + CHUNK, dtype=jnp.int32) - WINDOW)[None, :] \
        - jnp.arange(CHUNK, dtype=jnp.int32)[:, None]
    bucket = _rel_bucket(rel).astype(jnp.int32)
    bkt_t = jnp.tile(bucket.T, (1, Q_PER_KV))
    kw_s = WINDOW + cl_s
    bkt_r = jnp.tile(bucket[:cl_s, :kw_s], (PAIRS_PER_KV, 1))
    smem = pl.BlockSpec(memory_space=pltpu.SMEM)
    vmem = pl.BlockSpec(memory_space=pltpu.VMEM)
    return pl.pallas_call(
        functools.partial(_attn_tables_body, cl_s=cl_s, n_chunks=n_chunks),
        out_shape=(jax.ShapeDtypeStruct((N_KV_HEADS, WINDOW + CHUNK, SCORE_COLS), F32),
                   jax.ShapeDtypeStruct((1, n_chunks * N_KV_HEADS * SCORE_COLS), F32),
                   jax.ShapeDtypeStruct((N_KV_HEADS, 2, PAIRS_PER_KV * cl_s, kw_s), F32),
                   jax.ShapeDtypeStruct((N_KV_HEADS, 2, PAIRS_PER_KV * cl_s, 1), F32)),
        in_specs=[smem, smem, vmem, vmem],
        out_specs=(vmem, vmem, vmem, vmem),
        name="attn_tables",
    )(table, sinks, bkt_t, bkt_r)


def _swiglu_rows(x, g, wg_ref, wu_ref, wd_ref, a_ref, r0):
    rows = x.shape[0]
    xn = _rms(x, g).astype(BF16)
    for c0, cw in FF_CHUNKS:
        gate = jnp.dot(xn, wg_ref[:, c0:c0 + cw], preferred_element_type=F32)
        up = jnp.dot(xn, wu_ref[:, c0:c0 + cw], preferred_element_type=F32)
        a_ref[r0:r0 + rows, c0:c0 + cw] = (jax.nn.silu(gate) * up).astype(BF16)
    y = jnp.dot(a_ref[r0:r0 + rows, :], wd_ref[...], preferred_element_type=F32)
    return x + 0.5 * y


class _CastJob:
    def __init__(self, a, lead, steps):
        self.a, self.lead = a, tuple(lead)
        rows, cols = a.shape[-2:]
        self.chunk = next(c for c in range(BF16_SUBLANES, rows + 1, BF16_SUBLANES)
                          if rows % c == 0 and steps % (rows // c) == 0)
        self.repeat = steps // (rows // self.chunk)
        self.out_shape = jax.ShapeDtypeStruct((rows, cols), BF16)

    def specs(self, step_of):
        none = (None,) * len(self.lead)
        cols = self.a.shape[-1]
        return (pl.BlockSpec(none + (self.chunk, cols),
                             lambda *g: self.lead + (step_of(*g) // self.repeat, 0)),
                pl.BlockSpec((self.chunk, cols), lambda *g: (step_of(*g) // self.repeat, 0)))


def _host_casts(body, n_in, n_out, jobs):
    k = len(jobs)

    def hosted(*refs):
        ins, cast_in = refs[:n_in], refs[n_in:n_in + k]
        outs, cast_out = refs[n_in + k:n_in + k + n_out], refs[n_in + k + n_out:n_in + 2 * k + n_out]
        for src, dst in zip(cast_in, cast_out):
            dst[...] = src[...].astype(BF16)
        body(*ins, *outs, *refs[n_in + 2 * k + n_out:])
    return hosted


def _ffn_body(x_ref, g_ref, wg_ref, wu_ref, wd_ref, fg_ref, o_ref, a_ref, *, final_norm):
    out = _swiglu_rows(x_ref[...], g_ref[...], wg_ref, wu_ref, wd_ref, a_ref, 0)
    if final_norm:
        out = _rms(out, fg_ref[...])
    o_ref[...] = out


def _ffn(x, g, w16, fg, final_norm, tm, cast=()):
    n = x.shape[0]
    steps = n // tm
    jobs = [_CastJob(a, lead, steps) for a, lead in cast]
    job_specs = [job.specs(lambda i: i) for job in jobs]
    whole = lambda a: pl.BlockSpec(a.shape, lambda i: (0, 0), pipeline_mode=pl.Buffered(1))
    vec = pl.BlockSpec((1, D_MODEL), lambda i: (0, 0))
    row = pl.BlockSpec((tm, D_MODEL), lambda i: (i, 0))
    est = 3 * D_MODEL * D_FF * 2 + 4 * tm * D_MODEL * 4 + tm * D_FF * 2 + 5 * tm * 1024 * 4 \
        + sum(6 * job.chunk * job.a.shape[-1] * 2 for job in jobs)
    return pl.pallas_call(
        _host_casts(functools.partial(_ffn_body, final_norm=final_norm), 6, 1, jobs),
        out_shape=(jax.ShapeDtypeStruct((n, D_MODEL), F32),) + tuple(job.out_shape for job in jobs),
        grid=(steps,),
        in_specs=[row, vec, whole(w16[0]), whole(w16[1]), whole(w16[2]), vec] + [s[0] for s in job_specs],
        out_specs=(row,) + tuple(s[1] for s in job_specs),
        scratch_shapes=[pltpu.VMEM((tm, D_FF), BF16)],
        compiler_params=pltpu.CompilerParams(
            dimension_semantics=("arbitrary",), vmem_limit_bytes=_vmem_limit(est)),
        name="ffn",
    )(x, g.reshape(1, D_MODEL), *w16, fg.reshape(1, D_MODEL), *[job.a for job in jobs])


def _ffn_stream_body(x_ref, g_ref, wg_hbm, wu_hbm, wd_hbm, fg_ref, o_ref, wg, wu, wd, a_ref, sems,
                     *, final_norm):
    copies = [pltpu.make_async_copy(src, dst, sems.at[i])
              for i, (src, dst) in enumerate(((wg_hbm, wg), (wu_hbm, wu), (wd_hbm, wd)))]
    for c in copies:
        c.start()
    x = x_ref[...]
    xn = _rms(x, g_ref[...]).astype(BF16)
    copies[0].wait()
    copies[1].wait()
    for c0, cw in FF_CHUNKS:
        gate = jnp.dot(xn, wg[:, c0:c0 + cw], preferred_element_type=F32)
        up = jnp.dot(xn, wu[:, c0:c0 + cw], preferred_element_type=F32)
        a_ref[:, c0:c0 + cw] = (jax.nn.silu(gate) * up).astype(BF16)
    copies[2].wait()
    out = x + 0.5 * jnp.dot(a_ref[...], wd[...], preferred_element_type=F32)
    if final_norm:
        out = _rms(out, fg_ref[...])
    o_ref[...] = out


def _ffn_stream(x, g, w16, fg, final_norm):
    n = x.shape[0]
    vmem = pl.BlockSpec(memory_space=pltpu.VMEM)
    hbm = pl.BlockSpec(memory_space=pltpu.HBM)
    est = 3 * D_MODEL * D_FF * 2 + n * (D_FF * 2 + 10 * D_MODEL * 4)
    return pl.pallas_call(
        functools.partial(_ffn_stream_body, final_norm=final_norm),
        out_shape=jax.ShapeDtypeStruct((n, D_MODEL), F32),
        in_specs=[vmem, vmem, hbm, hbm, hbm, vmem],
        out_specs=vmem,
        scratch_shapes=[pltpu.VMEM((D_MODEL, D_FF), BF16), pltpu.VMEM((D_MODEL, D_FF), BF16),
                        pltpu.VMEM((D_FF, D_MODEL), BF16), pltpu.VMEM((n, D_FF), BF16),
                        pltpu.SemaphoreType.DMA((3,))],
        compiler_params=pltpu.CompilerParams(vmem_limit_bytes=_vmem_limit(est)),
        name="ffn_stream",
    )(x, g.reshape(1, D_MODEL), *w16, fg.reshape(1, D_MODEL))


ATTN_SCALE = HEAD_DIM ** -0.5


def _inproj_body(x_ref, g_ref, w_ref, q_ref, k_ref, v_ref, bg_ref, u_ref):
    h = _rms(x_ref[...], g_ref[...]).astype(BF16)
    p = jnp.dot(h, w_ref[...], preferred_element_type=F32)
    o = 0
    q_ref[...] = (p[:, o:o + ATTN_WIDTH] * ATTN_SCALE).astype(BF16)
    o += ATTN_WIDTH
    k_ref[...] = p[:, o:o + KV_WIDTH]
    o += KV_WIDTH
    v_ref[...] = p[:, o:o + KV_WIDTH]
    o += KV_WIDTH
    bg_ref[...] = p[:, o:o + CONV_WIDTH]
    o += CONV_WIDTH
    u_ref[...] = p[:, o:o + CONV_WIDTH] * p[:, o + CONV_WIDTH:o + 2 * CONV_WIDTH]


def _inproj(x, g, w_in, tm):
    n = x.shape[0]
    row = lambda w: pl.BlockSpec((tm, w), lambda i: (i, 0))
    est = 2 * D_MODEL * IN_WIDTH * 2 + 2 * tm * (D_MODEL + IN_WIDTH) * 4 + 2 * tm * IN_WIDTH * 4
    return pl.pallas_call(
        _inproj_body,
        out_shape=(jax.ShapeDtypeStruct((n, ATTN_WIDTH), BF16),
                   jax.ShapeDtypeStruct((n, KV_WIDTH), F32),
                   jax.ShapeDtypeStruct((n, KV_WIDTH), F32),
                   jax.ShapeDtypeStruct((n, CONV_WIDTH), F32),
                   jax.ShapeDtypeStruct((n, CONV_WIDTH), F32)),
        grid=(n // tm,),
        in_specs=[row(D_MODEL), pl.BlockSpec((1, D_MODEL), lambda i: (0, 0)),
                  pl.BlockSpec((D_MODEL, IN_WIDTH), lambda i: (0, 0))],
        out_specs=(row(ATTN_WIDTH), row(KV_WIDTH), row(KV_WIDTH), row(CONV_WIDTH), row(CONV_WIDTH)),
        compiler_params=pltpu.CompilerParams(
            dimension_semantics=("arbitrary",), vmem_limit_bytes=_vmem_limit(est)),
        name="inproj",
    )(x, g.reshape(1, D_MODEL), w_in)


def _mixer_ab_body(x_ref, q_ref, k_ref, v_ref, kh_ref, vh_ref, bg_ref, u_ref, uh_ref,
                   bias_ref, sink_ref, cw_ref, wo_ref, o_ref,
                   kbuf, vbuf, ubuf, mix, sbuf, pbuf, rbuf, *, nb, cl, e):
    cw_ref = cw_ref.at[e]
    low =lax.broadcasted_iota(jnp.int32, (1, KV_WIDTH), 1) < HEAD_DIM

    def stage(buf, a, r0):
        b = pltpu.roll(a, HEAD_DIM, axis=1)
        rows = a.shape[0]
        buf[0, 0, r0:r0 + rows, :] = jnp.where(low, a, 0.0).astype(BF16)
        buf[0, 1, r0:r0 + rows, :] = jnp.where(low, 0.0, b).astype(BF16)
        buf[1, 0, r0:r0 + rows, :] = jnp.where(low, b, 0.0).astype(BF16)
        buf[1, 1, r0:r0 + rows, :] = jnp.where(low, 0.0, a).astype(BF16)

    blk = PAIRS_PER_KV * cl
    blocks = [(sq, hh, lh) for sq in range(nb) for hh in range(N_KV_HEADS) for lh in range(2)]
    for sq in range(nb):
        r0, h0 = sq * cl, sq * WINDOW
        kb, vb = kbuf.at[sq], vbuf.at[sq]
        stage(kb, kh_ref[h0:h0 + WINDOW, :], 0)
        stage(kb, k_ref[r0:r0 + cl, :], WINDOW)
        stage(vb, vh_ref[h0:h0 + WINDOW, :], 0)
        stage(vb, v_ref[r0:r0 + cl, :], WINDOW)
        for hh in range(N_KV_HEADS):
            ql = jnp.concatenate(
                [q_ref[r0:r0 + cl, (hh * PAIRS_PER_KV + pr) * PAIR_WIDTH:
                       (hh * PAIRS_PER_KV + pr + 1) * PAIR_WIDTH] for pr in range(PAIRS_PER_KV)],
                axis=0)
            for lh in range(2):
                b0 = blocks.index((sq, hh, lh)) * blk
                sbuf[b0:b0 + blk, :] = lax.dot_general(
                    ql, kb[hh, lh], (((1,), (1,)), ((), ())),
                    preferred_element_type=F32) + bias_ref[hh, lh]

    s = sbuf[...]
    sk = jnp.concatenate([sink_ref[hh, lh] for _, hh, lh in blocks], axis=0)
    m = jnp.maximum(jnp.max(s, axis=-1, keepdims=True), sk)
    p = jnp.exp(s - m)
    den = jnp.sum(p, axis=-1, keepdims=True) + jnp.exp(sk - m)
    pbuf[...] = p.astype(BF16)
    rbuf[...] = 1.0 / den

    for sq in range(nb):
        r0 = sq * cl
        for hh in range(N_KV_HEADS):
            acc = None
            for lh in range(2):
                b0 = blocks.index((sq, hh, lh)) * blk
                o = jnp.dot(pbuf[b0:b0 + blk, :], vbuf[sq, hh, lh],
                            preferred_element_type=F32) * rbuf[b0:b0 + blk, :]
                acc = o if acc is None else acc + o
            for pr in range(PAIRS_PER_KV):
                c0 = (hh * PAIRS_PER_KV + pr) * PAIR_WIDTH
                mix[r0:r0 + cl, c0:c0 + PAIR_WIDTH] = acc[pr * cl:(pr + 1) * cl].astype(BF16)
        conv = _gated_conv(bg_ref[r0:r0 + cl, :], u_ref[r0:r0 + cl, :], uh_ref[sq], cw_ref,
                           ubuf.at[sq])
        mix[r0:r0 + cl, ATTN_WIDTH:MIX_WIDTH] = conv.astype(BF16)

    o_ref[...] = x_ref[...] + jnp.dot(mix[...], wo_ref[...], preferred_element_type=F32)


def _gated_conv(bg, u, uh, cw_ref, ubuf):
    rows = u.shape[0]
    ubuf[CONV_HIST_ROWS - uh.shape[0]:CONV_HIST_ROWS, :] = uh
    ubuf[CONV_HIST_ROWS:CONV_HIST_ROWS + rows, :] = u
    conv = None
    for j in range(CONV_K):
        off = CONV_HIST_ROWS - (CONV_K - 1) + j
        term = cw_ref[j:j + 1, :] * ubuf[off:off + rows, :]
        conv = term if conv is None else conv + term
    return bg * conv


def _mixer_full_body(x_ref, g_ref, win_ref, biast_ref, sinkt_ref, cw_ref, wo_ref,
                     o_ref, klast_ref, vlast_ref, ulast_ref,
                     qbuf, kdup, vta, vtb, ubuf, mix, sbuf, pbuf, rbuf, kcar, vcar, ucar, *, tq):
    b = pl.program_id(0)
    t = pl.program_id(1)
    cl = CHUNK
    kw = WINDOW + cl
    w = WINDOW + tq
    low = lax.broadcasted_iota(jnp.int32, (1, KV_WIDTH), 1) < HEAD_DIM

    @pl.when((b == 0) & (t == 0))
    def _():
        kcar[...] = jnp.zeros_like(kcar)
        vcar[...] = jnp.zeros_like(vcar)
        ucar[...] = jnp.zeros_like(ucar)

    x = x_ref[...]
    h = _rms(x, g_ref[...]).astype(BF16)

    def proj(c0, width):
        return jnp.dot(h, win_ref[:, c0:c0 + width], preferred_element_type=F32)

    qbuf[...] = (proj(0, ATTN_WIDTH) * ATTN_SCALE).astype(BF16)
    kv = proj(ATTN_WIDTH, 2 * KV_WIDTH)
    k = kv[:, 0:KV_WIDTH]
    v = kv[:, KV_WIDTH:2 * KV_WIDTH]
    c0 = ATTN_WIDTH + 2 * KV_WIDTH
    bg = proj(c0, CONV_WIDTH)
    u = proj(c0 + CONV_WIDTH, CONV_WIDTH) * proj(c0 + 2 * CONV_WIDTH, CONV_WIDTH)

    kh = kcar[...]
    vh = vcar[...]
    uh = jnp.where(t == 0, 0.0, ucar[...])
    k_tail = k[tq - WINDOW:tq, :]
    v_tail = v[tq - WINDOW:tq, :]
    u_tail = u[tq - CONV_HIST_ROWS:tq, :]
    kcar[...] = k_tail
    vcar[...] = v_tail
    ucar[...] = u_tail
    klast_ref[...] = k_tail
    vlast_ref[...] = v_tail
    ulast_ref[...] = u_tail

    kf = jnp.concatenate([kh, k], axis=0)
    kr = pltpu.roll(kf, HEAD_DIM, axis=1)
    kdup[0] = jnp.where(low, kf, kr).astype(BF16)
    kdup[1] = jnp.where(low, kr, kf).astype(BF16)

    vf = jnp.concatenate([vh, v, jnp.zeros((cl, KV_WIDTH), F32)], axis=0)
    vr = pltpu.roll(vf, HEAD_DIM, axis=1)
    for hh, vd in enumerate((jnp.where(low, vf, vr), jnp.where(low, vr, vf))):
        for blk in range(w // KV_WIDTH):
            b0 = blk * KV_WIDTH
            vta[hh, :, b0:b0 + KV_WIDTH] = vd[b0:b0 + KV_WIDTH].T.astype(BF16)
            vtb[hh, :, b0:b0 + KV_WIDTH] = vd[cl + b0:cl + b0 + KV_WIDTH].T.astype(BF16)

    n_cols = SCORE_COLS
    key = lax.broadcasted_iota(jnp.int32, (kw, n_cols), 0)
    rr = lax.broadcasted_iota(jnp.int32, (PAIR_WIDTH, PAIR_WIDTH), 0) < HEAD_DIM
    cc = lax.broadcasted_iota(jnp.int32, (PAIR_WIDTH, PAIR_WIDTH), 1) < HEAD_DIM
    diag = rr == cc
    for c in range(tq // cl):
        r0 = c * cl
        for hh in range(N_KV_HEADS):
            rows = []
            for pr in range(PAIRS_PER_KV):
                c0 = (hh * PAIRS_PER_KV + pr) * PAIR_WIDTH
                qp = qbuf[r0:r0 + cl, c0:c0 + PAIR_WIDTH]
                rows += [jnp.where(low, qp, 0), jnp.where(low, 0, qp)]
            qm = jnp.concatenate(rows, axis=0)
            s = lax.dot_general(kdup[hh, r0:r0 + kw, :], qm, (((1,), (1,)), ((), ())),
                                preferred_element_type=F32) + biast_ref[hh]
            if r0 < WINDOW:
                s = jnp.where(key + (t * tq + r0 - WINDOW) >= 0, s, NEG)
            j0 = (c * N_KV_HEADS + hh) * n_cols
            sbuf[:, j0:j0 + n_cols] = s

    s = sbuf[...]
    sk = sinkt_ref[...]
    m = jnp.maximum(jnp.max(s, axis=0, keepdims=True), sk)
    p = jnp.exp(s - m)
    den = jnp.sum(p, axis=0, keepdims=True) + jnp.exp(sk - m)
    pbuf[...] = p.astype(BF16)
    rbuf[...] = 1.0 / den

    for c in range(tq // cl):
        r0 = c * cl
        k0 = r0 if c % 2 == 0 else r0 - cl
        vt_ref = vta if c % 2 == 0 else vtb
        for hh in range(N_KV_HEADS):
            j0 = (c * N_KV_HEADS + hh) * n_cols
            ot = jnp.dot(vt_ref[hh, :, k0:k0 + kw], pbuf[:, j0:j0 + n_cols],
                         preferred_element_type=F32) * rbuf[:, j0:j0 + n_cols]
            for pr in range(PAIRS_PER_KV):
                blk = jnp.where(diag, ot[:, pr * PAIR_WIDTH:(pr + 1) * PAIR_WIDTH], 0.0).T
                c0 = (hh * PAIRS_PER_KV + pr) * PAIR_WIDTH
                mix[r0:r0 + cl, c0:c0 + PAIR_WIDTH] = (blk[0:cl] + blk[cl:2 * cl]).astype(BF16)

    mix[:, ATTN_WIDTH:MIX_WIDTH] = _gated_conv(bg, u, uh, cw_ref, ubuf).astype(BF16)
    o_ref[...] = x + jnp.dot(mix[...], wo_ref[...], preferred_element_type=F32)


def _mixer_full(x, g, w_in, bias_l, sink_l, conv_w, w_out, e, *, nb, t_len, tq, cast=()):
    n = x.shape[0]
    nt = t_len // tq
    jobs = [_CastJob(a, lead, nb * nt) for a, lead in cast]
    job_specs = [job.specs(lambda b, t: b * nt + t) for job in jobs]
    cl = CHUNK
    kw = WINDOW + cl
    assert tq % KV_WIDTH == 0 and tq >= WINDOW
    all_cols = (tq // cl) * N_KV_HEADS * SCORE_COLS
    assert sink_l.shape == (1, all_cols)
    row = pl.BlockSpec((tq, D_MODEL), lambda b, t: (b * nt + t, 0))
    const = lambda a: pl.BlockSpec(a.shape, lambda b, t: (0,) * a.ndim)
    per_seq = lambda rows, width: pl.BlockSpec((rows, width), lambda b, t: (b, 0))
    est = 4 * tq * D_MODEL * 4 + (D_MODEL * IN_WIDTH + MIX_WIDTH * D_MODEL) * 2 \
        + kw * all_cols * 6 + 16 * tq * D_MODEL * 4
    whole = lambda a: pl.BlockSpec(a.shape, lambda b, t: (0, 0), pipeline_mode=pl.Buffered(1))
    return pl.pallas_call(
        _host_casts(functools.partial(_mixer_full_body, tq=tq), 7, 4, jobs),
        out_shape=(jax.ShapeDtypeStruct((n, D_MODEL), F32),
                   jax.ShapeDtypeStruct((nb * WINDOW, KV_WIDTH), F32),
                   jax.ShapeDtypeStruct((nb * WINDOW, KV_WIDTH), F32),
                   jax.ShapeDtypeStruct((nb * CONV_HIST_ROWS, CONV_WIDTH), F32))
        + tuple(job.out_shape for job in jobs),
        grid=(nb, nt),
        in_specs=[row, pl.BlockSpec((1, D_MODEL), lambda b, t: (0, 0)), whole(w_in),
                  const(bias_l), const(sink_l),
                  pl.BlockSpec((None, CONV_K, CONV_WIDTH), lambda b, t: (e, 0, 0)),
                  whole(w_out)] + [s[0] for s in job_specs],
        out_specs=(row, per_seq(WINDOW, KV_WIDTH), per_seq(WINDOW, KV_WIDTH),
                   per_seq(CONV_HIST_ROWS, CONV_WIDTH)) + tuple(s[1] for s in job_specs),
        scratch_shapes=[pltpu.VMEM((tq, ATTN_WIDTH), BF16),
                        pltpu.VMEM((N_KV_HEADS, WINDOW + tq, KV_WIDTH), BF16),
                        pltpu.VMEM((N_KV_HEADS, PAIR_WIDTH, WINDOW + tq), BF16),
                        pltpu.VMEM((N_KV_HEADS, PAIR_WIDTH, WINDOW + tq), BF16),
                        pltpu.VMEM((CONV_HIST_ROWS + tq, CONV_WIDTH), F32),
                        pltpu.VMEM((tq, MIX_WIDTH), BF16),
                        pltpu.VMEM((kw, all_cols), F32), pltpu.VMEM((kw, all_cols), BF16),
                        pltpu.VMEM((1, all_cols), F32),
                        pltpu.VMEM((WINDOW, KV_WIDTH), F32), pltpu.VMEM((WINDOW, KV_WIDTH), F32),
                        pltpu.VMEM((CONV_HIST_ROWS, CONV_WIDTH), F32)],
        compiler_params=pltpu.CompilerParams(
            dimension_semantics=("arbitrary", "arbitrary"), vmem_limit_bytes=_vmem_limit(est)),
        name="mixer_full",
    )(x, g.reshape(1, D_MODEL), w_in, bias_l, sink_l, conv_w, w_out, *[job.a for job in jobs])


def _mixer_ab(x, q, k, v, k_hist, v_hist, bg, u, u_hist, bias_l, sink_l, conv_w, w_out, e,
              *, nb, t_len):
    n = x.shape[0]
    cl = t_len
    assert t_len <= CHUNK and t_len % BF16_SUBLANES == 0
    kw = WINDOW + cl
    vmem = pl.BlockSpec(memory_space=pltpu.VMEM)
    score_rows = nb * N_KV_HEADS * 2 * PAIRS_PER_KV * cl
    return pl.pallas_call(
        functools.partial(_mixer_ab_body, nb=nb, cl=cl, e=e),
        out_shape=jax.ShapeDtypeStruct((n, D_MODEL), F32),
        in_specs=[vmem] * 13,
        out_specs=vmem,
        scratch_shapes=[pltpu.VMEM((nb, N_KV_HEADS, 2, kw, KV_WIDTH), BF16),
                        pltpu.VMEM((nb, N_KV_HEADS, 2, kw, KV_WIDTH), BF16),
                        pltpu.VMEM((nb, CONV_HIST_ROWS + cl, CONV_WIDTH), F32),
                        pltpu.VMEM((n, MIX_WIDTH), BF16),
                        pltpu.VMEM((score_rows, kw), F32), pltpu.VMEM((score_rows, kw), BF16),
                        pltpu.VMEM((score_rows, 1), F32)],
        name="mixer_ab",
    )(x, q, k, v, k_hist, v_hist, bg, u, u_hist, bias_l, sink_l, conv_w, w_out)


def _pool_rows(x, hist, g, pw_ref, sc_ref, o_ref, r0, pos0):
    rows = x.shape[0]
    h = _rms(x, g)
    sums = []
    cur = jnp.concatenate([hist, h], axis=0)
    for gi, w in enumerate(POOL_SIZES):
        assert w == 2 ** (gi + 1)
        cur = cur + pltpu.roll(cur, w // 2, axis=0)
        sums.append(cur[POOL_HIST_ROWS:POOL_HIST_ROWS + rows, 0:POOL_GROUP])
        if gi + 1 < len(POOL_SIZES):
            cur = cur[:, POOL_GROUP:]

    pos = pos0 + lax.broadcasted_iota(jnp.int32, (rows, 1), 0)
    for gi, w in enumerate(POOL_SIZES):
        sl = slice(gi * POOL_GROUP, (gi + 1) * POOL_GROUP)
        cnt = jnp.minimum(pos + 1, w).astype(F32)
        d = (sums[gi] / cnt - h[:, sl]).astype(BF16)
        y = jnp.dot(d, pw_ref[gi], preferred_element_type=F32)
        o_ref[r0:r0 + rows, sl] = x[:, sl] + y * sc_ref[:, sl]
    return h[rows - POOL_HIST_ROWS:rows, :]


def _mixer_c_body(x_ref, hist_ref, g_ref, pw_ref, sc_ref, o_ref, st_ref, hbuf,
                  *, nb, t_len, first_pos):
    n_hist = hist_ref.shape[1]
    pad = POOL_HIST_ROWS - n_hist
    for sq in range(nb):
        r0, h0 = sq * t_len, sq * POOL_HIST_ROWS
        hb = hbuf.at[sq]
        hb[0:pad, :] = jnp.zeros((pad, D_MODEL), F32)
        hb[pad:POOL_HIST_ROWS, :] = hist_ref[sq]
        st_ref[h0:h0 + POOL_HIST_ROWS, :] = _pool_rows(
            x_ref[r0:r0 + t_len, :], hb[...], g_ref[...], pw_ref, sc_ref, o_ref, r0, first_pos)


def _pool_specs(o):
    return [pl.BlockSpec((None, len(POOL_SIZES), POOL_GROUP, POOL_GROUP), lambda *a: (o, 0, 0, 0)),
            pl.BlockSpec((None, 1, D_MODEL), lambda *a: (o, 0, 0))]


def _mixer_c(x, hist, g, pool_w, scale, o, *, nb, t_len, first_pos):
    n = x.shape[0]
    whole = lambda a: pl.BlockSpec(a.shape, lambda i: (0,) * a.ndim)
    g2 = g.reshape(1, D_MODEL)
    st_shape = jax.ShapeDtypeStruct((nb * POOL_HIST_ROWS, D_MODEL), F32)
    return pl.pallas_call(
        functools.partial(_mixer_c_body, nb=nb, t_len=t_len, first_pos=first_pos),
        out_shape=(jax.ShapeDtypeStruct((n, D_MODEL), F32), st_shape),
        grid=(1,),
        in_specs=[whole(x), whole(hist), whole(g2)] + _pool_specs(o),
        out_specs=(whole(x), whole(st_shape)),
        scratch_shapes=[pltpu.VMEM((nb, POOL_HIST_ROWS, D_MODEL), F32)],
        compiler_params=pltpu.CompilerParams(dimension_semantics=("arbitrary",)),
        name="mixer_c",
    )(x, hist, g2, pool_w, scale.reshape(scale.shape[0], 1, D_MODEL))


def _ffn_pool_body(x_ref, g1_ref, wg_ref, wu_ref, wd_ref, g2_ref, pw_ref, sc_ref,
                   o_ref, st_ref, a_ref, carry_ref, *, tm, sub, tiles_per_seq):
    i = pl.program_id(0)
    t = i % tiles_per_seq

    @pl.when(i == 0)
    def _():
        carry_ref[...] = jnp.zeros_like(carry_ref)

    hist = jnp.where(t == 0, 0.0, carry_ref[...])
    g1 = g1_ref[...]
    g2 = g2_ref[...]
    blocks = [_swiglu_rows(x_ref[r0:r0 + sub, :], g1, wg_ref, wu_ref, wd_ref, a_ref, r0)
              for r0 in range(0, tm, sub)]
    for s, xb in enumerate(blocks):
        hist = _pool_rows(xb, hist, g2, pw_ref, sc_ref, o_ref, s * sub, t * tm + s * sub)
    carry_ref[...] = hist
    st_ref[...] = hist


def _ffn_pool(x, g1, w16, g2, pool_w, scale, o, *, nb, t_len, tm, sub):
    n = x.shape[0]
    tiles_per_seq = t_len // tm
    whole = lambda a: pl.BlockSpec(a.shape, lambda i: (0, 0), pipeline_mode=pl.Buffered(1))
    vec = pl.BlockSpec((1, D_MODEL), lambda i: (0, 0))
    row = pl.BlockSpec((tm, D_MODEL), lambda i: (i, 0))
    est = 3 * D_MODEL * D_FF * 2 + 4 * tm * D_MODEL * 4 + tm * D_FF * 2 + 6 * tm * 1024 * 4
    return pl.pallas_call(
        functools.partial(_ffn_pool_body, tm=tm, sub=sub, tiles_per_seq=tiles_per_seq),
        out_shape=(jax.ShapeDtypeStruct((n, D_MODEL), F32),
                   jax.ShapeDtypeStruct((nb * POOL_HIST_ROWS, D_MODEL), F32)),
        grid=(n // tm,),
        in_specs=[row, vec, whole(w16[0]), whole(w16[1]), whole(w16[2]), vec] + _pool_specs(o),
        out_specs=(row, pl.BlockSpec((POOL_HIST_ROWS, D_MODEL), lambda i: (i // tiles_per_seq, 0))),
        scratch_shapes=[pltpu.VMEM((tm, D_FF), BF16), pltpu.VMEM((POOL_HIST_ROWS, D_MODEL), F32)],
        compiler_params=pltpu.CompilerParams(
            dimension_semantics=("arbitrary",), vmem_limit_bytes=_vmem_limit(est)),
        name="ffn_pool",
    )(x, g1.reshape(1, D_MODEL), *w16, g2.reshape(1, D_MODEL), pool_w,
      scale.reshape(scale.shape[0], 1, D_MODEL))


def kernel(x_prompt, x_sample, cache_attn_k, cache_attn_v, cache_conv, cache_pool, norm_g, ffn_w_gate, ffn_w_up, ffn_w_down, mix_w_in, mix_w_out, conv_w, attn_sinks, rel_bias_table, pool_w, pool_scale, final_norm_g):
    past_len = 1024
    nb, t_len, _ = x_prompt.shape
    nbs, ts_len, _ = x_sample.shape
    xp = x_prompt.reshape(nb * t_len, D_MODEL)
    xs = x_sample.reshape(nbs * ts_len, D_MODEL)
    tf = min(FFN_ROW_TILE, xp.shape[0])
    tq = min(MIXER_ROW_TILE, t_len)
    assert t_len % tf == 0
    ts = xs.shape[0]
    pw = pool_w.astype(BF16)
    assert ts_len <= CHUNK
    ffn_w = (ffn_w_gate, ffn_w_up, ffn_w_down)

    w16 = {0: tuple(a[0, 0].astype(BF16) for a in ffn_w)}
    pending = list(range(1, 2 * DEPTH))
    mix16 = {}

    def ffn_weights(n):
        if n not in w16:
            pending.remove(n)
            w16[n] = tuple(a[n // 2, n % 2].astype(BF16) for a in ffn_w)
        return w16[n]

    def next_casts(n):
        if not pending or pending[0] <= n:
            return None, []
        m = pending.pop(0)
        return m, [(a, (m // 2, m % 2)) for a in ffn_w]

    def prompt_ffn(x, layer, j, final):
        n = 2 * layer + j
        m, cast = next_casts(n)
        if n == 0:
            cast = cast + [(a, (e,)) for e in range(mix_w_in.shape[0]) for a in (mix_w_in, mix_w_out)]
        y, *done = _ffn(x, norm_g[layer, 2 * j], ffn_weights(n), final_norm_g, final, tf, cast=cast)
        if m is not None:
            w16[m] = tuple(done[:3])
            done = done[3:]
        for e in range(len(done) // 2):
            mix16[e] = (done[2 * e], done[2 * e + 1])
        return y

    def sample_ffn(x, layer, j, final):
        return _ffn_stream(x, norm_g[layer, 2 * j], ffn_weights(2 * layer + j), final_norm_g, final)

    def last_rows(a, seqs, rows, keep, tail_shape):
        return a.reshape((seqs, rows) + tail_shape)[:, rows - keep:]

    outs = {name: [] for name in ("kp", "vp", "cp", "pp", "ks", "vs", "cs", "ps")}
    for layer in range(DEPTH):
        last = layer == DEPTH - 1
        if layer % 2 == 0:
            e = layer // 2
            xp = prompt_ffn(xp, layer, 0, False)
            xs = sample_ffn(xs, layer, 0, False)
            if e not in mix16:
                mix16[e] = (mix_w_in[e].astype(BF16), mix_w_out[e].astype(BF16))
            w_in, w_out = mix16[e]
            m, cast = next_casts(2 * layer)
            bias_t, sink_t, bias_r, sink_r = _attn_tables(rel_bias_table, attn_sinks[e], ts_len,
                                                          tq // CHUNK)
            xp, k, v, u, *done = _mixer_full(xp, norm_g[layer, 1], w_in, bias_t, sink_t, conv_w,
                                             w_out, e, nb=nb, t_len=t_len, tq=tq, cast=cast)
            if m is not None:
                w16[m] = tuple(done)
            keep = min(WINDOW, t_len)
            outs["kp"].append(last_rows(k, nb, WINDOW, keep, (N_KV_HEADS, HEAD_DIM)))
            outs["vp"].append(last_rows(v, nb, WINDOW, keep, (N_KV_HEADS, HEAD_DIM)))
            outs["cp"].append(last_rows(u, nb, CONV_HIST_ROWS, CONV_K - 1, (CONV_WIDTH,)))
            q, k, v, bg, u = _inproj(xs, norm_g[layer, 1], w_in, min(ROW_TILE, ts))
            k_hist = cache_attn_k[e].reshape(nbs * WINDOW, KV_WIDTH)
            v_hist = cache_attn_v[e].reshape(nbs * WINDOW, KV_WIDTH)
            xs = _mixer_ab(xs, q, k, v, k_hist, v_hist, bg, u, cache_conv[e], bias_r, sink_r,
                           conv_w, w_out, e, nb=nbs, t_len=ts_len)
            keep = min(WINDOW, ts_len)
            outs["ks"].append(last_rows(k, nbs, ts_len, keep, (N_KV_HEADS, HEAD_DIM)))
            outs["vs"].append(last_rows(v, nbs, ts_len, keep, (N_KV_HEADS, HEAD_DIM)))
            outs["cs"].append(last_rows(u, nbs, ts_len, CONV_K - 1, (CONV_WIDTH,)))
        else:
            o = layer // 2
            xp, st = _ffn_pool(xp, norm_g[layer, 0], ffn_weights(2 * layer), norm_g[layer, 1],
                               pw, pool_scale, o, nb=nb, t_len=t_len, tm=tf,
                               sub=tf // POOL_SUB_BLOCKS)
            outs["pp"].append(last_rows(st, nb, POOL_HIST_ROWS, POOL_MAX - 1, (D_MODEL,)))
            xs = sample_ffn(xs, layer, 0, False)
            xs, st = _mixer_c(xs, cache_pool[o], norm_g[layer, 1], pw, pool_scale, o,
                              nb=nbs, t_len=ts_len, first_pos=past_len)
            outs["ps"].append(last_rows(st, nbs, POOL_HIST_ROWS, POOL_MAX - 1, (D_MODEL,)))
        xp = prompt_ffn(xp, layer, 1, last)
        xs = sample_ffn(xs, layer, 1, last)
    stacked = {name: jnp.stack(v) for name, v in outs.items()}
    return (xp.reshape(nb, t_len, D_MODEL), xs.reshape(nbs, ts_len, D_MODEL),
            stacked["kp"], stacked["vp"], stacked["cp"], stacked["pp"],
            stacked["ks"], stacked["vs"], stacked["cs"], stacked["ps"])
```

```python
import functools
import math

import jax
import jax.numpy as jnp
from jax import lax
from jax.experimental import pallas as pl
from jax.experimental.pallas import tpu as pltpu

D_MODEL = 1024
DEPTH = 2
CHUNK = 64
N_HEADS = 8
N_KV_HEADS = 2
Q_PER_KV = N_HEADS // N_KV_HEADS
HEAD_DIM = 64
ATTN_WIDTH = N_HEADS * HEAD_DIM
KV_WIDTH = N_KV_HEADS * HEAD_DIM
WINDOW = 128
N_BUCKETS = 32
MAX_DISTANCE = 128
CONV_WIDTH = D_MODEL // 2
CONV_K = 3
IN_WIDTH = ATTN_WIDTH + 2 * KV_WIDTH + 3 * CONV_WIDTH
MIX_WIDTH = ATTN_WIDTH + CONV_WIDTH
POOL_SIZES = (2, 4, 8, 16)
POOL_GROUP = D_MODEL // len(POOL_SIZES)
POOL_MAX = max(POOL_SIZES)
D_FF = 2816
EPS = 1e-6
NEG = -1e30

V7X_VMEM_BYTES = 64 * 1024 * 1024
SUBLANES_F32 = 8
BF16_SUBLANES = 16
LANES = 128
CONV_HIST_ROWS = SUBLANES_F32
POOL_HIST_ROWS = 16
ROW_TILE = 512
MIXER_ROW_TILE = 1024
FFN_ROW_TILE = 1024
FF_CHUNK = 256
FF_CHUNKS = tuple((c0, min(FF_CHUNK, D_FF - c0)) for c0 in range(0, D_FF, FF_CHUNK))
POOL_SUB_BLOCKS = 4

F32 = jnp.float32
BF16 = jnp.bfloat16


def _vmem_limit(nbytes):
    return int(min(V7X_VMEM_BYTES - (4 << 20), max(32 << 20, nbytes)))


def _rms(x, g):
    return x * lax.rsqrt(jnp.mean(x * x, axis=-1, keepdims=True) + EPS) * g


PAIR_WIDTH = 2 * HEAD_DIM
PAIRS_PER_KV = Q_PER_KV // 2
SCORE_COLS = PAIRS_PER_KV * PAIR_WIDTH


def _attn_tables_body(tab_ref, sink_ref, bkt_t_ref, bkt_r_ref,
                      bias_t_ref, sink_t_ref, bias_r_ref, sink_r_ref, *, cl_s, n_chunks):
    col_head = lax.broadcasted_iota(jnp.int32, (1, SCORE_COLS), 1) // HEAD_DIM
    row_pair = lax.broadcasted_iota(jnp.int32, (PAIRS_PER_KV * cl_s, 1), 0) // cl_s

    def per_column(hh, value):
        out = jnp.full((1, SCORE_COLS), value(hh * Q_PER_KV), F32)
        for g in range(1, Q_PER_KV):
            out = jnp.where(col_head == g, value(hh * Q_PER_KV + g), out)
        return out

    def per_row(hh, lh, value):
        out = jnp.full((PAIRS_PER_KV * cl_s, 1), value(hh * Q_PER_KV + lh), F32)
        for pr in range(1, PAIRS_PER_KV):
            out = jnp.where(row_pair == pr, value(hh * Q_PER_KV + 2 * pr + lh), out)
        return out

    bkt_t = bkt_t_ref[...]
    bkt_r = bkt_r_ref[...]
    for hh in range(N_KV_HEADS):
        acc = jnp.zeros(bkt_t.shape, F32)
        for b in range(N_BUCKETS):
            acc = jnp.where(bkt_t == b, per_column(hh, lambda h: tab_ref[b, h]), acc)
        bias_t_ref[hh] = acc
        sink_row = per_column(hh, lambda h: sink_ref[h])
        for c in range(n_chunks):
            j0 = (c * N_KV_HEADS + hh) * SCORE_COLS
            sink_t_ref[:, j0:j0 + SCORE_COLS] = sink_row
        for lh in range(2):
            acc = jnp.zeros(bkt_r.shape, F32)
            for b in range(N_BUCKETS):
                acc = jnp.where(bkt_r == b, per_row(hh, lh, lambda h: tab_ref[b, h]), acc)
            bias_r_ref[hh, lh] = acc
            sink_r_ref[hh, lh] = per_row(hh, lh, lambda h: sink_ref[h])


def _rel_bucket(rel):
    half = N_BUCKETS // 2
    ret = jnp.where(rel > 0, half, 0)
    n = jnp.abs(rel)
    max_exact = half // 2
    nf = jnp.maximum(n, 1).astype(F32)
    large = max_exact + (jnp.log(nf / max_exact) / math.log(MAX_DISTANCE / max_exact)
                         * (half - max_exact)).astype(jnp.int32)
    large = jnp.minimum(large, half - 1)
    return ret + jnp.where(n < max_exact, n, large)


def _attn_tables(table, sinks, cl_s, n_chunks):
    rel = (jnp.arange(WINDOW + CHUNK, dtype=jnp.int32) - WINDOW)[None, :] \
        - jnp.arange(CHUNK, dtype=jnp.int32)[:, None]
    bucket = _rel_bucket(rel).astype(jnp.int32)
    bkt_t = jnp.tile(bucket.T, (1, Q_PER_KV))
    kw_s = WINDOW + cl_s
    bkt_r = jnp.tile(bucket[:cl_s, :kw_s], (PAIRS_PER_KV, 1))
    smem = pl.BlockSpec(memory_space=pltpu.SMEM)
    vmem = pl.BlockSpec(memory_space=pltpu.VMEM)
    return pl.pallas_call(
        functools.partial(_attn_tables_body, cl_s=cl_s, n_chunks=n_chunks),
        out_shape=(jax.ShapeDtypeStruct((N_KV_HEADS, WINDOW + CHUNK, SCORE_COLS), F32),
                   jax.ShapeDtypeStruct((1, n_chunks * N_KV_HEADS * SCORE_COLS), F32),
                   jax.ShapeDtypeStruct((N_KV_HEADS, 2, PAIRS_PER_KV * cl_s, kw_s), F32),
                   jax.ShapeDtypeStruct((N_KV_HEADS, 2, PAIRS_PER_KV * cl_s, 1), F32)),
        in_specs=[smem, smem, vmem, vmem],
        out_specs=(vmem, vmem, vmem, vmem),
        name="attn_tables",
    )(table, sinks, bkt_t, bkt_r)


def _swiglu_rows(x, g, wg_ref, wu_ref, wd_ref, a_ref, r0):
    rows = x.shape[0]
    xn = _rms(x, g).astype(BF16)
    for c0, cw in FF_CHUNKS:
        gate = jnp.dot(xn, wg_ref[:, c0:c0 + cw].astype(BF16), preferred_element_type=F32)
        up = jnp.dot(xn, wu_ref[:, c0:c0 + cw].astype(BF16), preferred_element_type=F32)
        a_ref[r0:r0 + rows, c0:c0 + cw] = (jax.nn.silu(gate) * up).astype(BF16)
    y = jnp.dot(a_ref[r0:r0 + rows, :], wd_ref[...].astype(BF16), preferred_element_type=F32)
    return x + 0.5 * y


class _CastJob:
    def __init__(self, a, lead, steps):
        self.a, self.lead = a, tuple(lead)
        rows, cols = a.shape[-2:]
        self.chunk = next(c for c in range(BF16_SUBLANES, rows + 1, BF16_SUBLANES)
                          if rows % c == 0 and steps % (rows // c) == 0)
        self.repeat = steps // (rows // self.chunk)
        self.out_shape = jax.ShapeDtypeStruct((rows, cols), BF16)

    def specs(self, step_of):
        none = (None,) * len(self.lead)
        cols = self.a.shape[-1]
        return (pl.BlockSpec(none + (self.chunk, cols),
                             lambda *g: self.lead + (step_of(*g) // self.repeat, 0)),
                pl.BlockSpec((self.chunk, cols), lambda *g: (step_of(*g) // self.repeat, 0)))


def _host_casts(body, n_in, n_out, jobs):
    k = len(jobs)

    def hosted(*refs):
        ins, cast_in = refs[:n_in], refs[n_in:n_in + k]
        outs, cast_out = refs[n_in + k:n_in + k + n_out], refs[n_in + k + n_out:n_in + 2 * k + n_out]
        for src, dst in zip(cast_in, cast_out):
            dst[...] = src[...].astype(BF16)
        body(*ins, *outs, *refs[n_in + 2 * k + n_out:])
    return hosted


def _ffn_body(x_ref, g_ref, wg_ref, wu_ref, wd_ref, fg_ref, o_ref, a_ref, *, final_norm):
    out = _swiglu_rows(x_ref[...], g_ref[...], wg_ref, wu_ref, wd_ref, a_ref, 0)
    if final_norm:
        out = _rms(out, fg_ref[...])
    o_ref[...] = out


def _ffn(x, g, w16, fg, final_norm, tm, cast=()):
    n = x.shape[0]
    steps = n // tm
    jobs = [_CastJob(a, lead, steps) for a, lead in cast]
    job_specs = [job.specs(lambda i: i) for job in jobs]
    whole = lambda a: pl.BlockSpec((None,) * (a.ndim - 2) + a.shape[-2:], lambda i: (0,) * a.ndim,
                                   pipeline_mode=pl.Buffered(1))
    vec = pl.BlockSpec((1, D_MODEL), lambda i: (0, 0))
    row = pl.BlockSpec((tm, D_MODEL), lambda i: (i, 0))
    est = sum(math.prod(a.shape[-2:]) * a.dtype.itemsize for a in w16) \
        + 4 * tm * D_MODEL * 4 + tm * D_FF * 2 + 5 * tm * 1024 * 4 + D_MODEL * D_FF * 2 \
        + sum(6 * job.chunk * job.a.shape[-1] * 2 for job in jobs)
    return pl.pallas_call(
        _host_casts(functools.partial(_ffn_body, final_norm=final_norm), 6, 1, jobs),
        out_shape=(jax.ShapeDtypeStruct((n, D_MODEL), F32),) + tuple(job.out_shape for job in jobs),
        grid=(steps,),
        in_specs=[row, vec, whole(w16[0]), whole(w16[1]), whole(w16[2]), vec] + [s[0] for s in job_specs],
        out_specs=(row,) + tuple(s[1] for s in job_specs),
        scratch_shapes=[pltpu.VMEM((tm, D_FF), BF16)],
        compiler_params=pltpu.CompilerParams(
            dimension_semantics=("arbitrary",), vmem_limit_bytes=_vmem_limit(est)),
        name="ffn",
    )(x, g.reshape(1, D_MODEL), *w16, fg.reshape(1, D_MODEL), *[job.a for job in jobs])


FF_STREAM_CHUNK = D_FF // 2
assert FF_STREAM_CHUNK % LANES == 0 and D_FF % FF_STREAM_CHUNK == 0


def _ffn_stream_body(x_ref, g_ref, wg_ref, wu_ref, wd_ref, fg_ref, o_ref, xn_ref, *, final_norm):
    k = pl.program_id(0)

    @pl.when(k == 0)
    def _():
        x = x_ref[...]
        xn_ref[...] = _rms(x, g_ref[...]).astype(BF16)
        o_ref[...] = x

    xn = xn_ref[...]
    gate = jnp.dot(xn, wg_ref[...].astype(BF16), preferred_element_type=F32)
    up = jnp.dot(xn, wu_ref[...].astype(BF16), preferred_element_type=F32)
    a = (jax.nn.silu(gate) * up).astype(BF16)
    o_ref[...] += 0.5 * jnp.dot(a, wd_ref[...].astype(BF16), preferred_element_type=F32)

    if final_norm:
        @pl.when(k == pl.num_programs(0) - 1)
        def _():
            o_ref[...] = _rms(o_ref[...], fg_ref[...])


def _ffn_stream(x, g, w16, fg, final_norm):
    n = x.shape[0]
    fc = FF_STREAM_CHUNK
    cols = lambda a: pl.BlockSpec((None,) * (a.ndim - 2) + (D_MODEL, fc),
                                  lambda k: (0,) * (a.ndim - 2) + (0, k))
    rows = lambda a: pl.BlockSpec((None,) * (a.ndim - 2) + (fc, D_MODEL),
                                  lambda k: (0,) * (a.ndim - 2) + (k, 0))
    whole = pl.BlockSpec((n, D_MODEL), lambda k: (0, 0))
    vec = pl.BlockSpec((1, D_MODEL), lambda k: (0, 0))
    return pl.pallas_call(
        functools.partial(_ffn_stream_body, final_norm=final_norm),
        out_shape=jax.ShapeDtypeStruct((n, D_MODEL), F32),
        grid=(D_FF // fc,),
        in_specs=[whole, vec, cols(w16[0]), cols(w16[1]), rows(w16[2]), vec],
        out_specs=whole,
        scratch_shapes=[pltpu.VMEM((n, D_MODEL), BF16)],
        compiler_params=pltpu.CompilerParams(dimension_semantics=("arbitrary",)),
        name="ffn_stream",
    )(x, g.reshape(1, D_MODEL), *w16, fg.reshape(1, D_MODEL))


ATTN_SCALE = HEAD_DIM ** -0.5


def _inproj_body(x_ref, g_ref, w_ref, q_ref, k_ref, v_ref, bg_ref, u_ref):
    h = _rms(x_ref[...], g_ref[...]).astype(BF16)
    p = jnp.dot(h, w_ref[...], preferred_element_type=F32)
    o = 0
    q_ref[...] = (p[:, o:o + ATTN_WIDTH] * ATTN_SCALE).astype(BF16)
    o += ATTN_WIDTH
    k_ref[...] = p[:, o:o + KV_WIDTH]
    o += KV_WIDTH
    v_ref[...] = p[:, o:o + KV_WIDTH]
    o += KV_WIDTH
    bg_ref[...] = p[:, o:o + CONV_WIDTH]
    o += CONV_WIDTH
    u_ref[...] = p[:, o:o + CONV_WIDTH] * p[:, o + CONV_WIDTH:o + 2 * CONV_WIDTH]


def _inproj(x, g, w_in, tm):
    n = x.shape[0]
    row = lambda w: pl.BlockSpec((tm, w), lambda i: (i, 0))
    est = 2 * D_MODEL * IN_WIDTH * 2 + 2 * tm * (D_MODEL + IN_WIDTH) * 4 + 2 * tm * IN_WIDTH * 4
    return pl.pallas_call(
        _inproj_body,
        out_shape=(jax.ShapeDtypeStruct((n, ATTN_WIDTH), BF16),
                   jax.ShapeDtypeStruct((n, KV_WIDTH), F32),
                   jax.ShapeDtypeStruct((n, KV_WIDTH), F32),
                   jax.ShapeDtypeStruct((n, CONV_WIDTH), F32),
                   jax.ShapeDtypeStruct((n, CONV_WIDTH), F32)),
        grid=(n // tm,),
        in_specs=[row(D_MODEL), pl.BlockSpec((1, D_MODEL), lambda i: (0, 0)),
                  pl.BlockSpec((D_MODEL, IN_WIDTH), lambda i: (0, 0))],
        out_specs=(row(ATTN_WIDTH), row(KV_WIDTH), row(KV_WIDTH), row(CONV_WIDTH), row(CONV_WIDTH)),
        compiler_params=pltpu.CompilerParams(
            dimension_semantics=("arbitrary",), vmem_limit_bytes=_vmem_limit(est)),
        name="inproj",
    )(x, g.reshape(1, D_MODEL), w_in)


def _mixer_ab_body(x_ref, q_ref, k_ref, v_ref, kh_ref, vh_ref, bg_ref, u_ref, uh_ref,
                   bias_ref, sink_ref, cw_ref, wo_ref, o_ref,
                   kbuf, vbuf, ubuf, mix, sbuf, pbuf, rbuf, *, nb, cl, e):
    cw_ref = cw_ref.at[e]
    low =lax.broadcasted_iota(jnp.int32, (1, KV_WIDTH), 1) < HEAD_DIM

    def stage(buf, a, r0):
        b = pltpu.roll(a, HEAD_DIM, axis=1)
        rows = a.shape[0]
        buf[0, 0, r0:r0 + rows, :] = jnp.where(low, a, 0.0).astype(BF16)
        buf[0, 1, r0:r0 + rows, :] = jnp.where(low, 0.0, b).astype(BF16)
        buf[1, 0, r0:r0 + rows, :] = jnp.where(low, b, 0.0).astype(BF16)
        buf[1, 1, r0:r0 + rows, :] = jnp.where(low, 0.0, a).astype(BF16)

    blk = PAIRS_PER_KV * cl
    blocks = [(sq, hh, lh) for sq in range(nb) for hh in range(N_KV_HEADS) for lh in range(2)]
    for sq in range(nb):
        r0, h0 = sq * cl, sq * WINDOW
        kb, vb = kbuf.at[sq], vbuf.at[sq]
        stage(kb, kh_ref[h0:h0 + WINDOW, :], 0)
        stage(kb, k_ref[r0:r0 + cl, :], WINDOW)
        stage(vb, vh_ref[h0:h0 + WINDOW, :], 0)
        stage(vb, v_ref[r0:r0 + cl, :], WINDOW)
        for hh in range(N_KV_HEADS):
            ql = jnp.concatenate(
                [q_ref[r0:r0 + cl, (hh * PAIRS_PER_KV + pr) * PAIR_WIDTH:
                       (hh * PAIRS_PER_KV + pr + 1) * PAIR_WIDTH] for pr in range(PAIRS_PER_KV)],
                axis=0)
            for lh in range(2):
                b0 = blocks.index((sq, hh, lh)) * blk
                sbuf[b0:b0 + blk, :] = lax.dot_general(
                    ql, kb[hh, lh], (((1,), (1,)), ((), ())),
                    preferred_element_type=F32) + bias_ref[hh, lh]

    s = sbuf[...]
    sk = jnp.concatenate([sink_ref[hh, lh] for _, hh, lh in blocks], axis=0)
    m = jnp.maximum(jnp.max(s, axis=-1, keepdims=True), sk)
    p = jnp.exp(s - m)
    den = jnp.sum(p, axis=-1, keepdims=True) + jnp.exp(sk - m)
    pbuf[...] = p.astype(BF16)
    rbuf[...] = 1.0 / den

    for sq in range(nb):
        r0 = sq * cl
        for hh in range(N_KV_HEADS):
            acc = None
            for lh in range(2):
                b0 = blocks.index((sq, hh, lh)) * blk
                o = jnp.dot(pbuf[b0:b0 + blk, :], vbuf[sq, hh, lh],
                            preferred_element_type=F32) * rbuf[b0:b0 + blk, :]
                acc = o if acc is None else acc + o
            for pr in range(PAIRS_PER_KV):
                c0 = (hh * PAIRS_PER_KV + pr) * PAIR_WIDTH
                mix[r0:r0 + cl, c0:c0 + PAIR_WIDTH] = acc[pr * cl:(pr + 1) * cl].astype(BF16)
        conv = _gated_conv(bg_ref[r0:r0 + cl, :], u_ref[r0:r0 + cl, :], uh_ref[sq], cw_ref,
                           ubuf.at[sq])
        mix[r0:r0 + cl, ATTN_WIDTH:MIX_WIDTH] = conv.astype(BF16)

    o_ref[...] = x_ref[...] + jnp.dot(mix[...], wo_ref[...], preferred_element_type=F32)


def _gated_conv(bg, u, uh, cw_ref, ubuf):
    rows = u.shape[0]
    ubuf[CONV_HIST_ROWS - uh.shape[0]:CONV_HIST_ROWS, :] = uh
    ubuf[CONV_HIST_ROWS:CONV_HIST_ROWS + rows, :] = u
    conv = None
    for j in range(CONV_K):
        off = CONV_HIST_ROWS - (CONV_K - 1) + j
        term = cw_ref[j:j + 1, :] * ubuf[off:off + rows, :]
        conv = term if conv is None else conv + term
    return bg * conv


def _mixer_full_body(x_ref, g_ref, win_ref, biast_ref, sinkt_ref, cw_ref, wo_ref,
                     o_ref, klast_ref, vlast_ref, ulast_ref,
                     qbuf, kdup, vta, vtb, ubuf, mix, sbuf, pbuf, rbuf, kcar, vcar, ucar, *, tq):
    b = pl.program_id(0)
    t = pl.program_id(1)
    cl = CHUNK
    kw = WINDOW + cl
    w = WINDOW + tq
    low = lax.broadcasted_iota(jnp.int32, (1, KV_WIDTH), 1) < HEAD_DIM

    @pl.when((b == 0) & (t == 0))
    def _():
        kcar[...] = jnp.zeros_like(kcar)
        vcar[...] = jnp.zeros_like(vcar)
        ucar[...] = jnp.zeros_like(ucar)

    x = x_ref[...]
    h = _rms(x, g_ref[...]).astype(BF16)

    def proj(c0, width):
        return jnp.dot(h, win_ref[:, c0:c0 + width], preferred_element_type=F32)

    qbuf[...] = (proj(0, ATTN_WIDTH) * ATTN_SCALE).astype(BF16)
    kv = proj(ATTN_WIDTH, 2 * KV_WIDTH)
    k = kv[:, 0:KV_WIDTH]
    v = kv[:, KV_WIDTH:2 * KV_WIDTH]
    c0 = ATTN_WIDTH + 2 * KV_WIDTH
    bg = proj(c0, CONV_WIDTH)
    u = proj(c0 + CONV_WIDTH, CONV_WIDTH) * proj(c0 + 2 * CONV_WIDTH, CONV_WIDTH)

    kh = kcar[...]
    vh = vcar[...]
    uh = jnp.where(t == 0, 0.0, ucar[...])
    k_tail = k[tq - WINDOW:tq, :]
    v_tail = v[tq - WINDOW:tq, :]
    u_tail = u[tq - CONV_HIST_ROWS:tq, :]
    kcar[...] = k_tail
    vcar[...] = v_tail
    ucar[...] = u_tail
    klast_ref[...] = k_tail
    vlast_ref[...] = v_tail
    ulast_ref[...] = u_tail

    kf = jnp.concatenate([kh, k], axis=0)
    kr = pltpu.roll(kf, HEAD_DIM, axis=1)
    kdup[0] = jnp.where(low, kf, kr).astype(BF16)
    kdup[1] = jnp.where(low, kr, kf).astype(BF16)

    vf = jnp.concatenate([vh, v, jnp.zeros((cl, KV_WIDTH), F32)], axis=0)
    vr = pltpu.roll(vf, HEAD_DIM, axis=1)
    for hh, vd in enumerate((jnp.where(low, vf, vr), jnp.where(low, vr, vf))):
        for blk in range(w // KV_WIDTH):
            b0 = blk * KV_WIDTH
            vta[hh, :, b0:b0 + KV_WIDTH] = vd[b0:b0 + KV_WIDTH].T.astype(BF16)
            vtb[hh, :, b0:b0 + KV_WIDTH] = vd[cl + b0:cl + b0 + KV_WIDTH].T.astype(BF16)

    n_cols = SCORE_COLS
    key = lax.broadcasted_iota(jnp.int32, (kw, n_cols), 0)
    rr = lax.broadcasted_iota(jnp.int32, (PAIR_WIDTH, PAIR_WIDTH), 0) < HEAD_DIM
    cc = lax.broadcasted_iota(jnp.int32, (PAIR_WIDTH, PAIR_WIDTH), 1) < HEAD_DIM
    diag = rr == cc
    for c in range(tq // cl):
        r0 = c * cl
        for hh in range(N_KV_HEADS):
            rows = []
            for pr in range(PAIRS_PER_KV):
                c0 = (hh * PAIRS_PER_KV + pr) * PAIR_WIDTH
                qp = qbuf[r0:r0 + cl, c0:c0 + PAIR_WIDTH]
                rows += [jnp.where(low, qp, 0), jnp.where(low, 0, qp)]
            qm = jnp.concatenate(rows, axis=0)
            s = lax.dot_general(kdup[hh, r0:r0 + kw, :], qm, (((1,), (1,)), ((), ())),
                                preferred_element_type=F32) + biast_ref[hh]
            if r0 < WINDOW:
                s = jnp.where(key + (t * tq + r0 - WINDOW) >= 0, s, NEG)
            j0 = (c * N_KV_HEADS + hh) * n_cols
            sbuf[:, j0:j0 + n_cols] = s

    s = sbuf[...]
    sk = sinkt_ref[...]
    m = jnp.maximum(jnp.max(s, axis=0, keepdims=True), sk)
    p = jnp.exp(s - m)
    den = jnp.sum(p, axis=0, keepdims=True) + jnp.exp(sk - m)
    pbuf[...] = p.astype(BF16)
    rbuf[...] = 1.0 / den

    for c in range(tq // cl):
        r0 = c * cl
        k0 = r0 if c % 2 == 0 else r0 - cl
        vt_ref = vta if c % 2 == 0 else vtb
        for hh in range(N_KV_HEADS):
            j0 = (c * N_KV_HEADS + hh) * n_cols
            ot = jnp.dot(vt_ref[hh, :, k0:k0 + kw], pbuf[:, j0:j0 + n_cols],
                         preferred_element_type=F32) * rbuf[:, j0:j0 + n_cols]
            for pr in range(PAIRS_PER_KV):
                blk = jnp.where(diag, ot[:, pr * PAIR_WIDTH:(pr + 1) * PAIR_WIDTH], 0.0).T
                c0 = (hh * PAIRS_PER_KV + pr) * PAIR_WIDTH
                mix[r0:r0 + cl, c0:c0 + PAIR_WIDTH] = (blk[0:cl] + blk[cl:2 * cl]).astype(BF16)

    mix[:, ATTN_WIDTH:MIX_WIDTH] = _gated_conv(bg, u, uh, cw_ref, ubuf).astype(BF16)
    o_ref[...] = x + jnp.dot(mix[...], wo_ref[...], preferred_element_type=F32)


def _mixer_full(x, g, w_in, bias_l, sink_l, conv_w, w_out, e, *, nb, t_len, tq, cast=()):
    n = x.shape[0]
    nt = t_len // tq
    jobs = [_CastJob(a, lead, nb * nt) for a, lead in cast]
    job_specs = [job.specs(lambda b, t: b * nt + t) for job in jobs]
    cl = CHUNK
    kw = WINDOW + cl
    assert tq % KV_WIDTH == 0 and tq >= WINDOW
    all_cols = (tq // cl) * N_KV_HEADS * SCORE_COLS
    assert sink_l.shape == (1, all_cols)
    row = pl.BlockSpec((tq, D_MODEL), lambda b, t: (b * nt + t, 0))
    const = lambda a: pl.BlockSpec(a.shape, lambda b, t: (0,) * a.ndim)
    per_seq = lambda rows, width: pl.BlockSpec((rows, width), lambda b, t: (b, 0))
    est = 4 * tq * D_MODEL * 4 + (D_MODEL * IN_WIDTH + MIX_WIDTH * D_MODEL) * 2 \
        + kw * all_cols * 6 + 16 * tq * D_MODEL * 4
    whole = lambda a: pl.BlockSpec(a.shape, lambda b, t: (0, 0), pipeline_mode=pl.Buffered(1))
    return pl.pallas_call(
        _host_casts(functools.partial(_mixer_full_body, tq=tq), 7, 4, jobs),
        out_shape=(jax.ShapeDtypeStruct((n, D_MODEL), F32),
                   jax.ShapeDtypeStruct((nb * WINDOW, KV_WIDTH), F32),
                   jax.ShapeDtypeStruct((nb * WINDOW, KV_WIDTH), F32),
                   jax.ShapeDtypeStruct((nb * CONV_HIST_ROWS, CONV_WIDTH), F32))
        + tuple(job.out_shape for job in jobs),
        grid=(nb, nt),
        in_specs=[row, pl.BlockSpec((1, D_MODEL), lambda b, t: (0, 0)), whole(w_in),
                  const(bias_l), const(sink_l),
                  pl.BlockSpec((None, CONV_K, CONV_WIDTH), lambda b, t: (e, 0, 0)),
                  whole(w_out)] + [s[0] for s in job_specs],
        out_specs=(row, per_seq(WINDOW, KV_WIDTH), per_seq(WINDOW, KV_WIDTH),
                   per_seq(CONV_HIST_ROWS, CONV_WIDTH)) + tuple(s[1] for s in job_specs),
        scratch_shapes=[pltpu.VMEM((tq, ATTN_WIDTH), BF16),
                        pltpu.VMEM((N_KV_HEADS, WINDOW + tq, KV_WIDTH), BF16),
                        pltpu.VMEM((N_KV_HEADS, PAIR_WIDTH, WINDOW + tq), BF16),
                        pltpu.VMEM((N_KV_HEADS, PAIR_WIDTH, WINDOW + tq), BF16),
                        pltpu.VMEM((CONV_HIST_ROWS + tq, CONV_WIDTH), F32),
                        pltpu.VMEM((tq, MIX_WIDTH), BF16),
                        pltpu.VMEM((kw, all_cols), F32), pltpu.VMEM((kw, all_cols), BF16),
                        pltpu.VMEM((1, all_cols), F32),
                        pltpu.VMEM((WINDOW, KV_WIDTH), F32), pltpu.VMEM((WINDOW, KV_WIDTH), F32),
                        pltpu.VMEM((CONV_HIST_ROWS, CONV_WIDTH), F32)],
        compiler_params=pltpu.CompilerParams(
            dimension_semantics=("arbitrary", "arbitrary"), vmem_limit_bytes=_vmem_limit(est)),
        name="mixer_full",
    )(x, g.reshape(1, D_MODEL), w_in, bias_l, sink_l, conv_w, w_out, *[job.a for job in jobs])


def _mixer_ab(x, q, k, v, k_hist, v_hist, bg, u, u_hist, bias_l, sink_l, conv_w, w_out, e,
              *, nb, t_len):
    n = x.shape[0]
    cl = t_len
    assert t_len <= CHUNK and t_len % BF16_SUBLANES == 0
    kw = WINDOW + cl
    vmem = pl.BlockSpec(memory_space=pltpu.VMEM)
    score_rows = nb * N_KV_HEADS * 2 * PAIRS_PER_KV * cl
    return pl.pallas_call(
        functools.partial(_mixer_ab_body, nb=nb, cl=cl, e=e),
        out_shape=jax.ShapeDtypeStruct((n, D_MODEL), F32),
        in_specs=[vmem] * 13,
        out_specs=vmem,
        scratch_shapes=[pltpu.VMEM((nb, N_KV_HEADS, 2, kw, KV_WIDTH), BF16),
                        pltpu.VMEM((nb, N_KV_HEADS, 2, kw, KV_WIDTH), BF16),
                        pltpu.VMEM((nb, CONV_HIST_ROWS + cl, CONV_WIDTH), F32),
                        pltpu.VMEM((n, MIX_WIDTH), BF16),
                        pltpu.VMEM((score_rows, kw), F32), pltpu.VMEM((score_rows, kw), BF16),
                        pltpu.VMEM((score_rows, 1), F32)],
        name="mixer_ab",
    )(x, q, k, v, k_hist, v_hist, bg, u, u_hist, bias_l, sink_l, conv_w, w_out)


def _pool_rows(x, hist, g, pw_ref, sc_ref, o_ref, r0, pos0):
    rows = x.shape[0]
    h = _rms(x, g)
    sums = []
    cur = jnp.concatenate([hist, h], axis=0)
    for gi, w in enumerate(POOL_SIZES):
        assert w == 2 ** (gi + 1)
        cur = cur + pltpu.roll(cur, w // 2, axis=0)
        sums.append(cur[POOL_HIST_ROWS:POOL_HIST_ROWS + rows, 0:POOL_GROUP])
        if gi + 1 < len(POOL_SIZES):
            cur = cur[:, POOL_GROUP:]

    pos = pos0 + lax.broadcasted_iota(jnp.int32, (rows, 1), 0)
    for gi, w in enumerate(POOL_SIZES):
        sl = slice(gi * POOL_GROUP, (gi + 1) * POOL_GROUP)
        cnt = jnp.minimum(pos + 1, w).astype(F32)
        d = (sums[gi] / cnt - h[:, sl]).astype(BF16)
        y = jnp.dot(d, pw_ref[gi], preferred_element_type=F32)
        o_ref[r0:r0 + rows, sl] = x[:, sl] + y * sc_ref[:, sl]
    return h[rows - POOL_HIST_ROWS:rows, :]


def _mixer_c_body(x_ref, hist_ref, g_ref, pw_ref, sc_ref, o_ref, st_ref, hbuf,
                  *, nb, t_len, first_pos):
    n_hist = hist_ref.shape[1]
    pad = POOL_HIST_ROWS - n_hist
    for sq in range(nb):
        r0, h0 = sq * t_len, sq * POOL_HIST_ROWS
        hb = hbuf.at[sq]
        hb[0:pad, :] = jnp.zeros((pad, D_MODEL), F32)
        hb[pad:POOL_HIST_ROWS, :] = hist_ref[sq]
        st_ref[h0:h0 + POOL_HIST_ROWS, :] = _pool_rows(
            x_ref[r0:r0 + t_len, :], hb[...], g_ref[...], pw_ref, sc_ref, o_ref, r0, first_pos)


def _pool_specs(o):
    return [pl.BlockSpec((None, len(POOL_SIZES), POOL_GROUP, POOL_GROUP), lambda *a: (o, 0, 0, 0)),
            pl.BlockSpec((None, 1, D_MODEL), lambda *a: (o, 0, 0))]


def _mixer_c(x, hist, g, pool_w, scale, o, *, nb, t_len, first_pos):
    n = x.shape[0]
    whole = lambda a: pl.BlockSpec(a.shape, lambda i: (0,) * a.ndim)
    g2 = g.reshape(1, D_MODEL)
    st_shape = jax.ShapeDtypeStruct((nb * POOL_HIST_ROWS, D_MODEL), F32)
    return pl.pallas_call(
        functools.partial(_mixer_c_body, nb=nb, t_len=t_len, first_pos=first_pos),
        out_shape=(jax.ShapeDtypeStruct((n, D_MODEL), F32), st_shape),
        grid=(1,),
        in_specs=[whole(x), whole(hist), whole(g2)] + _pool_specs(o),
        out_specs=(whole(x), whole(st_shape)),
        scratch_shapes=[pltpu.VMEM((nb, POOL_HIST_ROWS, D_MODEL), F32)],
        compiler_params=pltpu.CompilerParams(dimension_semantics=("arbitrary",)),
        name="mixer_c",
    )(x, hist, g2, pool_w, scale.reshape(scale.shape[0], 1, D_MODEL))


def _ffn_pool_body(x_ref, g1_ref, wg_ref, wu_ref, wd_ref, g2_ref, pw_ref, sc_ref,
                   o_ref, st_ref, a_ref, carry_ref, *, tm, sub, tiles_per_seq):
    i = pl.program_id(0)
    t = i % tiles_per_seq

    @pl.when(i == 0)
    def _():
        carry_ref[...] = jnp.zeros_like(carry_ref)

    hist = jnp.where(t == 0, 0.0, carry_ref[...])
    g1 = g1_ref[...]
    g2 = g2_ref[...]
    blocks = [_swiglu_rows(x_ref[r0:r0 + sub, :], g1, wg_ref, wu_ref, wd_ref, a_ref, r0)
              for r0 in range(0, tm, sub)]
    for s, xb in enumerate(blocks):
        hist = _pool_rows(xb, hist, g2, pw_ref, sc_ref, o_ref, s * sub, t * tm + s * sub)
    carry_ref[...] = hist
    st_ref[...] = hist


def _ffn_pool(x, g1, w16, g2, pool_w, scale, o, *, nb, t_len, tm, sub):
    n = x.shape[0]
    tiles_per_seq = t_len // tm
    whole = lambda a: pl.BlockSpec(a.shape, lambda i: (0, 0), pipeline_mode=pl.Buffered(1))
    vec = pl.BlockSpec((1, D_MODEL), lambda i: (0, 0))
    row = pl.BlockSpec((tm, D_MODEL), lambda i: (i, 0))
    est = 3 * D_MODEL * D_FF * 2 + 4 * tm * D_MODEL * 4 + tm * D_FF * 2 + 6 * tm * 1024 * 4
    return pl.pallas_call(
        functools.partial(_ffn_pool_body, tm=tm, sub=sub, tiles_per_seq=tiles_per_seq),
        out_shape=(jax.ShapeDtypeStruct((n, D_MODEL), F32),
                   jax.ShapeDtypeStruct((nb * POOL_HIST_ROWS, D_MODEL), F32)),
        grid=(n // tm,),
        in_specs=[row, vec, whole(w16[0]), whole(w16[1]), whole(w16[2]), vec] + _pool_specs(o),
        out_specs=(row, pl.BlockSpec((POOL_HIST_ROWS, D_MODEL), lambda i: (i // tiles_per_seq, 0))),
        scratch_shapes=[pltpu.VMEM((tm, D_FF), BF16), pltpu.VMEM((POOL_HIST_ROWS, D_MODEL), F32)],
        compiler_params=pltpu.CompilerParams(
            dimension_semantics=("arbitrary",), vmem_limit_bytes=_vmem_limit(est)),
        name="ffn_pool",
    )(x, g1.reshape(1, D_MODEL), *w16, g2.reshape(1, D_MODEL), pool_w,
      scale.reshape(scale.shape[0], 1, D_MODEL))


def kernel(x_prompt, x_sample, cache_attn_k, cache_attn_v, cache_conv, cache_pool, norm_g, ffn_w_gate, ffn_w_up, ffn_w_down, mix_w_in, mix_w_out, conv_w, attn_sinks, rel_bias_table, pool_w, pool_scale, final_norm_g):
    past_len = 1024
    nb, t_len, _ = x_prompt.shape
    nbs, ts_len, _ = x_sample.shape
    xp = x_prompt.reshape(nb * t_len, D_MODEL)
    xs = x_sample.reshape(nbs * ts_len, D_MODEL)
    tf = min(FFN_ROW_TILE, xp.shape[0])
    tq = min(MIXER_ROW_TILE, t_len)
    assert t_len % tf == 0
    ts = xs.shape[0]
    pw = pool_w.astype(BF16)
    assert ts_len <= CHUNK
    ffn_w = (ffn_w_gate, ffn_w_up, ffn_w_down)

    w16 = {0: (ffn_w_gate[0, 0].astype(BF16), ffn_w_up[0, 0].astype(BF16), ffn_w_down)}
    pending = list(range(1, 2 * DEPTH))
    mix16 = {}

    def ffn_weights(n):
        if n not in w16:
            pending.remove(n)
            w16[n] = tuple(a[n // 2, n % 2].astype(BF16) for a in ffn_w)
        return w16[n]

    def next_casts(n):
        if not pending or pending[0] <= n:
            return None, []
        m = pending.pop(0)
        return m, [(a, (m // 2, m % 2)) for a in ffn_w]

    def prompt_ffn(x, layer, j, final):
        n = 2 * layer + j
        m, cast = next_casts(n)
        if n == 0:
            cast = cast + [(a, (e,)) for e in range(mix_w_in.shape[0]) for a in (mix_w_in, mix_w_out)]
        y, *done = _ffn(x, norm_g[layer, 2 * j], ffn_weights(n), final_norm_g, final, tf, cast=cast)
        if m is not None:
            w16[m] = tuple(done[:3])
            done = done[3:]
        for e in range(len(done) // 2):
            mix16[e] = (done[2 * e], done[2 * e + 1])
        return y

    def sample_ffn(x, layer, j, final):
        return _ffn_stream(x, norm_g[layer, 2 * j], ffn_weights(2 * layer + j), final_norm_g, final)

    def last_rows(a, seqs, rows, keep, tail_shape):
        return a.reshape((seqs, rows) + tail_shape)[:, rows - keep:]

    outs = {name: [] for name in ("kp", "vp", "cp", "pp", "ks", "vs", "cs", "ps")}
    for layer in range(DEPTH):
        last = layer == DEPTH - 1
        if layer % 2 == 0:
            e = layer // 2
            xp = prompt_ffn(xp, layer, 0, False)
            xs = sample_ffn(xs, layer, 0, False)
            if e not in mix16:
                mix16[e] = (mix_w_in[e].astype(BF16), mix_w_out[e].astype(BF16))
            w_in, w_out = mix16[e]
            m, cast = next_casts(2 * layer)
            bias_t, sink_t, bias_r, sink_r = _attn_tables(rel_bias_table, attn_sinks[e], ts_len,
                                                          tq // CHUNK)
            xp, k, v, u, *done = _mixer_full(xp, norm_g[layer, 1], w_in, bias_t, sink_t, conv_w,
                                             w_out, e, nb=nb, t_len=t_len, tq=tq, cast=cast)
            if m is not None:
                w16[m] = tuple(done)
            keep = min(WINDOW, t_len)
            outs["kp"].append(last_rows(k, nb, WINDOW, keep, (N_KV_HEADS, HEAD_DIM)))
            outs["vp"].append(last_rows(v, nb, WINDOW, keep, (N_KV_HEADS, HEAD_DIM)))
            outs["cp"].append(last_rows(u, nb, CONV_HIST_ROWS, CONV_K - 1, (CONV_WIDTH,)))
            q, k, v, bg, u = _inproj(xs, norm_g[layer, 1], w_in, min(ROW_TILE, ts))
            k_hist = cache_attn_k[e].reshape(nbs * WINDOW, KV_WIDTH)
            v_hist = cache_attn_v[e].reshape(nbs * WINDOW, KV_WIDTH)
            xs = _mixer_ab(xs, q, k, v, k_hist, v_hist, bg, u, cache_conv[e], bias_r, sink_r,
                           conv_w, w_out, e, nb=nbs, t_len=ts_len)
            keep = min(WINDOW, ts_len)
            outs["ks"].append(last_rows(k, nbs, ts_len, keep, (N_KV_HEADS, HEAD_DIM)))
            outs["vs"].append(last_rows(v, nbs, ts_len, keep, (N_KV_HEADS, HEAD_DIM)))
            outs["cs"].append(last_rows(u, nbs, ts_len, CONV_K - 1, (CONV_WIDTH,)))
        else:
            o = layer // 2
            xp, st = _ffn_pool(xp, norm_g[layer, 0], ffn_weights(2 * layer), norm_g[layer, 1],
                               pw, pool_scale, o, nb=nb, t_len=t_len, tm=tf,
                               sub=tf // POOL_SUB_BLOCKS)
            outs["pp"].append(last_rows(st, nb, POOL_HIST_ROWS, POOL_MAX - 1, (D_MODEL,)))
            xs = sample_ffn(xs, layer, 0, False)
            xs, st = _mixer_c(xs, cache_pool[o], norm_g[layer, 1], pw, pool_scale, o,
                              nb=nbs, t_len=ts_len, first_pos=past_len)
            outs["ps"].append(last_rows(st, nbs, POOL_HIST_ROWS, POOL_MAX - 1, (D_MODEL,)))
        xp = prompt_ffn(xp, layer, 1, last)
        xs = sample_ffn(xs, layer, 1, last)
    stacked = {name: jnp.stack(v) for name, v in outs.items()}
    return (xp.reshape(nb, t_len, D_MODEL), xs.reshape(nbs, ts_len, D_MODEL),
            stacked["kp"], stacked["vp"], stacked["cp"], stacked["pp"],
            stacked["ks"], stacked["vs"], stacked["cs"], stacked["ps"])
```

```python
import functools
import math

import jax
import jax.numpy as jnp
from jax import lax
from jax.experimental import pallas as pl
from jax.experimental.pallas import tpu as pltpu

D_MODEL = 1024
DEPTH = 2
CHUNK = 64
N_HEADS = 8
N_KV_HEADS = 2
Q_PER_KV = N_HEADS // N_KV_HEADS
HEAD_DIM = 64
ATTN_WIDTH = N_HEADS * HEAD_DIM
KV_WIDTH = N_KV_HEADS * HEAD_DIM
WINDOW = 128
N_BUCKETS = 32
MAX_DISTANCE = 128
CONV_WIDTH = D_MODEL // 2
CONV_K = 3
IN_WIDTH = ATTN_WIDTH + 2 * KV_WIDTH + 3 * CONV_WIDTH
MIX_WIDTH = ATTN_WIDTH + CONV_WIDTH
POOL_SIZES = (2, 4, 8, 16)
POOL_GROUP = D_MODEL // len(POOL_SIZES)
POOL_MAX = max(POOL_SIZES)
D_FF = 2816
EPS = 1e-6
NEG = -1e30

V7X_VMEM_BYTES = 64 * 1024 * 1024
SUBLANES_F32 = 8
BF16_SUBLANES = 16
LANES = 128
CONV_HIST_ROWS = SUBLANES_F32
POOL_HIST_ROWS = 16
ROW_TILE = 512
MIXER_ROW_TILE = 1024
FFN_ROW_TILE = 1024
FF_CHUNK = 256
FF_CHUNKS = tuple((c0, min(FF_CHUNK, D_FF - c0)) for c0 in range(0, D_FF, FF_CHUNK))
POOL_SUB_BLOCKS = 4

F32 = jnp.float32
BF16 = jnp.bfloat16


def _vmem_limit(nbytes):
    return int(min(V7X_VMEM_BYTES - (4 << 20), max(32 << 20, nbytes)))


def _rms(x, g):
    return x * lax.rsqrt(jnp.mean(x * x, axis=-1, keepdims=True) + EPS) * g


PAIR_WIDTH = 2 * HEAD_DIM
PAIRS_PER_KV = Q_PER_KV // 2
SCORE_COLS = PAIRS_PER_KV * PAIR_WIDTH


def _attn_tables_body(tab_ref, sink_ref, bkt_t_ref, bkt_r_ref,
                      bias_t_ref, sink_t_ref, bias_r_ref, sink_r_ref, *, cl_s, n_chunks):
    col_head = lax.broadcasted_iota(jnp.int32, (1, SCORE_COLS), 1) // HEAD_DIM
    row_pair = lax.broadcasted_iota(jnp.int32, (PAIRS_PER_KV * cl_s, 1), 0) // cl_s

    def per_column(hh, value):
        out = jnp.full((1, SCORE_COLS), value(hh * Q_PER_KV), F32)
        for g in range(1, Q_PER_KV):
            out = jnp.where(col_head == g, value(hh * Q_PER_KV + g), out)
        return out

    def per_row(hh, lh, value):
        out = jnp.full((PAIRS_PER_KV * cl_s, 1), value(hh * Q_PER_KV + lh), F32)
        for pr in range(1, PAIRS_PER_KV):
            out = jnp.where(row_pair == pr, value(hh * Q_PER_KV + 2 * pr + lh), out)
        return out

    bkt_t = bkt_t_ref[...]
    bkt_r = bkt_r_ref[...]
    for hh in range(N_KV_HEADS):
        acc = jnp.zeros(bkt_t.shape, F32)
        for b in range(N_BUCKETS):
            acc = jnp.where(bkt_t == b, per_column(hh, lambda h: tab_ref[b, h]), acc)
        bias_t_ref[hh] = acc
        sink_row = per_column(hh, lambda h: sink_ref[h])
        for c in range(n_chunks):
            j0 = (c * N_KV_HEADS + hh) * SCORE_COLS
            sink_t_ref[:, j0:j0 + SCORE_COLS] = sink_row
        for lh in range(2):
            acc = jnp.zeros(bkt_r.shape, F32)
            for b in range(N_BUCKETS):
                acc = jnp.where(bkt_r == b, per_row(hh, lh, lambda h: tab_ref[b, h]), acc)
            bias_r_ref[hh, lh] = acc
            sink_r_ref[hh, lh] = per_row(hh, lh, lambda h: sink_ref[h])


def _rel_bucket(rel):
    half = N_BUCKETS // 2
    ret = jnp.where(rel > 0, half, 0)
    n = jnp.abs(rel)
    max_exact = half // 2
    nf = jnp.maximum(n, 1).astype(F32)
    large = max_exact + (jnp.log(nf / max_exact) / math.log(MAX_DISTANCE / max_exact)
                         * (half - max_exact)).astype(jnp.int32)
    large = jnp.minimum(large, half - 1)
    return ret + jnp.where(n < max_exact, n, large)


def _attn_tables(table, sinks, cl_s, n_chunks):
    rel = (jnp.arange(WINDOW + CHUNK, dtype=jnp.int32) - WINDOW)[None, :] \
        - jnp.arange(CHUNK, dtype=jnp.int32)[:, None]
    bucket = _rel_bucket(rel).astype(jnp.int32)
    bkt_t = jnp.tile(bucket.T, (1, Q_PER_KV))
    kw_s = WINDOW + cl_s
    bkt_r = jnp.tile(bucket[:cl_s, :kw_s], (PAIRS_PER_KV, 1))
    smem = pl.BlockSpec(memory_space=pltpu.SMEM)
    vmem = pl.BlockSpec(memory_space=pltpu.VMEM)
    return pl.pallas_call(
        functools.partial(_attn_tables_body, cl_s=cl_s, n_chunks=n_chunks),
        out_shape=(jax.ShapeDtypeStruct((N_KV_HEADS, WINDOW + CHUNK, SCORE_COLS), F32),
                   jax.ShapeDtypeStruct((1, n_chunks * N_KV_HEADS * SCORE_COLS), F32),
                   jax.ShapeDtypeStruct((N_KV_HEADS, 2, PAIRS_PER_KV * cl_s, kw_s), F32),
                   jax.ShapeDtypeStruct((N_KV_HEADS, 2, PAIRS_PER_KV * cl_s, 1), F32)),
        in_specs=[smem, smem, vmem, vmem],
        out_specs=(vmem, vmem, vmem, vmem),
        name="attn_tables",
    )(table, sinks, bkt_t, bkt_r)


def _swiglu_rows(x, g, wg_ref, wu_ref, wd_ref, a_ref, r0):
    rows = x.shape[0]
    xn = _rms(x, g).astype(BF16)
    for c0, cw in FF_CHUNKS:
        gate = jnp.dot(xn, wg_ref[:, c0:c0 + cw].astype(BF16), preferred_element_type=F32)
        up = jnp.dot(xn, wu_ref[:, c0:c0 + cw].astype(BF16), preferred_element_type=F32)
        a_ref[r0:r0 + rows, c0:c0 + cw] = (jax.nn.silu(gate) * up).astype(BF16)
    y = jnp.dot(a_ref[r0:r0 + rows, :], wd_ref[...].astype(BF16), preferred_element_type=F32)
    return x + 0.5 * y


class _CastJob:
    def __init__(self, a, lead, steps):
        self.a, self.lead = a, tuple(lead)
        rows, cols = a.shape[-2:]
        self.chunk = next(c for c in range(BF16_SUBLANES, rows + 1, BF16_SUBLANES)
                          if rows % c == 0 and steps % (rows // c) == 0)
        self.repeat = steps // (rows // self.chunk)
        self.out_shape = jax.ShapeDtypeStruct((rows, cols), BF16)

    def specs(self, step_of):
        none = (None,) * len(self.lead)
        cols = self.a.shape[-1]
        return (pl.BlockSpec(none + (self.chunk, cols),
                             lambda *g: self.lead + (step_of(*g) // self.repeat, 0)),
                pl.BlockSpec((self.chunk, cols), lambda *g: (step_of(*g) // self.repeat, 0)))


def _host_casts(body, n_in, n_out, jobs):
    k = len(jobs)

    def hosted(*refs):
        ins, cast_in = refs[:n_in], refs[n_in:n_in + k]
        outs, cast_out = refs[n_in + k:n_in + k + n_out], refs[n_in + k + n_out:n_in + 2 * k + n_out]
        for src, dst in zip(cast_in, cast_out):
            dst[...] = src[...].astype(BF16)
        body(*ins, *outs, *refs[n_in + 2 * k + n_out:])
    return hosted


def _ffn_body(x_ref, g_ref, wg_ref, wu_ref, wd_ref, fg_ref, o_ref, a_ref, *, final_norm):
    half = x_ref.shape[0] // 2
    for r0 in (0, half):
        out = _swiglu_rows(x_ref[r0:r0 + half, :], g_ref[...], wg_ref, wu_ref, wd_ref, a_ref, r0)
        if final_norm:
            out = _rms(out, fg_ref[...])
        o_ref[r0:r0 + half, :] = out


def _ffn(x, g, w16, fg, final_norm, tm, cast=()):
    n = x.shape[0]
    steps = n // tm
    jobs = [_CastJob(a, lead, steps) for a, lead in cast]
    job_specs = [job.specs(lambda i: i) for job in jobs]
    whole = lambda a: pl.BlockSpec((None,) * (a.ndim - 2) + a.shape[-2:], lambda i: (0,) * a.ndim,
                                   pipeline_mode=pl.Buffered(1))
    vec = pl.BlockSpec((1, D_MODEL), lambda i: (0, 0))
    row = pl.BlockSpec((tm, D_MODEL), lambda i: (i, 0))
    est = sum(math.prod(a.shape[-2:]) * a.dtype.itemsize for a in w16) \
        + 4 * tm * D_MODEL * 4 + tm * D_FF * 2 + 5 * tm * 1024 * 4 + D_MODEL * D_FF * 2 \
        + sum(6 * job.chunk * job.a.shape[-1] * 2 for job in jobs)
    return pl.pallas_call(
        _host_casts(functools.partial(_ffn_body, final_norm=final_norm), 6, 1, jobs),
        out_shape=(jax.ShapeDtypeStruct((n, D_MODEL), F32),) + tuple(job.out_shape for job in jobs),
        grid=(steps,),
        in_specs=[row, vec, whole(w16[0]), whole(w16[1]), whole(w16[2]), vec] + [s[0] for s in job_specs],
        out_specs=(row,) + tuple(s[1] for s in job_specs),
        scratch_shapes=[pltpu.VMEM((tm, D_FF), BF16)],
        compiler_params=pltpu.CompilerParams(
            dimension_semantics=("arbitrary",), vmem_limit_bytes=_vmem_limit(est)),
        name="ffn",
    )(x, g.reshape(1, D_MODEL), *w16, fg.reshape(1, D_MODEL), *[job.a for job in jobs])


FF_STREAM_CHUNK = D_FF // 2
assert FF_STREAM_CHUNK % LANES == 0 and D_FF % FF_STREAM_CHUNK == 0


def _ffn_stream_body(x_ref, g_ref, wg_ref, wu_ref, wd_ref, fg_ref, o_ref, xn_ref, *, final_norm):
    k = pl.program_id(0)

    @pl.when(k == 0)
    def _():
        x = x_ref[...]
        xn_ref[...] = _rms(x, g_ref[...]).astype(BF16)
        o_ref[...] = x

    xn = xn_ref[...]
    gate = jnp.dot(xn, wg_ref[...].astype(BF16), preferred_element_type=F32)
    up = jnp.dot(xn, wu_ref[...].astype(BF16), preferred_element_type=F32)
    a = (jax.nn.silu(gate) * up).astype(BF16)
    o_ref[...] += 0.5 * jnp.dot(a, wd_ref[...].astype(BF16), preferred_element_type=F32)

    if final_norm:
        @pl.when(k == pl.num_programs(0) - 1)
        def _():
            o_ref[...] = _rms(o_ref[...], fg_ref[...])


def _ffn_stream(x, g, w16, fg, final_norm):
    n = x.shape[0]
    fc = FF_STREAM_CHUNK
    cols = lambda a: pl.BlockSpec((None,) * (a.ndim - 2) + (D_MODEL, fc),
                                  lambda k: (0,) * (a.ndim - 2) + (0, k))
    rows = lambda a: pl.BlockSpec((None,) * (a.ndim - 2) + (fc, D_MODEL),
                                  lambda k: (0,) * (a.ndim - 2) + (k, 0))
    whole = pl.BlockSpec((n, D_MODEL), lambda k: (0, 0))
    vec = pl.BlockSpec((1, D_MODEL), lambda k: (0, 0))
    return pl.pallas_call(
        functools.partial(_ffn_stream_body, final_norm=final_norm),
        out_shape=jax.ShapeDtypeStruct((n, D_MODEL), F32),
        grid=(D_FF // fc,),
        in_specs=[whole, vec, cols(w16[0]), cols(w16[1]), rows(w16[2]), vec],
        out_specs=whole,
        scratch_shapes=[pltpu.VMEM((n, D_MODEL), BF16)],
        compiler_params=pltpu.CompilerParams(dimension_semantics=("arbitrary",)),
        name="ffn_stream",
    )(x, g.reshape(1, D_MODEL), *w16, fg.reshape(1, D_MODEL))


ATTN_SCALE = HEAD_DIM ** -0.5


def _inproj_body(x_ref, g_ref, w_ref, q_ref, k_ref, v_ref, bg_ref, u_ref):
    h = _rms(x_ref[...], g_ref[...]).astype(BF16)
    p = jnp.dot(h, w_ref[...], preferred_element_type=F32)
    o = 0
    q_ref[...] = (p[:, o:o + ATTN_WIDTH] * ATTN_SCALE).astype(BF16)
    o += ATTN_WIDTH
    k_ref[...] = p[:, o:o + KV_WIDTH]
    o += KV_WIDTH
    v_ref[...] = p[:, o:o + KV_WIDTH]
    o += KV_WIDTH
    bg_ref[...] = p[:, o:o + CONV_WIDTH]
    o += CONV_WIDTH
    u_ref[...] = p[:, o:o + CONV_WIDTH] * p[:, o + CONV_WIDTH:o + 2 * CONV_WIDTH]


def _inproj(x, g, w_in, tm):
    n = x.shape[0]
    row = lambda w: pl.BlockSpec((tm, w), lambda i: (i, 0))
    est = 2 * D_MODEL * IN_WIDTH * 2 + 2 * tm * (D_MODEL + IN_WIDTH) * 4 + 2 * tm * IN_WIDTH * 4
    return pl.pallas_call(
        _inproj_body,
        out_shape=(jax.ShapeDtypeStruct((n, ATTN_WIDTH), BF16),
                   jax.ShapeDtypeStruct((n, KV_WIDTH), F32),
                   jax.ShapeDtypeStruct((n, KV_WIDTH), F32),
                   jax.ShapeDtypeStruct((n, CONV_WIDTH), F32),
                   jax.ShapeDtypeStruct((n, CONV_WIDTH), F32)),
        grid=(n // tm,),
        in_specs=[row(D_MODEL), pl.BlockSpec((1, D_MODEL), lambda i: (0, 0)),
                  pl.BlockSpec((D_MODEL, IN_WIDTH), lambda i: (0, 0))],
        out_specs=(row(ATTN_WIDTH), row(KV_WIDTH), row(KV_WIDTH), row(CONV_WIDTH), row(CONV_WIDTH)),
        compiler_params=pltpu.CompilerParams(
            dimension_semantics=("arbitrary",), vmem_limit_bytes=_vmem_limit(est)),
        name="inproj",
    )(x, g.reshape(1, D_MODEL), w_in)


def _mixer_ab_body(x_ref, q_ref, k_ref, v_ref, kh_ref, vh_ref, bg_ref, u_ref, uh_ref,
                   bias_ref, sink_ref, cw_ref, wo_ref, o_ref,
                   kbuf, vbuf, ubuf, mix, sbuf, pbuf, rbuf, *, nb, cl, e):
    cw_ref = cw_ref.at[e]
    low =lax.broadcasted_iota(jnp.int32, (1, KV_WIDTH), 1) < HEAD_DIM

    def stage(buf, a, r0):
        b = pltpu.roll(a, HEAD_DIM, axis=1)
        rows = a.shape[0]
        buf[0, 0, r0:r0 + rows, :] = jnp.where(low, a, 0.0).astype(BF16)
        buf[0, 1, r0:r0 + rows, :] = jnp.where(low, 0.0, b).astype(BF16)
        buf[1, 0, r0:r0 + rows, :] = jnp.where(low, b, 0.0).astype(BF16)
        buf[1, 1, r0:r0 + rows, :] = jnp.where(low, 0.0, a).astype(BF16)

    blk = PAIRS_PER_KV * cl
    blocks = [(sq, hh, lh) for sq in range(nb) for hh in range(N_KV_HEADS) for lh in range(2)]
    for sq in range(nb):
        r0, h0 = sq * cl, sq * WINDOW
        kb, vb = kbuf.at[sq], vbuf.at[sq]
        stage(kb, kh_ref[h0:h0 + WINDOW, :], 0)
        stage(kb, k_ref[r0:r0 + cl, :], WINDOW)
        stage(vb, vh_ref[h0:h0 + WINDOW, :], 0)
        stage(vb, v_ref[r0:r0 + cl, :], WINDOW)
        for hh in range(N_KV_HEADS):
            ql = jnp.concatenate(
                [q_ref[r0:r0 + cl, (hh * PAIRS_PER_KV + pr) * PAIR_WIDTH:
                       (hh * PAIRS_PER_KV + pr + 1) * PAIR_WIDTH] for pr in range(PAIRS_PER_KV)],
                axis=0)
            for lh in range(2):
                b0 = blocks.index((sq, hh, lh)) * blk
                sbuf[b0:b0 + blk, :] = lax.dot_general(
                    ql, kb[hh, lh], (((1,), (1,)), ((), ())),
                    preferred_element_type=F32) + bias_ref[hh, lh]

    s = sbuf[...]
    sk = jnp.concatenate([sink_ref[hh, lh] for _, hh, lh in blocks], axis=0)
    m = jnp.maximum(jnp.max(s, axis=-1, keepdims=True), sk)
    p = jnp.exp(s - m)
    den = jnp.sum(p, axis=-1, keepdims=True) + jnp.exp(sk - m)
    pbuf[...] = p.astype(BF16)
    rbuf[...] = 1.0 / den

    for sq in range(nb):
        r0 = sq * cl
        for hh in range(N_KV_HEADS):
            acc = None
            for lh in range(2):
                b0 = blocks.index((sq, hh, lh)) * blk
                o = jnp.dot(pbuf[b0:b0 + blk, :], vbuf[sq, hh, lh],
                            preferred_element_type=F32) * rbuf[b0:b0 + blk, :]
                acc = o if acc is None else acc + o
            for pr in range(PAIRS_PER_KV):
                c0 = (hh * PAIRS_PER_KV + pr) * PAIR_WIDTH
                mix[r0:r0 + cl, c0:c0 + PAIR_WIDTH] = acc[pr * cl:(pr + 1) * cl].astype(BF16)
        conv = _gated_conv(bg_ref[r0:r0 + cl, :], u_ref[r0:r0 + cl, :], uh_ref[sq], cw_ref,
                           ubuf.at[sq])
        mix[r0:r0 + cl, ATTN_WIDTH:MIX_WIDTH] = conv.astype(BF16)

    o_ref[...] = x_ref[...] + jnp.dot(mix[...], wo_ref[...], preferred_element_type=F32)


def _gated_conv(bg, u, uh, cw_ref, ubuf):
    rows = u.shape[0]
    ubuf[CONV_HIST_ROWS - uh.shape[0]:CONV_HIST_ROWS, :] = uh
    ubuf[CONV_HIST_ROWS:CONV_HIST_ROWS + rows, :] = u
    conv = None
    for j in range(CONV_K):
        off = CONV_HIST_ROWS - (CONV_K - 1) + j
        term = cw_ref[j:j + 1, :] * ubuf[off:off + rows, :]
        conv = term if conv is None else conv + term
    return bg * conv


def _mixer_full_body(x_ref, g_ref, win_ref, biast_ref, sinkt_ref, cw_ref, wo_ref,
                     o_ref, klast_ref, vlast_ref, ulast_ref,
                     qbuf, kdup, vta, vtb, ubuf, mix, sbuf, pbuf, rbuf, kcar, vcar, ucar, *, tq):
    b = pl.program_id(0)
    t = pl.program_id(1)
    cl = CHUNK
    kw = WINDOW + cl
    w = WINDOW + tq
    low = lax.broadcasted_iota(jnp.int32, (1, KV_WIDTH), 1) < HEAD_DIM

    @pl.when((b == 0) & (t == 0))
    def _():
        kcar[...] = jnp.zeros_like(kcar)
        vcar[...] = jnp.zeros_like(vcar)
        ucar[...] = jnp.zeros_like(ucar)

    x = x_ref[...]
    h = _rms(x, g_ref[...]).astype(BF16)

    def proj(c0, width):
        return jnp.dot(h, win_ref[:, c0:c0 + width], preferred_element_type=F32)

    qbuf[...] = (proj(0, ATTN_WIDTH) * ATTN_SCALE).astype(BF16)
    kv = proj(ATTN_WIDTH, 2 * KV_WIDTH)
    k = kv[:, 0:KV_WIDTH]
    v = kv[:, KV_WIDTH:2 * KV_WIDTH]
    c0 = ATTN_WIDTH + 2 * KV_WIDTH
    bg = proj(c0, CONV_WIDTH)
    u = proj(c0 + CONV_WIDTH, CONV_WIDTH) * proj(c0 + 2 * CONV_WIDTH, CONV_WIDTH)

    kh = kcar[...]
    vh = vcar[...]
    uh = jnp.where(t == 0, 0.0, ucar[...])
    k_tail = k[tq - WINDOW:tq, :]
    v_tail = v[tq - WINDOW:tq, :]
    u_tail = u[tq - CONV_HIST_ROWS:tq, :]
    kcar[...] = k_tail
    vcar[...] = v_tail
    ucar[...] = u_tail
    klast_ref[...] = k_tail
    vlast_ref[...] = v_tail
    ulast_ref[...] = u_tail

    kf = jnp.concatenate([kh, k], axis=0)
    kr = pltpu.roll(kf, HEAD_DIM, axis=1)
    kdup[0] = jnp.where(low, kf, kr).astype(BF16)
    kdup[1] = jnp.where(low, kr, kf).astype(BF16)

    vf = jnp.concatenate([vh, v, jnp.zeros((cl, KV_WIDTH), F32)], axis=0)
    vr = pltpu.roll(vf, HEAD_DIM, axis=1)
    for hh, vd in enumerate((jnp.where(low, vf, vr), jnp.where(low, vr, vf))):
        for blk in range(w // KV_WIDTH):
            b0 = blk * KV_WIDTH
            vta[hh, :, b0:b0 + KV_WIDTH] = vd[b0:b0 + KV_WIDTH].T.astype(BF16)
            vtb[hh, :, b0:b0 + KV_WIDTH] = vd[cl + b0:cl + b0 + KV_WIDTH].T.astype(BF16)

    n_cols = SCORE_COLS
    key = lax.broadcasted_iota(jnp.int32, (kw, n_cols), 0)
    rr = lax.broadcasted_iota(jnp.int32, (PAIR_WIDTH, PAIR_WIDTH), 0) < HEAD_DIM
    cc = lax.broadcasted_iota(jnp.int32, (PAIR_WIDTH, PAIR_WIDTH), 1) < HEAD_DIM
    diag = rr == cc
    for c in range(tq // cl):
        r0 = c * cl
        for hh in range(N_KV_HEADS):
            rows = []
            for pr in range(PAIRS_PER_KV):
                c0 = (hh * PAIRS_PER_KV + pr) * PAIR_WIDTH
                qp = qbuf[r0:r0 + cl, c0:c0 + PAIR_WIDTH]
                rows += [jnp.where(low, qp, 0), jnp.where(low, 0, qp)]
            qm = jnp.concatenate(rows, axis=0)
            s = lax.dot_general(kdup[hh, r0:r0 + kw, :], qm, (((1,), (1,)), ((), ())),
                                preferred_element_type=F32) + biast_ref[hh]
            if r0 < WINDOW:
                s = jnp.where(key + (t * tq + r0 - WINDOW) >= 0, s, NEG)
            j0 = (c * N_KV_HEADS + hh) * n_cols
            sbuf[:, j0:j0 + n_cols] = s

    s = sbuf[...]
    sk = sinkt_ref[...]
    m = jnp.maximum(jnp.max(s, axis=0, keepdims=True), sk)
    p = jnp.exp(s - m)
    den = jnp.sum(p, axis=0, keepdims=True) + jnp.exp(sk - m)
    pbuf[...] = p.astype(BF16)
    rbuf[...] = 1.0 / den

    for c in range(tq // cl):
        r0 = c * cl
        k0 = r0 if c % 2 == 0 else r0 - cl
        vt_ref = vta if c % 2 == 0 else vtb
        for hh in range(N_KV_HEADS):
            j0 = (c * N_KV_HEADS + hh) * n_cols
            ot = jnp.dot(vt_ref[hh, :, k0:k0 + kw], pbuf[:, j0:j0 + n_cols],
                         preferred_element_type=F32) * rbuf[:, j0:j0 + n_cols]
            for pr in range(PAIRS_PER_KV):
                blk = jnp.where(diag, ot[:, pr * PAIR_WIDTH:(pr + 1) * PAIR_WIDTH], 0.0).T
                c0 = (hh * PAIRS_PER_KV + pr) * PAIR_WIDTH
                mix[r0:r0 + cl, c0:c0 + PAIR_WIDTH] = (blk[0:cl] + blk[cl:2 * cl]).astype(BF16)

    mix[:, ATTN_WIDTH:MIX_WIDTH] = _gated_conv(bg, u, uh, cw_ref, ubuf).astype(BF16)
    o_ref[...] = x + jnp.dot(mix[...], wo_ref[...], preferred_element_type=F32)


def _mixer_full(x, g, w_in, bias_l, sink_l, conv_w, w_out, e, *, nb, t_len, tq, cast=()):
    n = x.shape[0]
    nt = t_len // tq
    jobs = [_CastJob(a, lead, nb * nt) for a, lead in cast]
    job_specs = [job.specs(lambda b, t: b * nt + t) for job in jobs]
    cl = CHUNK
    kw = WINDOW + cl
    assert tq % KV_WIDTH == 0 and tq >= WINDOW
    all_cols = (tq // cl) * N_KV_HEADS * SCORE_COLS
    assert sink_l.shape == (1, all_cols)
    row = pl.BlockSpec((tq, D_MODEL), lambda b, t: (b * nt + t, 0))
    const = lambda a: pl.BlockSpec(a.shape, lambda b, t: (0,) * a.ndim)
    per_seq = lambda rows, width: pl.BlockSpec((rows, width), lambda b, t: (b, 0))
    est = 4 * tq * D_MODEL * 4 + (D_MODEL * IN_WIDTH + MIX_WIDTH * D_MODEL) * 2 \
        + kw * all_cols * 6 + 16 * tq * D_MODEL * 4
    whole = lambda a: pl.BlockSpec(a.shape, lambda b, t: (0, 0), pipeline_mode=pl.Buffered(1))
    return pl.pallas_call(
        _host_casts(functools.partial(_mixer_full_body, tq=tq), 7, 4, jobs),
        out_shape=(jax.ShapeDtypeStruct((n, D_MODEL), F32),
                   jax.ShapeDtypeStruct((nb * WINDOW, KV_WIDTH), F32),
                   jax.ShapeDtypeStruct((nb * WINDOW, KV_WIDTH), F32),
                   jax.ShapeDtypeStruct((nb * CONV_HIST_ROWS, CONV_WIDTH), F32))
        + tuple(job.out_shape for job in jobs),
        grid=(nb, nt),
        in_specs=[row, pl.BlockSpec((1, D_MODEL), lambda b, t: (0, 0)), whole(w_in),
                  const(bias_l), const(sink_l),
                  pl.BlockSpec((None, CONV_K, CONV_WIDTH), lambda b, t: (e, 0, 0)),
                  whole(w_out)] + [s[0] for s in job_specs],
        out_specs=(row, per_seq(WINDOW, KV_WIDTH), per_seq(WINDOW, KV_WIDTH),
                   per_seq(CONV_HIST_ROWS, CONV_WIDTH)) + tuple(s[1] for s in job_specs),
        scratch_shapes=[pltpu.VMEM((tq, ATTN_WIDTH), BF16),
                        pltpu.VMEM((N_KV_HEADS, WINDOW + tq, KV_WIDTH), BF16),
                        pltpu.VMEM((N_KV_HEADS, PAIR_WIDTH, WINDOW + tq), BF16),
                        pltpu.VMEM((N_KV_HEADS, PAIR_WIDTH, WINDOW + tq), BF16),
                        pltpu.VMEM((CONV_HIST_ROWS + tq, CONV_WIDTH), F32),
                        pltpu.VMEM((tq, MIX_WIDTH), BF16),
                        pltpu.VMEM((kw, all_cols), F32), pltpu.VMEM((kw, all_cols), BF16),
                        pltpu.VMEM((1, all_cols), F32),
                        pltpu.VMEM((WINDOW, KV_WIDTH), F32), pltpu.VMEM((WINDOW, KV_WIDTH), F32),
                        pltpu.VMEM((CONV_HIST_ROWS, CONV_WIDTH), F32)],
        compiler_params=pltpu.CompilerParams(
            dimension_semantics=("arbitrary", "arbitrary"), vmem_limit_bytes=_vmem_limit(est)),
        name="mixer_full",
    )(x, g.reshape(1, D_MODEL), w_in, bias_l, sink_l, conv_w, w_out, *[job.a for job in jobs])


def _mixer_ab(x, q, k, v, k_hist, v_hist, bg, u, u_hist, bias_l, sink_l, conv_w, w_out, e,
              *, nb, t_len):
    n = x.shape[0]
    cl = t_len
    assert t_len <= CHUNK and t_len % BF16_SUBLANES == 0
    kw = WINDOW + cl
    vmem = pl.BlockSpec(memory_space=pltpu.VMEM)
    score_rows = nb * N_KV_HEADS * 2 * PAIRS_PER_KV * cl
    return pl.pallas_call(
        functools.partial(_mixer_ab_body, nb=nb, cl=cl, e=e),
        out_shape=jax.ShapeDtypeStruct((n, D_MODEL), F32),
        in_specs=[vmem] * 13,
        out_specs=vmem,
        scratch_shapes=[pltpu.VMEM((nb, N_KV_HEADS, 2, kw, KV_WIDTH), BF16),
                        pltpu.VMEM((nb, N_KV_HEADS, 2, kw, KV_WIDTH), BF16),
                        pltpu.VMEM((nb, CONV_HIST_ROWS + cl, CONV_WIDTH), F32),
                        pltpu.VMEM((n, MIX_WIDTH), BF16),
                        pltpu.VMEM((score_rows, kw), F32), pltpu.VMEM((score_rows, kw), BF16),
                        pltpu.VMEM((score_rows, 1), F32)],
        name="mixer_ab",
    )(x, q, k, v, k_hist, v_hist, bg, u, u_hist, bias_l, sink_l, conv_w, w_out)


def _pool_rows(x, hist, g, pw_ref, sc_ref, o_ref, r0, pos0):
    rows = x.shape[0]
    h = _rms(x, g)
    sums = []
    cur = jnp.concatenate([hist, h], axis=0)
    for gi, w in enumerate(POOL_SIZES):
        assert w == 2 ** (gi + 1)
        cur = cur + pltpu.roll(cur, w // 2, axis=0)
        sums.append(cur[POOL_HIST_ROWS:POOL_HIST_ROWS + rows, 0:POOL_GROUP])
        if gi + 1 < len(POOL_SIZES):
            cur = cur[:, POOL_GROUP:]

    pos = pos0 + lax.broadcasted_iota(jnp.int32, (rows, 1), 0)
    for gi, w in enumerate(POOL_SIZES):
        sl = slice(gi * POOL_GROUP, (gi + 1) * POOL_GROUP)
        cnt = jnp.minimum(pos + 1, w).astype(F32)
        d = (sums[gi] / cnt - h[:, sl]).astype(BF16)
        y = jnp.dot(d, pw_ref[gi], preferred_element_type=F32)
        o_ref[r0:r0 + rows, sl] = x[:, sl] + y * sc_ref[:, sl]
    return h[rows - POOL_HIST_ROWS:rows, :]


def _mixer_c_body(x_ref, hist_ref, g_ref, pw_ref, sc_ref, o_ref, st_ref, hbuf,
                  *, nb, t_len, first_pos):
    n_hist = hist_ref.shape[1]
    pad = POOL_HIST_ROWS - n_hist
    for sq in range(nb):
        r0, h0 = sq * t_len, sq * POOL_HIST_ROWS
        hb = hbuf.at[sq]
        hb[0:pad, :] = jnp.zeros((pad, D_MODEL), F32)
        hb[pad:POOL_HIST_ROWS, :] = hist_ref[sq]
        st_ref[h0:h0 + POOL_HIST_ROWS, :] = _pool_rows(
            x_ref[r0:r0 + t_len, :], hb[...], g_ref[...], pw_ref, sc_ref, o_ref, r0, first_pos)


def _pool_specs(o):
    return [pl.BlockSpec((None, len(POOL_SIZES), POOL_GROUP, POOL_GROUP), lambda *a: (o, 0, 0, 0)),
            pl.BlockSpec((None, 1, D_MODEL), lambda *a: (o, 0, 0))]


def _mixer_c(x, hist, g, pool_w, scale, o, *, nb, t_len, first_pos):
    n = x.shape[0]
    whole = lambda a: pl.BlockSpec(a.shape, lambda i: (0,) * a.ndim)
    g2 = g.reshape(1, D_MODEL)
    st_shape = jax.ShapeDtypeStruct((nb * POOL_HIST_ROWS, D_MODEL), F32)
    return pl.pallas_call(
        functools.partial(_mixer_c_body, nb=nb, t_len=t_len, first_pos=first_pos),
        out_shape=(jax.ShapeDtypeStruct((n, D_MODEL), F32), st_shape),
        grid=(1,),
        in_specs=[whole(x), whole(hist), whole(g2)] + _pool_specs(o),
        out_specs=(whole(x), whole(st_shape)),
        scratch_shapes=[pltpu.VMEM((nb, POOL_HIST_ROWS, D_MODEL), F32)],
        compiler_params=pltpu.CompilerParams(dimension_semantics=("arbitrary",)),
        name="mixer_c",
    )(x, hist, g2, pool_w, scale.reshape(scale.shape[0], 1, D_MODEL))


def _ffn_pool_body(x_ref, g1_ref, wg_ref, wu_ref, wd_ref, g2_ref, pw_ref, sc_ref,
                   o_ref, st_ref, a_ref, carry_ref, *, tm, sub, tiles_per_seq):
    i = pl.program_id(0)
    t = i % tiles_per_seq

    @pl.when(i == 0)
    def _():
        carry_ref[...] = jnp.zeros_like(carry_ref)

    hist = jnp.where(t == 0, 0.0, carry_ref[...])
    g1 = g1_ref[...]
    g2 = g2_ref[...]
    blocks = [_swiglu_rows(x_ref[r0:r0 + sub, :], g1, wg_ref, wu_ref, wd_ref, a_ref, r0)
              for r0 in range(0, tm, sub)]
    for s, xb in enumerate(blocks):
        hist = _pool_rows(xb, hist, g2, pw_ref, sc_ref, o_ref, s * sub, t * tm + s * sub)
    carry_ref[...] = hist
    st_ref[...] = hist


def _ffn_pool(x, g1, w16, g2, pool_w, scale, o, *, nb, t_len, tm, sub):
    n = x.shape[0]
    tiles_per_seq = t_len // tm
    whole = lambda a: pl.BlockSpec(a.shape, lambda i: (0, 0), pipeline_mode=pl.Buffered(1))
    vec = pl.BlockSpec((1, D_MODEL), lambda i: (0, 0))
    row = pl.BlockSpec((tm, D_MODEL), lambda i: (i, 0))
    est = 3 * D_MODEL * D_FF * 2 + 4 * tm * D_MODEL * 4 + tm * D_FF * 2 + 6 * tm * 1024 * 4
    return pl.pallas_call(
        functools.partial(_ffn_pool_body, tm=tm, sub=sub, tiles_per_seq=tiles_per_seq),
        out_shape=(jax.ShapeDtypeStruct((n, D_MODEL), F32),
                   jax.ShapeDtypeStruct((nb * POOL_HIST_ROWS, D_MODEL), F32)),
        grid=(n // tm,),
        in_specs=[row, vec, whole(w16[0]), whole(w16[1]), whole(w16[2]), vec] + _pool_specs(o),
        out_specs=(row, pl.BlockSpec((POOL_HIST_ROWS, D_MODEL), lambda i: (i // tiles_per_seq, 0))),
        scratch_shapes=[pltpu.VMEM((tm, D_FF), BF16), pltpu.VMEM((POOL_HIST_ROWS, D_MODEL), F32)],
        compiler_params=pltpu.CompilerParams(
            dimension_semantics=("arbitrary",), vmem_limit_bytes=_vmem_limit(est)),
        name="ffn_pool",
    )(x, g1.reshape(1, D_MODEL), *w16, g2.reshape(1, D_MODEL), pool_w,
      scale.reshape(scale.shape[0], 1, D_MODEL))


def kernel(x_prompt, x_sample, cache_attn_k, cache_attn_v, cache_conv, cache_pool, norm_g, ffn_w_gate, ffn_w_up, ffn_w_down, mix_w_in, mix_w_out, conv_w, attn_sinks, rel_bias_table, pool_w, pool_scale, final_norm_g):
    past_len = 1024
    nb, t_len, _ = x_prompt.shape
    nbs, ts_len, _ = x_sample.shape
    xp = x_prompt.reshape(nb * t_len, D_MODEL)
    xs = x_sample.reshape(nbs * ts_len, D_MODEL)
    tf = min(FFN_ROW_TILE, xp.shape[0])
    tq = min(MIXER_ROW_TILE, t_len)
    assert t_len % tf == 0
    ts = xs.shape[0]
    pw = pool_w.astype(BF16)
    assert ts_len <= CHUNK
    ffn_w = (ffn_w_gate, ffn_w_up, ffn_w_down)

    w16 ={0: (ffn_w_gate[0, 0].astype(BF16), ffn_w_up, ffn_w_down)}
    pending = list(range(1, 2 * DEPTH))
    mix16 = {}

    def ffn_weights(n):
        if n not in w16:
            pending.remove(n)
            w16[n] = tuple(a[n // 2, n % 2].astype(BF16) for a in ffn_w)
        return w16[n]

    def next_casts(n):
        if not pending or pending[0] <= n:
            return None, []
        m = pending.pop(0)
        return m, [(a, (m // 2, m % 2)) for a in ffn_w]

    def prompt_ffn(x, layer, j, final):
        n = 2 * layer + j
        m, cast = next_casts(n)
        if n == 0:
            cast = cast + [(a, (e,)) for e in range(mix_w_in.shape[0]) for a in (mix_w_in, mix_w_out)]
        y, *done = _ffn(x, norm_g[layer, 2 * j], ffn_weights(n), final_norm_g, final, tf, cast=cast)
        if m is not None:
            w16[m] = tuple(done[:3])
            done = done[3:]
        for e in range(len(done) // 2):
            mix16[e] = (done[2 * e], done[2 * e + 1])
        return y

    def sample_ffn(x, layer, j, final):
        return _ffn_stream(x, norm_g[layer, 2 * j], ffn_weights(2 * layer + j), final_norm_g, final)

    def last_rows(a, seqs, rows, keep, tail_shape):
        return a.reshape((seqs, rows) + tail_shape)[:, rows - keep:]

    outs = {name: [] for name in ("kp", "vp", "cp", "pp", "ks", "vs", "cs", "ps")}
    for layer in range(DEPTH):
        last = layer == DEPTH - 1
        if layer % 2 == 0:
            e = layer // 2
            xp = prompt_ffn(xp, layer, 0, False)
            xs = sample_ffn(xs, layer, 0, False)
            if e not in mix16:
                mix16[e] = (mix_w_in[e].astype(BF16), mix_w_out[e].astype(BF16))
            w_in, w_out = mix16[e]
            m, cast = next_casts(2 * layer)
            bias_t, sink_t, bias_r, sink_r = _attn_tables(rel_bias_table, attn_sinks[e], ts_len,
                                                          tq // CHUNK)
            xp, k, v, u, *done = _mixer_full(xp, norm_g[layer, 1], w_in, bias_t, sink_t, conv_w,
                                             w_out, e, nb=nb, t_len=t_len, tq=tq, cast=cast)
            if m is not None:
                w16[m] = tuple(done)
            keep = min(WINDOW, t_len)
            outs["kp"].append(last_rows(k, nb, WINDOW, keep, (N_KV_HEADS, HEAD_DIM)))
            outs["vp"].append(last_rows(v, nb, WINDOW, keep, (N_KV_HEADS, HEAD_DIM)))
            outs["cp"].append(last_rows(u, nb, CONV_HIST_ROWS, CONV_K - 1, (CONV_WIDTH,)))
            q, k, v, bg, u = _inproj(xs, norm_g[layer, 1], w_in, min(ROW_TILE, ts))
            k_hist = cache_attn_k[e].reshape(nbs * WINDOW, KV_WIDTH)
            v_hist = cache_attn_v[e].reshape(nbs * WINDOW, KV_WIDTH)
            xs = _mixer_ab(xs, q, k, v, k_hist, v_hist, bg, u, cache_conv[e], bias_r, sink_r,
                           conv_w, w_out, e, nb=nbs, t_len=ts_len)
            keep = min(WINDOW, ts_len)
            outs["ks"].append(last_rows(k, nbs, ts_len, keep, (N_KV_HEADS, HEAD_DIM)))
            outs["vs"].append(last_rows(v, nbs, ts_len, keep, (N_KV_HEADS, HEAD_DIM)))
            outs["cs"].append(last_rows(u, nbs, ts_len, CONV_K - 1, (CONV_WIDTH,)))
        else:
            o = layer // 2
            xp, st = _ffn_pool(xp, norm_g[layer, 0], ffn_weights(2 * layer), norm_g[layer, 1],
                               pw, pool_scale, o, nb=nb, t_len=t_len, tm=tf,
                               sub=tf // POOL_SUB_BLOCKS)
            outs["pp"].append(last_rows(st, nb, POOL_HIST_ROWS, POOL_MAX - 1, (D_MODEL,)))
            xs = sample_ffn(xs, layer, 0, False)
            xs, st = _mixer_c(xs, cache_pool[o], norm_g[layer, 1], pw, pool_scale, o,
                              nb=nbs, t_len=ts_len, first_pos=past_len)
            outs["ps"].append(last_rows(st, nbs, POOL_HIST_ROWS, POOL_MAX - 1, (D_MODEL,)))
        xp = prompt_ffn(xp, layer, 1, last)
        xs = sample_ffn(xs, layer, 1, last)
    stacked = {name: jnp.stack(v) for name, v in outs.items()}
    return (xp.reshape(nb, t_len, D_MODEL), xs.reshape(nbs, ts_len, D_MODEL),
            stacked["kp"], stacked["vp"], stacked["cp"], stacked["pp"],
            stacked["ks"], stacked["vs"], stacked["cs"], stacked["ps"])
```
